```python
import jax, jax.numpy as jnp
from jax import lax
import numpy as np

D_MODEL = 1024
BATCH = 32
SEQ = 2048
DEPTH = 4

CHUNK = 64
A_HEADS = 8
A_HEAD_DIM = 64
A_WIDTH = A_HEADS * A_HEAD_DIM
A_LEFT_CHUNKS = 8
A_BAND = (A_LEFT_CHUNKS + 1) * CHUNK
A_MAX_REL = 128
A_N_REL = 2 * A_MAX_REL + 1
CONV_CH = 512
CONV_WIDTH = 31
C_HEADS = 8
C_NOPE = 64
C_ROPE = 32
C_V = 64
C_QK = C_NOPE + C_ROPE
C_WIDTH = C_HEADS * C_V
C_Q_RANK = 384
C_KV_RANK = 256
ROPE_THETA = 10000.0
Q_BLOCK = 128
IN_OFFSETS = (A_WIDTH, 2 * A_WIDTH, 3 * A_WIDTH, 3 * A_WIDTH + 2 * CONV_CH, 3 * A_WIDTH + 2 * CONV_CH + C_Q_RANK, 3 * A_WIDTH + 2 * CONV_CH + C_Q_RANK + C_KV_RANK)
IN_COLS = 3 * A_WIDTH + 2 * CONV_CH + C_Q_RANK + C_KV_RANK + C_ROPE
N_BRANCH = 3
N_EXPERTS = 32
TOP_K = 4
D_EXPERT = 1024
SWIGLU_ALPHA = 1.702
SWIGLU_LIMIT = 7.0
MOE_BLOCK = 256
DN_ALPHA = (2 * DEPTH) ** 0.25
DN_BETA = (8 * DEPTH) ** -0.25
LN_EPS = 1e-5
RMS_EPS = 1e-6
NEG_INF = -1e30

kernel_name = 'hybrid_chunked_attn_conv_mla_moe_deepnorm'


def layer_norm(x, g, b):
    xf = x.astype(jnp.float32)
    mu = jnp.mean(xf, axis=-1, keepdims=True)
    var = jnp.mean(jnp.square(xf - mu), axis=-1, keepdims=True)
    return ((xf - mu) * lax.rsqrt(var + LN_EPS) * g + b).astype(x.dtype)


def rms_norm(x, g):
    xf = x.astype(jnp.float32)
    return (xf * lax.rsqrt(jnp.mean(jnp.square(xf), axis=-1, keepdims=True) + RMS_EPS) * g).astype(x.dtype)


def rope_cos_sin(positions):
    inv = ROPE_THETA ** (-jnp.arange(0, C_ROPE, 2, dtype=jnp.float32) / C_ROPE)
    ang = positions.astype(jnp.float32)[..., None] * inv
    return jnp.cos(ang), jnp.sin(ang)


def apply_rope(x, cos, sin):
    x1, x2 = jnp.split(x, 2, axis=-1)
    return jnp.concatenate([x1 * cos - x2 * sin, x2 * cos + x1 * sin], axis=-1).astype(x.dtype)


def chunked_relpos_attention(q, k, v, rel_table):
    B, S, H, hd = q.shape
    nc = S // CHUNK
    pad = A_LEFT_CHUNKS * CHUNK
    kp = jnp.pad(k, ((0, 0), (pad, 0), (0, 0), (0, 0)))
    vp = jnp.pad(v, ((0, 0), (pad, 0), (0, 0), (0, 0)))
    qc = q.reshape(B, nc, CHUNK, H, hd).transpose(1, 0, 2, 3, 4)
    band_off = jnp.arange(A_BAND) - pad
    rel = band_off[None, :] - jnp.arange(CHUNK)[:, None]
    rel_idx = jnp.clip(rel, -A_MAX_REL, A_MAX_REL) + A_MAX_REL
    bias = rel_table[:, rel_idx].astype(jnp.float32)
    scale = hd ** -0.5

    def one_chunk(args):
        c, q_blk = args
        start = c * CHUNK
        k_band = lax.dynamic_slice_in_dim(kp, start, A_BAND, axis=1)
        v_band = lax.dynamic_slice_in_dim(vp, start, A_BAND, axis=1)
        s = jnp.einsum('bqhd,bkhd->bhqk', q_blk, k_band, preferred_element_type=jnp.float32) * scale + bias
        valid = (start + band_off) >= 0
        s = jnp.where(valid, s, NEG_INF)
        p = jax.nn.softmax(s, axis=-1).astype(v.dtype)
        return jnp.einsum('bhqk,bkhd->bqhd', p, v_band)

    out = lax.map(one_chunk, (jnp.arange(nc), qc))
    return out.transpose(1, 0, 2, 3, 4).reshape(B, S, H * hd)


def conformer_conv(u, w_dw, b_dw, g_ln, b_ln, w_pw2):
    a, gate = jnp.split(u, 2, axis=-1)
    h = a * jax.nn.sigmoid(gate)
    h = lax.conv_general_dilated(h, w_dw[:, None, :], window_strides=(1,), padding=[(CONV_WIDTH - 1, 0)],
                                 dimension_numbers=('NWC', 'WIO', 'NWC'), feature_group_count=CONV_CH) + b_dw
    h = jax.nn.silu(layer_norm(h, g_ln, b_ln))
    return h @ w_pw2


def latent_attention(cq_raw, ckv_raw, kr_raw, g_q, g_kv, w_uq, w_ukv, cos, sin):
    B, S, _ = cq_raw.shape
    q = (rms_norm(cq_raw, g_q) @ w_uq).reshape(B, S, C_HEADS, C_QK)
    q_nope, q_rope = q[..., :C_NOPE], q[..., C_NOPE:]
    q_rope = apply_rope(q_rope, cos[:, :, None, :], sin[:, :, None, :])
    kv = (rms_norm(ckv_raw, g_kv) @ w_ukv).reshape(B, S, C_HEADS, C_NOPE + C_V)
    k_nope, v = kv[..., :C_NOPE], kv[..., C_NOPE:]
    k_rope = apply_rope(kr_raw, cos, sin)
    nqb = S // Q_BLOCK
    qn_b = q_nope.reshape(B, nqb, Q_BLOCK, C_HEADS, C_NOPE).transpose(1, 0, 2, 3, 4)
    qr_b = q_rope.reshape(B, nqb, Q_BLOCK, C_HEADS, C_ROPE).transpose(1, 0, 2, 3, 4)
    key_chunk = jnp.arange(S) // CHUNK
    scale = C_QK ** -0.5

    def one_block(args):
        i, qn, qr = args
        s = (jnp.einsum('bqhd,bkhd->bhqk', qn, k_nope, preferred_element_type=jnp.float32)
             + jnp.einsum('bqhd,bkd->bhqk', qr, k_rope, preferred_element_type=jnp.float32)) * scale
        q_chunk = (i * Q_BLOCK + jnp.arange(Q_BLOCK)) // CHUNK
        s = jnp.where(key_chunk[None, :] <= q_chunk[:, None], s, NEG_INF)
        p = jax.nn.softmax(s, axis=-1).astype(v.dtype)
        return jnp.einsum('bhqk,bkhd->bqhd', p, v)

    out = lax.map(one_block, (jnp.arange(nqb), qn_b, qr_b))
    return out.transpose(1, 0, 2, 3, 4).reshape(B, S, C_WIDTH)


def clamped_swiglu(u):
    glu, lin = jnp.split(u, 2, axis=-1)
    glu = jnp.minimum(glu, SWIGLU_LIMIT)
    lin = jnp.clip(lin, -SWIGLU_LIMIT, SWIGLU_LIMIT)
    return glu * jax.nn.sigmoid(SWIGLU_ALPHA * glu) * (lin + 1.0)


def moe_ffn(h, w_router, b_router, w1, b1, w2, b2):
    B, S, D = h.shape
    N = B * S
    M = N * TOP_K
    xf = h.reshape(N, D)
    logits = (xf @ w_router).astype(jnp.float32) + b_router.astype(jnp.float32)
    top_val, top_idx = lax.top_k(logits, TOP_K)
    gates = jax.nn.softmax(top_val, axis=-1)
    e_flat = top_idx.reshape(-1).astype(jnp.int32)
    g_flat = gates.reshape(-1)
    t_flat = jnp.arange(M, dtype=jnp.int32) // TOP_K
    e_sorted, order = lax.sort((e_flat, jnp.arange(M, dtype=jnp.int32)), num_keys=1, is_stable=True)
    t_sorted = t_flat[order]
    g_sorted = g_flat[order]
    counts = jnp.bincount(e_flat, length=N_EXPERTS)
    padded = (counts + MOE_BLOCK - 1) // MOE_BLOCK * MOE_BLOCK
    start = jnp.cumsum(counts) - counts
    pad_end = jnp.cumsum(padded)
    pad_start = pad_end - padded
    dest = pad_start[e_sorted] + (jnp.arange(M, dtype=jnp.int32) - start[e_sorted])
    n_rows = (M + N_EXPERTS * (MOE_BLOCK - 1) + MOE_BLOCK - 1) // MOE_BLOCK * MOE_BLOCK
    n_blocks = n_rows // MOE_BLOCK
    row_tok = jnp.full((n_rows,), N, jnp.int32).at[dest].set(t_sorted)
    row_gate = jnp.zeros((n_rows,), jnp.float32).at[dest].set(g_sorted)
    block_expert = jnp.minimum(jnp.searchsorted(pad_end, jnp.arange(n_blocks) * MOE_BLOCK, side='right'), N_EXPERTS - 1).astype(jnp.int32)
    x_pad = jnp.concatenate([xf, jnp.zeros((1, D), xf.dtype)], axis=0)

    def one_block(args):
        e, toks, gw = args
        u = x_pad[toks] @ w1[e] + b1[e]
        y = clamped_swiglu(u) @ w2[e] + b2[e]
        return y * gw[:, None].astype(y.dtype)

    y = lax.map(one_block, (block_expert, row_tok.reshape(n_blocks, MOE_BLOCK), row_gate.reshape(n_blocks, MOE_BLOCK)))
    out = jnp.zeros((N + 1, D), y.dtype).at[row_tok].add(y.reshape(n_rows, D))
    return out[:N].reshape(B, S, D).astype(h.dtype)


def setup_inputs(seed: int = 0) -> dict:
    key = jax.random.key(seed)
    ks = jax.random.split(key, 32)
    f32 = jnp.float32
    L, D = DEPTH, D_MODEL

    def w(k, shape, fan_in, scale=1.0):
        return jax.random.normal(k, shape, f32) * (scale * fan_in ** -0.5)

    def gain(k, shape):
        return 1.0 + 0.05 * jax.random.normal(k, shape, f32)

    def small(k, shape, s=0.02):
        return s * jax.random.normal(k, shape, f32)

    x = jax.random.normal(ks[0], (BATCH, SEQ, D), f32)
    offset = jax.random.randint(ks[1], (BATCH, 1), 0, 64, dtype=jnp.int32) * CHUNK
    positions = (offset + jnp.arange(SEQ, dtype=jnp.int32)[None, :]).astype(jnp.int32)
    return {
        'x': x,
        'positions': positions,
        'ln_in_g': gain(ks[2], (D,)),
        'ln_in_b': small(ks[3], (D,)),
        'w_in': w(ks[4], (L, D, IN_COLS), D),
        'w_gate': w(ks[5], (L, D, N_BRANCH * D), D),
        'b_gate': small(ks[6], (L, N_BRANCH * D), 0.01),
        'rel_bias': small(ks[7], (L, A_HEADS, A_N_REL), 0.2),
        'conv_w': w(ks[8], (L, CONV_WIDTH, CONV_CH), CONV_WIDTH),
        'conv_b': small(ks[9], (L, CONV_CH)),
        'conv_ln_g': gain(ks[10], (L, CONV_CH)),
        'conv_ln_b': small(ks[11], (L, CONV_CH)),
        'w_pw2': w(ks[12], (L, CONV_CH, D), CONV_CH),
        'q_norm_g': gain(ks[13], (L, C_Q_RANK)),
        'kv_norm_g': gain(ks[14], (L, C_KV_RANK)),
        'w_uq': w(ks[15], (L, C_Q_RANK, C_HEADS * C_QK), C_Q_RANK),
        'w_ukv': w(ks[16], (L, C_KV_RANK, C_HEADS * (C_NOPE + C_V)), C_KV_RANK),
        'w_oa': w(ks[17], (L, A_WIDTH, D), A_WIDTH),
        'w_oc': w(ks[18], (L, C_WIDTH, D), C_WIDTH),
        'w_out': w(ks[19], (L, D, D), D, DN_BETA),
        'ln1_g': gain(ks[20], (L, D)),
        'ln1_b': small(ks[21], (L, D)),
        'w_router': w(ks[22], (L, D, N_EXPERTS), D),
        'b_router': small(ks[23], (L, N_EXPERTS), 0.01),
        'w1': w(ks[24], (L, N_EXPERTS, D, 2 * D_EXPERT), D),
        'b1': small(ks[25], (L, N_EXPERTS, 2 * D_EXPERT)),
        'w2': w(ks[26], (L, N_EXPERTS, D_EXPERT, D), D_EXPERT, DN_BETA),
        'b2': small(ks[27], (L, N_EXPERTS, D)),
        'ln2_g': gain(ks[28], (L, D)),
        'ln2_b': small(ks[29], (L, D)),
    }


def reference(x, positions, ln_in_g, ln_in_b, w_in, w_gate, b_gate, rel_bias, conv_w, conv_b, conv_ln_g, conv_ln_b,
              w_pw2, q_norm_g, kv_norm_g, w_uq, w_ukv, w_oa, w_oc, w_out, ln1_g, ln1_b, w_router, b_router,
              w1, b1, w2, b2, ln2_g, ln2_b):
    B, S, D = x.shape
    cos, sin = rope_cos_sin(positions)
    h = layer_norm(x, ln_in_g, ln_in_b)
    for l in range(DEPTH):
        proj = h @ w_in[l]
        qa, ka, va, glu_in, cq, ckv, kr = jnp.split(proj, IN_OFFSETS, axis=-1)
        y_a = chunked_relpos_attention(qa.reshape(B, S, A_HEADS, A_HEAD_DIM), ka.reshape(B, S, A_HEADS, A_HEAD_DIM),
                                       va.reshape(B, S, A_HEADS, A_HEAD_DIM), rel_bias[l]) @ w_oa[l]
        y_b = conformer_conv(glu_in, conv_w[l], conv_b[l], conv_ln_g[l], conv_ln_b[l], w_pw2[l])
        y_c = latent_attention(cq, ckv, kr, q_norm_g[l], kv_norm_g[l], w_uq[l], w_ukv[l], cos, sin) @ w_oc[l]
        g_a, g_b, g_c = jnp.split(jax.nn.sigmoid(h @ w_gate[l] + b_gate[l]), N_BRANCH, axis=-1)
        mix = (g_a * y_a + g_b * y_b + g_c * y_c) @ w_out[l]
        h = layer_norm(DN_ALPHA * h + mix, ln1_g[l], ln1_b[l])
        ffn = moe_ffn(h, w_router[l], b_router[l], w1[l], b1[l], w2[l], b2[l])
        h = layer_norm(DN_ALPHA * h + ffn, ln2_g[l], ln2_b[l])
    return h
```

```python
import functools

import jax
import jax.numpy as jnp
from jax import lax
from jax.experimental import pallas as pl
from jax.experimental.pallas import tpu as pltpu
from jax.experimental.pallas import tpu_sc as plsc

f32 = jnp.float32
bf16 = jnp.bfloat16
i32 = jnp.int32

D_MODEL = 1024
DEPTH = 4
CHUNK = 64
A_HEADS = 8
A_HEAD_DIM = 64
A_WIDTH = A_HEADS * A_HEAD_DIM
A_LEFT = 8 * CHUNK
A_MAX_REL = 128
CONV_CH = 512
CONV_WIDTH = 31
C_HEADS = 8
C_NOPE = 64
C_ROPE = 32
C_V = 64
C_QK = C_NOPE + C_ROPE
C_Q_RANK = 384
C_KV_RANK = 256
ROPE_THETA = 10000.0
N_EXPERTS = 32
TOP_K = 4
D_EXPERT = 1024
SWIGLU_ALPHA = 1.702
SWIGLU_LIMIT = 7.0
DN_ALPHA = (2 * DEPTH) ** 0.25
LN_EPS = 1e-5
RMS_EPS = 1e-6
NEG_INF = -1e30

LANES = 128
A_QBLK = 2 * CHUNK
A_BAND = A_LEFT + A_QBLK
C_BLK = 256
C_SLOT = 256
CONV_BLK = 256
CONV_HALO = 32
MOE_BLK = 512
ROW_TILE = 512
MIX_TILE = 256
ROUTER_TILE = 512
HALF = D_MODEL // 2
QUART = D_MODEL // 4
SC_WINDOW = 128
VMEM_LIMIT = 56 * 1024 * 1024


def _cparams(*sem):
    return pltpu.CompilerParams(dimension_semantics=tuple(sem), vmem_limit_bytes=VMEM_LIMIT)


def _layer_norm(x, g, b):
    mu = jnp.mean(x, axis=-1, keepdims=True)
    xc = x - mu
    var = jnp.mean(xc * xc, axis=-1, keepdims=True)
    return xc * lax.rsqrt(var + LN_EPS) * g + b


def _sigmoid(x):
    return 1.0 / (1.0 + jnp.exp(-x))


def _pack2(a, b):
    ab = lax.bitcast_convert_type(a.astype(bf16).astype(f32), i32)
    bb = lax.bitcast_convert_type(b.astype(bf16).astype(f32), i32)
    return lax.shift_right_logical(ab, 16) | (bb & jnp.int32(-65536))


def _unpack_lo(w):
    return lax.bitcast_convert_type(lax.shift_left(w, 16), f32)


def _unpack_hi(w):
    return lax.bitcast_convert_type(w & jnp.int32(-65536), f32)


def _pack_row(y):
    return (_pack2(y[:, 0:QUART], y[:, 2 * QUART:3 * QUART]),
            _pack2(y[:, QUART:2 * QUART], y[:, 3 * QUART:4 * QUART]))


def _ln_in_kernel(x_ref, g_ref, b_ref, hf_ref, hb_ref):
    y = _layer_norm(x_ref[...], g_ref[...], b_ref[...])
    hf_ref[...] = y
    hb_ref[...] = y.astype(bf16)


def ln_in(x2d, g, b):
    n, d = x2d.shape
    row = pl.BlockSpec((ROW_TILE, d), lambda i: (i, 0))
    vec = pl.BlockSpec((1, d), lambda i: (0, 0))
    return pl.pallas_call(
        _ln_in_kernel, grid=(n // ROW_TILE,),
        in_specs=[row, vec, vec], out_specs=[row, row],
        out_shape=[jax.ShapeDtypeStruct((n, d), f32), jax.ShapeDtypeStruct((n, d), bf16)],
        compiler_params=_cparams("parallel"), name="ln_in",
    )(x2d, g.reshape(1, d), b.reshape(1, d))


def _proj_kernel(x_ref, w_ref, o_ref):
    o_ref[...] = jnp.dot(x_ref[...], w_ref[...], preferred_element_type=f32).astype(o_ref.dtype)


def proj(hb, w):
    n, d = hb.shape
    c = w.shape[1]
    return pl.pallas_call(
        _proj_kernel, grid=(n // ROW_TILE,),
        in_specs=[pl.BlockSpec((ROW_TILE, d), lambda i: (i, 0)), pl.BlockSpec((d, c), lambda i: (0, 0))],
        out_specs=pl.BlockSpec((ROW_TILE, c), lambda i: (i, 0)),
        out_shape=jax.ShapeDtypeStruct((n, c), bf16),
        compiler_params=_cparams("parallel"), name="proj",
    )(hb, w)


def _mla_prep_kernel(hb_ref, wcq_ref, wckv_ref, wkr_ref, gq_ref, gkv_ref, wuq_ref, wukv_ref, cos_ref, sin_ref,
                     qc_ref, kc_ref, vc_ref):
    hb = hb_ref[...]
    cosm = cos_ref[...]
    sinm = sin_ref[...]
    lane = lax.broadcasted_iota(i32, cosm.shape, 1)
    half = C_ROPE // 2
    scale = C_QK ** -0.5

    def rope(x):
        swapped = jnp.where(lane < half, pltpu.roll(x, LANES - half, 1), pltpu.roll(x, half, 1))
        return x * cosm + swapped * sinm

    def rms(x, g):
        return (x * lax.rsqrt(jnp.mean(x * x, axis=-1, keepdims=True) + RMS_EPS) * g).astype(bf16)

    cqn = rms(jnp.dot(hb, wcq_ref[...], preferred_element_type=f32), gq_ref[...])
    q = jnp.dot(cqn, wuq_ref[...], preferred_element_type=f32)
    for h in range(C_HEADS):
        lo = h * C_SLOT
        qc_ref[:, lo:lo + LANES] = (q[:, lo:lo + LANES] * scale).astype(bf16)
        qc_ref[:, lo + LANES:lo + C_SLOT] = (rope(q[:, lo + LANES:lo + C_SLOT]) * scale).astype(bf16)
    ckvn = rms(jnp.dot(hb, wckv_ref[...], preferred_element_type=f32), gkv_ref[...])
    kv = jnp.dot(ckvn, wukv_ref[...], preferred_element_type=f32)
    kr = rope(jnp.dot(hb, wkr_ref[...], preferred_element_type=f32)).astype(bf16)
    for j in range(C_HEADS // 2):
        lo = j * C_SLOT
        kc_ref[:, lo:lo + LANES] = kv[:, j * LANES:(j + 1) * LANES].astype(bf16)
        kc_ref[:, lo + LANES:lo + C_SLOT] = kr
    vc_ref[...] = kv[:, C_HEADS * C_NOPE:].astype(bf16)


def mla_prep(hb, wcq, wckv, wkr, gq, gkv, wuq, wukv, cosm, sinm):
    n, d = hb.shape
    t = ROW_TILE
    full = lambda a: pl.BlockSpec(a.shape, lambda i: (0,) * a.ndim)
    rowb = lambda c: pl.BlockSpec((t, c), lambda i: (i, 0))
    qw, kw, vw = C_HEADS * C_SLOT, (C_HEADS // 2) * C_SLOT, C_HEADS * C_V
    return pl.pallas_call(
        _mla_prep_kernel, grid=(n // t,),
        in_specs=[rowb(d), full(wcq), full(wckv), full(wkr), full(gq), full(gkv), full(wuq), full(wukv),
                  rowb(LANES), rowb(LANES)],
        out_specs=[rowb(qw), rowb(kw), rowb(vw)],
        out_shape=[jax.ShapeDtypeStruct((n, qw), bf16), jax.ShapeDtypeStruct((n, kw), bf16),
                   jax.ShapeDtypeStruct((n, vw), bf16)],
        compiler_params=_cparams("parallel"), name="mla_prep",
    )(hb, wcq, wckv, wkr, gq, gkv, wuq, wukv, cosm, sinm)


def _attn_a_kernel(q_ref, k_ref, v_ref, bias_ref, o_ref, kpad, vpad):
    qi = pl.program_id(1)
    seq = k_ref.shape[0]

    @pl.when(qi == 0)
    def _():
        kpad[0:A_LEFT, :] = jnp.zeros((A_LEFT, A_WIDTH), bf16)
        vpad[0:A_LEFT, :] = jnp.zeros((A_LEFT, A_WIDTH), bf16)
        kpad[A_LEFT:A_LEFT + seq, :] = k_ref[...]
        vpad[A_LEFT:A_LEFT + seq, :] = v_ref[...]

    start = pl.multiple_of(qi * A_QBLK, A_QBLK)
    col = lax.broadcasted_iota(i32, (2 * A_QBLK, A_BAND), 1)
    before_start = jnp.where(col + start >= A_LEFT, 0.0, NEG_INF)
    lane = lax.broadcasted_iota(i32, (A_QBLK, LANES), 1)
    scale = A_HEAD_DIM ** -0.5
    for j in range(A_HEADS // 2):
        cs = slice(j * LANES, (j + 1) * LANES)
        qp = q_ref[:, cs].astype(f32) * scale
        qs = jnp.concatenate([jnp.where(lane < A_HEAD_DIM, qp, 0.0), jnp.where(lane >= A_HEAD_DIM, qp, 0.0)],
                             axis=0).astype(bf16)
        kb = kpad[pl.ds(start, A_BAND), cs]
        vb = vpad[pl.ds(start, A_BAND), cs]
        s = lax.dot_general(qs, kb, (((1,), (1,)), ((), ())), preferred_element_type=f32)
        s = s + bias_ref[j] + before_start
        m = jnp.max(s, axis=-1, keepdims=True)
        p = jnp.exp(s - m)
        l = jnp.sum(p, axis=-1, keepdims=True)
        o = jnp.dot(p.astype(bf16), vb, preferred_element_type=f32) / l
        o_ref[:, cs] = jnp.where(lane < A_HEAD_DIM, o[:A_QBLK], o[A_QBLK:]).astype(bf16)


def attn_a(qkvg, bias, batch, seq):
    n = batch * seq
    nq = seq // A_QBLK
    return pl.pallas_call(
        _attn_a_kernel, grid=(batch, nq),
        in_specs=[pl.BlockSpec((A_QBLK, A_WIDTH), lambda b, i: (b * nq + i, 0)),
                  pl.BlockSpec((seq, A_WIDTH), lambda b, i: (b, 1)),
                  pl.BlockSpec((seq, A_WIDTH), lambda b, i: (b, 2)),
                  pl.BlockSpec(bias.shape, lambda b, i: (0, 0, 0))],
        out_specs=pl.BlockSpec((A_QBLK, A_WIDTH), lambda b, i: (b * nq + i, 0)),
        out_shape=jax.ShapeDtypeStruct((n, A_WIDTH), bf16),
        scratch_shapes=[pltpu.VMEM((A_LEFT + seq, A_WIDTH), bf16), pltpu.VMEM((A_LEFT + seq, A_WIDTH), bf16)],
        compiler_params=_cparams("parallel", "arbitrary"), name="attn_a",
    )(qkvg, qkvg, qkvg, bias)


def _conv_kernel(a_ref, g_ref, ap_ref, gp_ref, w_ref, cb_ref, lg_ref, lb_ref, o_ref, hs):
    i = pl.program_id(1)
    t = CONV_BLK
    tail = slice(t - CONV_HALO, t)
    prev = ap_ref[tail, :].astype(f32) * _sigmoid(gp_ref[tail, :].astype(f32))
    hs[0:CONV_HALO, :] = jnp.where(i > 0, prev, 0.0)
    hs[CONV_HALO:CONV_HALO + t, :] = a_ref[...].astype(f32) * _sigmoid(g_ref[...].astype(f32))
    first = CONV_HALO - CONV_WIDTH + 1
    acc = jnp.zeros((t, CONV_CH), f32) + cb_ref[...]
    for b in range(8):
        a_vals = [a for a in range(CONV_HALO // 8 + 1) if first <= 8 * a + b <= CONV_HALO]
        shifted = hs[b:b + 8 * max(a_vals) + t, :]
        for a in a_vals:
            tap = 8 * a + b - first
            acc = acc + shifted[8 * a:8 * a + t, :] * w_ref[tap:tap + 1, :]
    y = _layer_norm(acc, lg_ref[...], lb_ref[...])
    o_ref[...] = (y * _sigmoid(y)).astype(bf16)


def conv_module(qkvg, w_dw, b_dw, ln_g, ln_b, batch, seq):
    n = batch * seq
    nb = seq // CONV_BLK
    a_col = 3 * A_WIDTH // CONV_CH
    cur = lambda c: pl.BlockSpec((CONV_BLK, CONV_CH), lambda b, i: (b * nb + i, c))
    prv = lambda c: pl.BlockSpec((CONV_BLK, CONV_CH), lambda b, i: (b * nb + jnp.maximum(i - 1, 0), c))
    vec = pl.BlockSpec((1, CONV_CH), lambda b, i: (0, 0))
    return pl.pallas_call(
        _conv_kernel, grid=(batch, nb),
        in_specs=[cur(a_col), cur(a_col + 1), prv(a_col), prv(a_col + 1),
                  pl.BlockSpec((CONV_WIDTH, CONV_CH), lambda b, i: (0, 0)), vec, vec, vec],
        out_specs=pl.BlockSpec((CONV_BLK, CONV_CH), lambda b, i: (b * nb + i, 0)),
        out_shape=jax.ShapeDtypeStruct((n, CONV_CH), bf16),
        scratch_shapes=[pltpu.VMEM((CONV_HALO + CONV_BLK, CONV_CH), f32)],
        compiler_params=_cparams("parallel", "parallel"), name="conv_module",
    )(qkvg, qkvg, qkvg, qkvg, w_dw, b_dw.reshape(1, -1), ln_g.reshape(1, -1), ln_b.reshape(1, -1))


def _mla_attn_kernel(q_ref, k_ref, v_ref, o_ref):
    qi = pl.program_id(1)
    t = C_BLK
    row = lax.broadcasted_iota(i32, (2 * t, t), 0)
    col = lax.broadcasted_iota(i32, (2 * t, t), 1)
    diag_ok = (col // CHUNK) <= ((row % t) // CHUNK)
    lane = lax.broadcasted_iota(i32, (t, LANES), 1)
    for j in range(C_HEADS // 2):
        qs = jnp.concatenate([q_ref[:, (2 * j) * C_SLOT:(2 * j + 1) * C_SLOT],
                              q_ref[:, (2 * j + 1) * C_SLOT:(2 * j + 2) * C_SLOT]], axis=0)

        def step(kb, carry, masked):
            m, l, acc = carry
            k0 = pl.multiple_of(kb * t, t)
            kblk = k_ref[pl.ds(k0, t), j * C_SLOT:(j + 1) * C_SLOT]
            vblk = v_ref[pl.ds(k0, t), j * LANES:(j + 1) * LANES]
            s = lax.dot_general(qs, kblk, (((1,), (1,)), ((), ())), preferred_element_type=f32)
            if masked:
                s = jnp.where(diag_ok, s, NEG_INF)
            m_new = jnp.maximum(m, jnp.max(s, axis=-1, keepdims=True))
            a = jnp.exp(m - m_new)
            p = jnp.exp(s - m_new)
            l = a * l + jnp.sum(p, axis=-1, keepdims=True)
            acc = a * acc + jnp.dot(p.astype(bf16), vblk, preferred_element_type=f32)
            return m_new, l, acc

        init = (jnp.full((2 * t, 1), NEG_INF, f32), jnp.zeros((2 * t, 1), f32), jnp.zeros((2 * t, LANES), f32))
        carry = lax.fori_loop(0, qi, lambda kb, c: step(kb, c, False), init)
        _, l, acc = step(qi, carry, True)
        o = acc / l
        o_ref[:, j * LANES:(j + 1) * LANES] = jnp.where(lane < C_V, o[:t], o[t:]).astype(bf16)


def mla_attn(qc, kc, vc, batch, seq):
    n = batch * seq
    nq = seq // C_BLK
    return pl.pallas_call(
        _mla_attn_kernel, grid=(batch, nq),
        in_specs=[pl.BlockSpec((C_BLK, qc.shape[1]), lambda b, i: (b * nq + i, 0)),
                  pl.BlockSpec((seq, kc.shape[1]), lambda b, i: (b, 0)),
                  pl.BlockSpec((seq, vc.shape[1]), lambda b, i: (b, 0))],
        out_specs=pl.BlockSpec((C_BLK, C_HEADS * C_V), lambda b, i: (b * nq + i, 0)),
        out_shape=jax.ShapeDtypeStruct((n, C_HEADS * C_V), bf16),
        compiler_params=_cparams("parallel", "parallel"), name="mla_attn",
    )(qc, kc, vc)


def _mix_kernel(ya_ref, cb_ref, yc_ref, hb_ref, hf_ref, woa_ref, wpw_ref, woc_ref, wg_ref, bg_ref, wout_ref,
                g_ref, b_ref, of_ref, lo_ref, hi_ref):
    d = D_MODEL
    gates = _sigmoid(jnp.dot(hb_ref[...], wg_ref[...], preferred_element_type=f32) + bg_ref[...])
    mix = (gates[:, 0:d] * jnp.dot(ya_ref[...], woa_ref[...], preferred_element_type=f32)
           + gates[:, d:2 * d] * jnp.dot(cb_ref[...], wpw_ref[...], preferred_element_type=f32)
           + gates[:, 2 * d:3 * d] * jnp.dot(yc_ref[...], woc_ref[...], preferred_element_type=f32))
    z = DN_ALPHA * hf_ref[...] + jnp.dot(mix.astype(bf16), wout_ref[...], preferred_element_type=f32)
    y = _layer_norm(z, g_ref[...], b_ref[...])
    of_ref[...] = y
    lo, hi = _pack_row(y)
    lo_ref[...] = lo
    hi_ref[...] = hi


def mix_layer(ya, cb, yc, hb, hf, woa, wpw, woc, wg, bg, wout, g, b):
    n, d = hf.shape
    t = MIX_TILE
    full = lambda a: pl.BlockSpec(a.shape, lambda i: (0,) * a.ndim)
    rowb = lambda c: pl.BlockSpec((t, c), lambda i: (i, 0))
    return pl.pallas_call(
        _mix_kernel, grid=(n // t,),
        in_specs=[rowb(ya.shape[1]), rowb(cb.shape[1]), rowb(yc.shape[1]), rowb(d), rowb(d),
                  full(woa), full(wpw), full(woc), full(wg), full(bg), full(wout), full(g), full(b)],
        out_specs=[rowb(d), rowb(QUART), rowb(QUART)],
        out_shape=[jax.ShapeDtypeStruct((n, d), f32), jax.ShapeDtypeStruct((n, QUART), i32),
                   jax.ShapeDtypeStruct((n, QUART), i32)],
        compiler_params=_cparams("parallel"), name="mix_layer",
    )(ya, cb, yc, hb, hf, woa, wpw, woc, wg, bg, wout, g, b)


def _router_kernel(h_ref, wr_ref, br_ref, idx_ref, gate_ref, rank_ref, cnt_ref, base):
    i = pl.program_id(0)
    t = ROUTER_TILE
    e = N_EXPERTS

    @pl.when(i == 0)
    def _():
        base[...] = jnp.zeros_like(base)

    logits = lax.dot_general(wr_ref[...], h_ref[...], (((1,), (1,)), ((), ())),
                             precision=lax.Precision.HIGHEST, preferred_element_type=f32) + br_ref[...]
    row = lax.broadcasted_iota(i32, (e, t), 0).astype(f32)
    vals, hots = [], []
    cur = logits
    for k in range(TOP_K):
        m = jnp.max(cur, axis=0, keepdims=True)
        first = jnp.min(jnp.where(cur == m, row, float(e)), axis=0, keepdims=True)
        hot = row == first
        cur = jnp.where(hot, -jnp.inf, cur)
        vals.append(m)
        hots.append(hot)
        idx_ref[k:k + 1, :] = first.astype(i32)
    ex = [jnp.exp(v - vals[0]) for v in vals]
    den = ex[0] + ex[1] + ex[2] + ex[3]
    for k in range(TOP_K):
        gate_ref[k:k + 1, :] = ex[k] / den
    onehot = jnp.concatenate([jnp.where(h, 1.0, 0.0) for h in hots], axis=0)
    r = lax.broadcasted_iota(i32, (t, t), 0)
    c = lax.broadcasted_iota(i32, (t, t), 1)
    upper = jnp.where(r <= c, 1.0, 0.0).astype(bf16)
    prefix = jnp.dot(onehot.astype(bf16), upper, preferred_element_type=f32)
    counts = jnp.sum(onehot, axis=1, keepdims=True)
    offset = base[:, 0:1]
    for k in range(TOP_K):
        sel = jnp.where(hots[k], prefix[k * e:(k + 1) * e, :] - 1.0 + offset, 0.0)
        rank_ref[k:k + 1, :] = jnp.sum(sel, axis=0, keepdims=True).astype(i32)
        offset = offset + counts[k * e:(k + 1) * e, :]
    base[...] = jnp.broadcast_to(offset, base.shape)
    cnt_ref[...] = base[...]


def router(hf, w_rt, b_r):
    n, d = hf.shape
    t = ROUTER_TILE
    tok = pl.BlockSpec((TOP_K, t), lambda i: (0, i))
    return pl.pallas_call(
        _router_kernel, grid=(n // t,),
        in_specs=[pl.BlockSpec((t, d), lambda i: (i, 0)), pl.BlockSpec((N_EXPERTS, d), lambda i: (0, 0)),
                  pl.BlockSpec((N_EXPERTS, 1), lambda i: (0, 0))],
        out_specs=[tok, tok, tok, pl.BlockSpec((N_EXPERTS, LANES), lambda i: (0, 0))],
        out_shape=[jax.ShapeDtypeStruct((TOP_K, n), i32), jax.ShapeDtypeStruct((TOP_K, n), f32),
                   jax.ShapeDtypeStruct((TOP_K, n), i32), jax.ShapeDtypeStruct((N_EXPERTS, LANES), f32)],
        scratch_shapes=[pltpu.VMEM((N_EXPERTS, LANES), f32)],
        compiler_params=_cparams("arbitrary"), name="router",
    )(hf, w_rt, b_r)


def _dest_kernel(idx_ref, rank_ref, start_ref, dest_ref):
    t = idx_ref.shape[1]
    row = lax.broadcasted_iota(i32, (N_EXPERTS, t), 0)
    for k in range(TOP_K):
        hot = row == idx_ref[k:k + 1, :]
        off = jnp.sum(jnp.where(hot, start_ref[...], 0.0), axis=0, keepdims=True)
        dest_ref[k:k + 1, :] = rank_ref[k:k + 1, :] + off.astype(i32)


def dest_rows(idx, rank, pad_start):
    n = idx.shape[1]
    t = ROUTER_TILE
    tok = pl.BlockSpec((TOP_K, t), lambda i: (0, i))
    return pl.pallas_call(
        _dest_kernel, grid=(n // t,),
        in_specs=[tok, tok, pl.BlockSpec((N_EXPERTS, 1), lambda i: (0, 0))],
        out_specs=tok, out_shape=jax.ShapeDtypeStruct((TOP_K, n), i32),
        compiler_params=_cparams("parallel"), name="dest_rows",
    )(idx, rank, pad_start)


def _sc_mesh():
    return plsc.VectorSubcoreMesh(core_axis_name="c", subcore_axis_name="s")


def sc_scatter_rows(x, dest, n_rows):
    n, d = x.shape
    kk = dest.shape[0]

    @functools.partial(pl.kernel, out_type=jax.ShapeDtypeStruct((n_rows, d), x.dtype), mesh=_sc_mesh(),
                       scratch_types=[])
    def k(x_hbm, i_hbm, o_hbm):
        def body(x_vmem, i_vmem):
            for j in range(kk):
                pltpu.sync_copy(x_vmem, o_hbm.at[i_vmem.at[j]])

        pltpu.emit_pipeline(
            body, grid=(n // SC_WINDOW,),
            in_specs=[pl.BlockSpec((SC_WINDOW, d), lambda i: (i, 0)),
                      pl.BlockSpec((kk, SC_WINDOW), lambda i: (0, i))],
            out_specs=[], core_axis_name=("c", "s"), dimension_semantics=(pltpu.PARALLEL,),
        )(x_hbm, i_hbm)

    return k(x, dest)


def sc_gather_rows(table, idx):
    m = idx.shape[1]
    d = table.shape[1]

    @functools.partial(pl.kernel, out_type=jax.ShapeDtypeStruct((m, d), table.dtype), mesh=_sc_mesh(),
                       scratch_types=[])
    def k(t_hbm, i_hbm, o_hbm):
        def body(i_vmem, o_vmem):
            pltpu.sync_copy(t_hbm.at[i_vmem.at[0]], o_vmem)

        pltpu.emit_pipeline(
            body, grid=(m // SC_WINDOW,),
            in_specs=[pl.BlockSpec((1, SC_WINDOW), lambda i: (0, i))],
            out_specs=[pl.BlockSpec((SC_WINDOW, d), lambda i: (i, 0))],
            core_axis_name=("c", "s"), dimension_semantics=(pltpu.PARALLEL,),
        )(i_hbm, o_hbm)

    return k(table, idx)


def _expert_kernel(be_ref, nu_ref, xlo_ref, xhi_ref, w1_ref, b1_ref, w2_ref, b2_ref, ylo_ref, yhi_ref):
    i = pl.program_id(0)

    @pl.when(i < nu_ref[0])
    def _():
        lo = xlo_ref[...]
        hi = xhi_ref[...]
        parts = (_unpack_lo(lo), _unpack_lo(hi), _unpack_hi(lo), _unpack_hi(hi))
        u = b1_ref[0]
        for p, xp in enumerate(parts):
            u = u + jnp.dot(xp.astype(bf16), w1_ref[0, p * QUART:(p + 1) * QUART, :], preferred_element_type=f32)
        glu = jnp.minimum(u[:, :D_EXPERT], SWIGLU_LIMIT)
        lin = jnp.clip(u[:, D_EXPERT:], -SWIGLU_LIMIT, SWIGLU_LIMIT)
        act = (glu * _sigmoid(SWIGLU_ALPHA * glu) * (lin + 1.0)).astype(bf16)
        y = jnp.dot(act, w2_ref[0], preferred_element_type=f32) + b2_ref[0]
        ylo, yhi = _pack_row(y)
        ylo_ref[...] = ylo
        yhi_ref[...] = yhi


def expert_ffn(block_expert, n_used, xs_lo, xs_hi, w1, b1, w2, b2):
    n_rows = xs_lo.shape[0]
    nb = n_rows // MOE_BLK
    rows = pl.BlockSpec((MOE_BLK, QUART), lambda i, be, nu: (jnp.minimum(i, nu[0] - 1), 0))
    wsel = lambda shape: pl.BlockSpec((1,) + shape, lambda i, be, nu: (be[i], 0, 0))
    out = jax.ShapeDtypeStruct((n_rows, QUART), i32)
    return pl.pallas_call(
        _expert_kernel,
        grid_spec=pltpu.PrefetchScalarGridSpec(
            num_scalar_prefetch=2, grid=(nb,),
            in_specs=[rows, rows, wsel((D_MODEL, 2 * D_EXPERT)), wsel((1, 2 * D_EXPERT)),
                      wsel((D_EXPERT, D_MODEL)), wsel((1, D_MODEL))],
            out_specs=[rows, rows]),
        out_shape=[out, out],
        compiler_params=_cparams("arbitrary"), name="expert_ffn",
    )(block_expert, n_used, xs_lo, xs_hi, w1, b1, w2, b2)


def _combine_kernel(ylo_ref, yhi_ref, gt_ref, hf_ref, g_ref, b_ref, of_ref, ob_ref):
    gt = gt_ref[...]
    parts = [None] * 4
    for k in range(TOP_K):
        gk = gt[:, k:k + 1]
        lo = ylo_ref[k]
        hi = yhi_ref[k]
        vals = (_unpack_lo(lo), _unpack_lo(hi), _unpack_hi(lo), _unpack_hi(hi))
        for p in range(4):
            parts[p] = gk * vals[p] if parts[p] is None else parts[p] + gk * vals[p]
    z = DN_ALPHA * hf_ref[...] + jnp.concatenate(parts, axis=1)
    y = _layer_norm(z, g_ref[...], b_ref[...])
    of_ref[...] = y
    ob_ref[...] = y.astype(bf16)


def combine(yk_lo, yk_hi, gates_t, hf, g, b):
    n, d = hf.shape
    t = MIX_TILE
    ysp = pl.BlockSpec((TOP_K, t, QUART), lambda i: (0, i, 0))
    rowb = lambda c: pl.BlockSpec((t, c), lambda i: (i, 0))
    vec = pl.BlockSpec((1, d), lambda i: (0, 0))
    return pl.pallas_call(
        _combine_kernel, grid=(n // t,),
        in_specs=[ysp, ysp, rowb(TOP_K), rowb(d), vec, vec],
        out_specs=[rowb(d), rowb(d)],
        out_shape=[jax.ShapeDtypeStruct((n, d), f32), jax.ShapeDtypeStruct((n, d), bf16)],
        compiler_params=_cparams("parallel"), name="combine",
    )(yk_lo, yk_hi, gates_t, hf, g, b)


def _rope_tables(positions):
    inv = ROPE_THETA ** (-jnp.arange(0, C_ROPE, 2, dtype=f32) / C_ROPE)
    ang = positions.reshape(-1).astype(f32)[:, None] * inv
    cos, sin = jnp.cos(ang), jnp.sin(ang)
    pad = jnp.zeros((ang.shape[0], LANES - C_ROPE), f32)
    return jnp.concatenate([cos, cos, pad], axis=1), jnp.concatenate([-sin, sin, pad], axis=1)


def _attn_a_bias(rel_table):
    r = jnp.arange(A_QBLK)[:, None]
    c = jnp.arange(A_BAND)[None, :]
    rel = c - A_LEFT - r
    own = c - CHUNK * (r // CHUNK)
    valid = (own >= 0) & (own < A_LEFT + CHUNK)
    bias = rel_table[:, jnp.clip(rel, -A_MAX_REL, A_MAX_REL) + A_MAX_REL]
    bias = jnp.where(valid[None], bias, NEG_INF).astype(f32)
    return bias.reshape(A_HEADS // 2, 2 * A_QBLK, A_BAND)


def _layout_w_uq(w_uq):
    w = w_uq.reshape(C_Q_RANK, C_HEADS, C_QK)
    out = jnp.zeros((C_Q_RANK, C_HEADS, C_SLOT), w_uq.dtype)
    for h in range(C_HEADS):
        off = (h % 2) * C_NOPE
        out = out.at[:, h, off:off + C_NOPE].set(w[:, h, :C_NOPE])
        out = out.at[:, h, LANES:LANES + C_ROPE].set(w[:, h, C_NOPE:])
    return out.reshape(C_Q_RANK, C_HEADS * C_SLOT)


def _layout_w_ukv(w_ukv):
    w = w_ukv.reshape(C_KV_RANK, C_HEADS, C_NOPE + C_V)
    return jnp.concatenate([w[:, :, :C_NOPE].reshape(C_KV_RANK, -1), w[:, :, C_NOPE:].reshape(C_KV_RANK, -1)], axis=1)


def _moe(hf, lo, hi, w_rt, b_r, w1, b1, w2, b2, g2, bb2):
    n = hf.shape[0]
    n_rows = n * TOP_K + N_EXPERTS * MOE_BLK
    nb = n_rows // MOE_BLK
    idx, gates, rank, cnt = router(hf, w_rt, b_r)
    counts = cnt[:, 0].astype(i32)
    padded = (counts + MOE_BLK - 1) // MOE_BLK * MOE_BLK
    pad_end = jnp.cumsum(padded)
    pad_start = (pad_end - padded).astype(f32).reshape(N_EXPERTS, 1)
    block_expert = jnp.minimum(
        jnp.searchsorted(pad_end, jnp.arange(nb, dtype=i32) * MOE_BLK, side="right"), N_EXPERTS - 1).astype(i32)
    n_used = (pad_end[-1:] // MOE_BLK).astype(i32)
    dest = dest_rows(idx, rank, pad_start)
    xs_lo = sc_scatter_rows(lo, dest, n_rows)
    xs_hi = sc_scatter_rows(hi, dest, n_rows)
    ys_lo, ys_hi = expert_ffn(block_expert, n_used, xs_lo, xs_hi, w1, b1, w2, b2)
    flat = dest.reshape(1, TOP_K * n)
    yk_lo = sc_gather_rows(ys_lo, flat)
    yk_hi = sc_gather_rows(ys_hi, flat)
    return combine(yk_lo.reshape(TOP_K, n, QUART), yk_hi.reshape(TOP_K, n, QUART), gates.T, hf, g2, bb2)


def kernel(x, positions, ln_in_g, ln_in_b, w_in, w_gate, b_gate, rel_bias, conv_w, conv_b, conv_ln_g, conv_ln_b, w_pw2, q_norm_g, kv_norm_g, w_uq, w_ukv, w_oa, w_oc, w_out, ln1_g, ln1_b, w_router, b_router, w1, b1, w2, b2, ln2_g, ln2_b):
    batch, seq, d = x.shape
    n = batch * seq
    cosm, sinm = _rope_tables(positions)
    hf, hb = ln_in(x.reshape(n, d), ln_in_g, ln_in_b)
    ab = 3 * A_WIDTH + 2 * CONV_CH
    row1 = lambda v: v.reshape(1, -1)
    for l in range(DEPTH):
        w_ab = w_in[l, :, :ab].astype(bf16)
        w_cq = w_in[l, :, ab:ab + C_Q_RANK].astype(bf16)
        w_ckv = w_in[l, :, ab + C_Q_RANK:ab + C_Q_RANK + C_KV_RANK].astype(bf16)
        w_kr = jnp.pad(w_in[l, :, ab + C_Q_RANK + C_KV_RANK:], ((0, 0), (0, LANES - C_ROPE))).astype(bf16)
        qkvg = proj(hb, w_ab)
        qc, kc, vc = mla_prep(hb, w_cq, w_ckv, w_kr, row1(q_norm_g[l]), row1(kv_norm_g[l]),
                              _layout_w_uq(w_uq[l]).astype(bf16), _layout_w_ukv(w_ukv[l]).astype(bf16), cosm, sinm)
        ya = attn_a(qkvg, _attn_a_bias(rel_bias[l]), batch, seq)
        cb = conv_module(qkvg, conv_w[l], conv_b[l], conv_ln_g[l], conv_ln_b[l], batch, seq)
        yc = mla_attn(qc, kc, vc, batch, seq)
        h1f, lo, hi = mix_layer(ya, cb, yc, hb, hf, w_oa[l].astype(bf16), w_pw2[l].astype(bf16),
                                w_oc[l].astype(bf16), w_gate[l].astype(bf16), row1(b_gate[l]),
                                w_out[l].astype(bf16), row1(ln1_g[l]), row1(ln1_b[l]))
        hf, hb = _moe(h1f, lo, hi, w_router[l].T, b_router[l].reshape(N_EXPERTS, 1),
                      w1[l].astype(bf16), b1[l].reshape(N_EXPERTS, 1, -1), w2[l].astype(bf16),
                      b2[l].reshape(N_EXPERTS, 1, -1), row1(ln2_g[l]), row1(ln2_b[l]))
    return hf.reshape(batch, seq, d)
```

```python
import functools

import jax
import jax.numpy as jnp
from jax import lax
from jax.experimental import pallas as pl
from jax.experimental.pallas import tpu as pltpu
from jax.experimental.pallas import tpu_sc as plsc

f32 = jnp.float32
bf16 = jnp.bfloat16
i32 = jnp.int32

D_MODEL = 1024
DEPTH = 4
CHUNK = 64
A_HEADS = 8
A_HEAD_DIM = 64
A_WIDTH = A_HEADS * A_HEAD_DIM
A_LEFT = 8 * CHUNK
A_MAX_REL = 128
CONV_CH = 512
CONV_WIDTH = 31
C_HEADS = 8
C_NOPE = 64
C_ROPE = 32
C_V = 64
C_QK = C_NOPE + C_ROPE
C_Q_RANK = 384
C_KV_RANK = 256
ROPE_THETA = 10000.0
N_EXPERTS = 32
TOP_K = 4
D_EXPERT = 1024
SWIGLU_ALPHA = 1.702
SWIGLU_LIMIT = 7.0
DN_ALPHA = (2 * DEPTH) ** 0.25
LN_EPS = 1e-5
RMS_EPS = 1e-6
NEG_INF = -1e30
LOG2E = 1.4426950408889634

LANES = 128
A_QBLK = 2 * CHUNK
A_BAND = A_LEFT + A_QBLK
C_BLK = 256
C_SLOT = 256
CONV_BLK = 256
CONV_HALO = 32
MOE_BLK = 512
ROW_TILE = 512
MIX_TILE = 256
ROUTER_TILE = 512
HALF = D_MODEL // 2
QUART = D_MODEL // 4
SC_WINDOW = 128
VMEM_LIMIT = 56 * 1024 * 1024


def _cparams(*sem):
    return pltpu.CompilerParams(dimension_semantics=tuple(sem), vmem_limit_bytes=VMEM_LIMIT)


def _layer_norm(x, g, b):
    mu = jnp.mean(x, axis=-1, keepdims=True)
    xc = x - mu
    var = jnp.mean(xc * xc, axis=-1, keepdims=True)
    return xc * lax.rsqrt(var + LN_EPS) * g + b


def _sigmoid(x):
    return 1.0 / (1.0 + jnp.exp(-x))


def _pack2(a, b):
    ab = lax.bitcast_convert_type(a.astype(bf16).astype(f32), i32)
    bb = lax.bitcast_convert_type(b.astype(bf16).astype(f32), i32)
    return lax.shift_right_logical(ab, 16) | (bb & jnp.int32(-65536))


def _unpack_lo(w):
    return lax.bitcast_convert_type(lax.shift_left(w, 16), f32)


def _unpack_hi(w):
    return lax.bitcast_convert_type(w & jnp.int32(-65536), f32)


def _pack_row(y):
    return (_pack2(y[:, 0:QUART], y[:, 2 * QUART:3 * QUART]),
            _pack2(y[:, QUART:2 * QUART], y[:, 3 * QUART:4 * QUART]))


def _ln_in_kernel(x_ref, g_ref, b_ref, hf_ref, hb_ref):
    y = _layer_norm(x_ref[...], g_ref[...], b_ref[...])
    hf_ref[...] = y
    hb_ref[...] = y.astype(bf16)


def ln_in(x2d, g, b):
    n, d = x2d.shape
    row = pl.BlockSpec((ROW_TILE, d), lambda i: (i, 0))
    vec = pl.BlockSpec((1, d), lambda i: (0, 0))
    return pl.pallas_call(
        _ln_in_kernel, grid=(n // ROW_TILE,),
        in_specs=[row, vec, vec], out_specs=[row, row],
        out_shape=[jax.ShapeDtypeStruct((n, d), f32), jax.ShapeDtypeStruct((n, d), bf16)],
        compiler_params=_cparams("parallel"), name="ln_in",
    )(x2d, g.reshape(1, d), b.reshape(1, d))


def _proj_kernel(x_ref, w_ref, o_ref):
    o_ref[...] = jnp.dot(x_ref[...], w_ref[...], preferred_element_type=f32).astype(o_ref.dtype)


def proj(hb, w):
    n, d = hb.shape
    c = w.shape[1]
    return pl.pallas_call(
        _proj_kernel, grid=(n // ROW_TILE,),
        in_specs=[pl.BlockSpec((ROW_TILE, d), lambda i: (i, 0)), pl.BlockSpec((d, c), lambda i: (0, 0))],
        out_specs=pl.BlockSpec((ROW_TILE, c), lambda i: (i, 0)),
        out_shape=jax.ShapeDtypeStruct((n, c), bf16),
        compiler_params=_cparams("parallel"), name="proj",
    )(hb, w)


def _mla_prep_kernel(hb_ref, wcq_ref, wckv_ref, wkr_ref, gq_ref, gkv_ref, wuq_ref, wukv_ref, cos_ref, sin_ref,
                     qc_ref, kc_ref, vc_ref):
    hb = hb_ref[...]
    cosm = cos_ref[...]
    sinm = sin_ref[...]
    lane = lax.broadcasted_iota(i32, cosm.shape, 1)
    half = C_ROPE // 2
    scale = C_QK ** -0.5 * LOG2E

    def rope(x):
        swapped = jnp.where(lane < half, pltpu.roll(x, LANES - half, 1), pltpu.roll(x, half, 1))
        return x * cosm + swapped * sinm

    def rms(x, g):
        return (x * lax.rsqrt(jnp.mean(x * x, axis=-1, keepdims=True) + RMS_EPS) * g).astype(bf16)

    cqn = rms(jnp.dot(hb, wcq_ref[...], preferred_element_type=f32), gq_ref[...])
    q = jnp.dot(cqn, wuq_ref[...], preferred_element_type=f32)
    for h in range(C_HEADS):
        lo = h * C_SLOT
        qc_ref[:, lo:lo + LANES] = (q[:, lo:lo + LANES] * scale).astype(bf16)
        qc_ref[:, lo + LANES:lo + C_SLOT] = (rope(q[:, lo + LANES:lo + C_SLOT]) * scale).astype(bf16)
    ckvn = rms(jnp.dot(hb, wckv_ref[...], preferred_element_type=f32), gkv_ref[...])
    kv = jnp.dot(ckvn, wukv_ref[...], preferred_element_type=f32)
    kr = rope(jnp.dot(hb, wkr_ref[...], preferred_element_type=f32)).astype(bf16)
    for j in range(C_HEADS // 2):
        lo = j * C_SLOT
        kc_ref[:, lo:lo + LANES] = kv[:, j * LANES:(j + 1) * LANES].astype(bf16)
        kc_ref[:, lo + LANES:lo + C_SLOT] = kr
    vc_ref[...] = kv[:, C_HEADS * C_NOPE:].astype(bf16)


def mla_prep(hb, wcq, wckv, wkr, gq, gkv, wuq, wukv, cosm, sinm):
    n, d = hb.shape
    t = ROW_TILE
    full = lambda a: pl.BlockSpec(a.shape, lambda i: (0,) * a.ndim)
    rowb = lambda c: pl.BlockSpec((t, c), lambda i: (i, 0))
    qw, kw, vw = C_HEADS * C_SLOT, (C_HEADS // 2) * C_SLOT, C_HEADS * C_V
    return pl.pallas_call(
        _mla_prep_kernel, grid=(n // t,),
        in_specs=[rowb(d), full(wcq), full(wckv), full(wkr), full(gq), full(gkv), full(wuq), full(wukv),
                  rowb(LANES), rowb(LANES)],
        out_specs=[rowb(qw), rowb(kw), rowb(vw)],
        out_shape=[jax.ShapeDtypeStruct((n, qw), bf16), jax.ShapeDtypeStruct((n, kw), bf16),
                   jax.ShapeDtypeStruct((n, vw), bf16)],
        compiler_params=_cparams("parallel"), name="mla_prep",
    )(hb, wcq, wckv, wkr, gq, gkv, wuq, wukv, cosm, sinm)


def _attn_a_kernel(q_ref, k_ref, v_ref, bias_ref, o_ref, kpad, vpad):
    qi = pl.program_id(1)
    seq = k_ref.shape[0]

    @pl.when(qi == 0)
    def _():
        kpad[0:A_LEFT, :] = jnp.zeros((A_LEFT, A_WIDTH), bf16)
        vpad[0:A_LEFT, :] = jnp.zeros((A_LEFT, A_WIDTH), bf16)
        kpad[A_LEFT:A_LEFT + seq, :] = k_ref[...]
        vpad[A_LEFT:A_LEFT + seq, :] = v_ref[...]

    start = pl.multiple_of(qi * A_QBLK, A_QBLK)
    col = lax.broadcasted_iota(i32, (2 * A_QBLK, A_BAND), 1)
    before_start = jnp.where(col + start >= A_LEFT, 0.0, NEG_INF)
    lane = lax.broadcasted_iota(i32, (A_QBLK, LANES), 1)
    scale = A_HEAD_DIM ** -0.5
    for j in range(A_HEADS // 2):
        cs = slice(j * LANES, (j + 1) * LANES)
        qp = q_ref[:, cs].astype(f32) * scale
        qs = jnp.concatenate([jnp.where(lane < A_HEAD_DIM, qp, 0.0), jnp.where(lane >= A_HEAD_DIM, qp, 0.0)],
                             axis=0).astype(bf16)
        kb = kpad[pl.ds(start, A_BAND), cs]
        vb = vpad[pl.ds(start, A_BAND), cs]
        s = lax.dot_general(qs, kb, (((1,), (1,)), ((), ())), preferred_element_type=f32)
        s = s + bias_ref[j] + before_start
        m = jnp.max(s, axis=-1, keepdims=True)
        p = jnp.exp(s - m)
        l = jnp.sum(p, axis=-1, keepdims=True)
        o = jnp.dot(p.astype(bf16), vb, preferred_element_type=f32) / l
        o_ref[:, cs] = jnp.where(lane < A_HEAD_DIM, o[:A_QBLK], o[A_QBLK:]).astype(bf16)


def attn_a(qkvg, bias, batch, seq):
    n = batch * seq
    nq = seq // A_QBLK
    return pl.pallas_call(
        _attn_a_kernel, grid=(batch, nq),
        in_specs=[pl.BlockSpec((A_QBLK, A_WIDTH), lambda b, i: (b * nq + i, 0)),
                  pl.BlockSpec((seq, A_WIDTH), lambda b, i: (b, 1)),
                  pl.BlockSpec((seq, A_WIDTH), lambda b, i: (b, 2)),
                  pl.BlockSpec(bias.shape, lambda b, i: (0, 0, 0))],
        out_specs=pl.BlockSpec((A_QBLK, A_WIDTH), lambda b, i: (b * nq + i, 0)),
        out_shape=jax.ShapeDtypeStruct((n, A_WIDTH), bf16),
        scratch_shapes=[pltpu.VMEM((A_LEFT + seq, A_WIDTH), bf16), pltpu.VMEM((A_LEFT + seq, A_WIDTH), bf16)],
        compiler_params=_cparams("parallel", "arbitrary"), name="attn_a",
    )(qkvg, qkvg, qkvg, bias)


def _conv_kernel(a_ref, g_ref, ap_ref, gp_ref, w_ref, cb_ref, lg_ref, lb_ref, o_ref, hs):
    i = pl.program_id(1)
    t = CONV_BLK
    tail = slice(t - CONV_HALO, t)
    prev = ap_ref[tail, :].astype(f32) * _sigmoid(gp_ref[tail, :].astype(f32))
    hs[0:CONV_HALO, :] = jnp.where(i > 0, prev, 0.0)
    hs[CONV_HALO:CONV_HALO + t, :] = a_ref[...].astype(f32) * _sigmoid(g_ref[...].astype(f32))
    first = CONV_HALO - CONV_WIDTH + 1
    acc = jnp.zeros((t, CONV_CH), f32) + cb_ref[...]
    for b in range(8):
        a_vals = [a for a in range(CONV_HALO // 8 + 1) if first <= 8 * a + b <= CONV_HALO]
        shifted = hs[b:b + 8 * max(a_vals) + t, :]
        for a in a_vals:
            tap = 8 * a + b - first
            acc = acc + shifted[8 * a:8 * a + t, :] * w_ref[tap:tap + 1, :]
    y = _layer_norm(acc, lg_ref[...], lb_ref[...])
    o_ref[...] = (y * _sigmoid(y)).astype(bf16)


def conv_module(qkvg, w_dw, b_dw, ln_g, ln_b, batch, seq):
    n = batch * seq
    nb = seq // CONV_BLK
    a_col = 3 * A_WIDTH // CONV_CH
    cur = lambda c: pl.BlockSpec((CONV_BLK, CONV_CH), lambda b, i: (b * nb + i, c))
    prv = lambda c: pl.BlockSpec((CONV_BLK, CONV_CH), lambda b, i: (b * nb + jnp.maximum(i - 1, 0), c))
    vec = pl.BlockSpec((1, CONV_CH), lambda b, i: (0, 0))
    return pl.pallas_call(
        _conv_kernel, grid=(batch, nb),
        in_specs=[cur(a_col), cur(a_col + 1), prv(a_col), prv(a_col + 1),
                  pl.BlockSpec((CONV_WIDTH, CONV_CH), lambda b, i: (0, 0)), vec, vec, vec],
        out_specs=pl.BlockSpec((CONV_BLK, CONV_CH), lambda b, i: (b * nb + i, 0)),
        out_shape=jax.ShapeDtypeStruct((n, CONV_CH), bf16),
        scratch_shapes=[pltpu.VMEM((CONV_HALO + CONV_BLK, CONV_CH), f32)],
        compiler_params=_cparams("parallel", "parallel"), name="conv_module",
    )(qkvg, qkvg, qkvg, qkvg, w_dw, b_dw.reshape(1, -1), ln_g.reshape(1, -1), ln_b.reshape(1, -1))


def _mla_attn_kernel(q_ref, k_ref, v_ref, o_ref, acc_ref):
    qi = pl.program_id(1)
    t = C_BLK
    pairs = C_HEADS // 2
    row = lax.broadcasted_iota(i32, (2 * t, t), 0)
    col = lax.broadcasted_iota(i32, (2 * t, t), 1)
    diag_ok = (col // CHUNK) <= ((row % t) // CHUNK)
    lane = lax.broadcasted_iota(i32, (t, LANES), 1)
    ones = jnp.ones((t, LANES), bf16)
    acc_ref[...] = jnp.zeros(acc_ref.shape, f32)

    def step(kb, ms, masked):
        k0 = pl.multiple_of(kb * t, t)
        new_ms = []
        for j in range(pairs):
            qs = jnp.concatenate([q_ref[:, (2 * j) * C_SLOT:(2 * j + 1) * C_SLOT],
                                  q_ref[:, (2 * j + 1) * C_SLOT:(2 * j + 2) * C_SLOT]], axis=0)
            kblk = k_ref[pl.ds(k0, t), j * C_SLOT:(j + 1) * C_SLOT]
            vext = jnp.concatenate([v_ref[pl.ds(k0, t), j * LANES:(j + 1) * LANES], ones], axis=1)
            s = lax.dot_general(qs, kblk, (((1,), (1,)), ((), ())), preferred_element_type=f32)
            if masked:
                s = jnp.where(diag_ok, s, NEG_INF)
            m_new = jnp.maximum(ms[j], jnp.max(s, axis=-1, keepdims=True))
            a = jnp.exp2(ms[j] - m_new)
            p = jnp.exp2(s - m_new).astype(bf16)
            acc_ref[j] = a * acc_ref[j] + jnp.dot(p, vext, preferred_element_type=f32)
            new_ms.append(m_new)
        return tuple(new_ms)

    init = tuple(jnp.full((2 * t, 1), NEG_INF, f32) for _ in range(pairs))
    ms = lax.fori_loop(0, qi, lambda kb, c: step(kb, c, False), init)
    step(qi, ms, True)
    for j in range(pairs):
        acc = acc_ref[j]
        o = acc[:, :LANES] / acc[:, LANES:]
        o_ref[:, j * LANES:(j + 1) * LANES] = jnp.where(lane < C_V, o[:t], o[t:]).astype(bf16)


def mla_attn(qc, kc, vc, batch, seq):
    n = batch * seq
    nq = seq // C_BLK
    return pl.pallas_call(
        _mla_attn_kernel, grid=(batch, nq),
        in_specs=[pl.BlockSpec((C_BLK, qc.shape[1]), lambda b, i: (b * nq + i, 0)),
                  pl.BlockSpec((seq, kc.shape[1]), lambda b, i: (b, 0)),
                  pl.BlockSpec((seq, vc.shape[1]), lambda b, i: (b, 0))],
        out_specs=pl.BlockSpec((C_BLK, C_HEADS * C_V), lambda b, i: (b * nq + i, 0)),
        out_shape=jax.ShapeDtypeStruct((n, C_HEADS * C_V), bf16),
        scratch_shapes=[pltpu.VMEM((C_HEADS // 2, 2 * C_BLK, 2 * LANES), f32)],
        compiler_params=_cparams("parallel", "parallel"), name="mla_attn",
    )(qc, kc, vc)


def _mix_kernel(ya_ref, cb_ref, yc_ref, hb_ref, hf_ref, woa_ref, wpw_ref, woc_ref, wg_ref, bg_ref, wout_ref,
                g_ref, b_ref, of_ref, lo_ref, hi_ref):
    d = D_MODEL
    gates = _sigmoid(jnp.dot(hb_ref[...], wg_ref[...], preferred_element_type=f32) + bg_ref[...])
    mix = (gates[:, 0:d] * jnp.dot(ya_ref[...], woa_ref[...], preferred_element_type=f32)
           + gates[:, d:2 * d] * jnp.dot(cb_ref[...], wpw_ref[...], preferred_element_type=f32)
           + gates[:, 2 * d:3 * d] * jnp.dot(yc_ref[...], woc_ref[...], preferred_element_type=f32))
    z = DN_ALPHA * hf_ref[...] + jnp.dot(mix.astype(bf16), wout_ref[...], preferred_element_type=f32)
    y = _layer_norm(z, g_ref[...], b_ref[...])
    of_ref[...] = y
    lo, hi = _pack_row(y)
    lo_ref[...] = lo
    hi_ref[...] = hi


def mix_layer(ya, cb, yc, hb, hf, woa, wpw, woc, wg, bg, wout, g, b):
    n, d = hf.shape
    t = MIX_TILE
    full = lambda a: pl.BlockSpec(a.shape, lambda i: (0,) * a.ndim)
    rowb = lambda c: pl.BlockSpec((t, c), lambda i: (i, 0))
    return pl.pallas_call(
        _mix_kernel, grid=(n // t,),
        in_specs=[rowb(ya.shape[1]), rowb(cb.shape[1]), rowb(yc.shape[1]), rowb(d), rowb(d),
                  full(woa), full(wpw), full(woc), full(wg), full(bg), full(wout), full(g), full(b)],
        out_specs=[rowb(d), rowb(QUART), rowb(QUART)],
        out_shape=[jax.ShapeDtypeStruct((n, d), f32), jax.ShapeDtypeStruct((n, QUART), i32),
                   jax.ShapeDtypeStruct((n, QUART), i32)],
        compiler_params=_cparams("parallel"), name="mix_layer",
    )(ya, cb, yc, hb, hf, woa, wpw, woc, wg, bg, wout, g, b)


def _router_kernel(h_ref, wr_ref, br_ref, idx_ref, gate_ref, rank_ref, cnt_ref, base):
    i = pl.program_id(0)
    t = ROUTER_TILE
    e = N_EXPERTS

    @pl.when(i == 0)
    def _():
        base[...] = jnp.zeros_like(base)

    logits = lax.dot_general(wr_ref[...], h_ref[...], (((1,), (1,)), ((), ())),
                             precision=lax.Precision.HIGHEST, preferred_element_type=f32) + br_ref[...]
    row = lax.broadcasted_iota(i32, (e, t), 0).astype(f32)
    vals, hots = [], []
    cur = logits
    for k in range(TOP_K):
        m = jnp.max(cur, axis=0, keepdims=True)
        first = jnp.min(jnp.where(cur == m, row, float(e)), axis=0, keepdims=True)
        hot = row == first
        cur = jnp.where(hot, -jnp.inf, cur)
        vals.append(m)
        hots.append(hot)
        idx_ref[k:k + 1, :] = first.astype(i32)
    ex = [jnp.exp(v - vals[0]) for v in vals]
    den = ex[0] + ex[1] + ex[2] + ex[3]
    for k in range(TOP_K):
        gate_ref[k:k + 1, :] = ex[k] / den
    onehot = jnp.concatenate([jnp.where(h, 1.0, 0.0) for h in hots], axis=0)
    r = lax.broadcasted_iota(i32, (t, t), 0)
    c = lax.broadcasted_iota(i32, (t, t), 1)
    upper = jnp.where(r <= c, 1.0, 0.0).astype(bf16)
    prefix = jnp.dot(onehot.astype(bf16), upper, preferred_element_type=f32)
    counts = jnp.sum(onehot, axis=1, keepdims=True)
    offset = base[:, 0:1]
    for k in range(TOP_K):
        sel = jnp.where(hots[k], prefix[k * e:(k + 1) * e, :] - 1.0 + offset, 0.0)
        rank_ref[k:k + 1, :] = jnp.sum(sel, axis=0, keepdims=True).astype(i32)
        offset = offset + counts[k * e:(k + 1) * e, :]
    base[...] = jnp.broadcast_to(offset, base.shape)
    cnt_ref[...] = base[...]


def router(hf, w_rt, b_r):
    n, d = hf.shape
    t = ROUTER_TILE
    tok = pl.BlockSpec((TOP_K, t), lambda i: (0, i))
    return pl.pallas_call(
        _router_kernel, grid=(n // t,),
        in_specs=[pl.BlockSpec((t, d), lambda i: (i, 0)), pl.BlockSpec((N_EXPERTS, d), lambda i: (0, 0)),
                  pl.BlockSpec((N_EXPERTS, 1), lambda i: (0, 0))],
        out_specs=[tok, tok, tok, pl.BlockSpec((N_EXPERTS, LANES), lambda i: (0, 0))],
        out_shape=[jax.ShapeDtypeStruct((TOP_K, n), i32), jax.ShapeDtypeStruct((TOP_K, n), f32),
                   jax.ShapeDtypeStruct((TOP_K, n), i32), jax.ShapeDtypeStruct((N_EXPERTS, LANES), f32)],
        scratch_shapes=[pltpu.VMEM((N_EXPERTS, LANES), f32)],
        compiler_params=_cparams("arbitrary"), name="router",
    )(hf, w_rt, b_r)


def _dest_kernel(idx_ref, rank_ref, start_ref, dest_ref):
    t = idx_ref.shape[1]
    row = lax.broadcasted_iota(i32, (N_EXPERTS, t), 0)
    for k in range(TOP_K):
        hot = row == idx_ref[k:k + 1, :]
        off = jnp.sum(jnp.where(hot, start_ref[...], 0.0), axis=0, keepdims=True)
        dest_ref[k:k + 1, :] = rank_ref[k:k + 1, :] + off.astype(i32)


def dest_rows(idx, rank, pad_start):
    n = idx.shape[1]
    t = ROUTER_TILE
    tok = pl.BlockSpec((TOP_K, t), lambda i: (0, i))
    return pl.pallas_call(
        _dest_kernel, grid=(n // t,),
        in_specs=[tok, tok, pl.BlockSpec((N_EXPERTS, 1), lambda i: (0, 0))],
        out_specs=tok, out_shape=jax.ShapeDtypeStruct((TOP_K, n), i32),
        compiler_params=_cparams("parallel"), name="dest_rows",
    )(idx, rank, pad_start)


def _sc_mesh():
    return plsc.VectorSubcoreMesh(core_axis_name="c", subcore_axis_name="s")


def sc_scatter_rows(x, dest, n_rows):
    n, d = x.shape
    kk = dest.shape[0]

    @functools.partial(pl.kernel, out_type=jax.ShapeDtypeStruct((n_rows, d), x.dtype), mesh=_sc_mesh(),
                       scratch_types=[])
    def k(x_hbm, i_hbm, o_hbm):
        def body(x_vmem, i_vmem):
            for j in range(kk):
                pltpu.sync_copy(x_vmem, o_hbm.at[i_vmem.at[j]])

        pltpu.emit_pipeline(
            body, grid=(n // SC_WINDOW,),
            in_specs=[pl.BlockSpec((SC_WINDOW, d), lambda i: (i, 0)),
                      pl.BlockSpec((kk, SC_WINDOW), lambda i: (0, i))],
            out_specs=[], core_axis_name=("c", "s"), dimension_semantics=(pltpu.PARALLEL,),
        )(x_hbm, i_hbm)

    return k(x, dest)


def sc_gather_rows(table, idx):
    m = idx.shape[1]
    d = table.shape[1]

    @functools.partial(pl.kernel, out_type=jax.ShapeDtypeStruct((m, d), table.dtype), mesh=_sc_mesh(),
                       scratch_types=[])
    def k(t_hbm, i_hbm, o_hbm):
        def body(i_vmem, o_vmem):
            pltpu.sync_copy(t_hbm.at[i_vmem.at[0]], o_vmem)

        pltpu.emit_pipeline(
            body, grid=(m // SC_WINDOW,),
            in_specs=[pl.BlockSpec((1, SC_WINDOW), lambda i: (0, i))],
            out_specs=[pl.BlockSpec((SC_WINDOW, d), lambda i: (i, 0))],
            core_axis_name=("c", "s"), dimension_semantics=(pltpu.PARALLEL,),
        )(i_hbm, o_hbm)

    return k(table, idx)


def _expert_kernel(be_ref, nu_ref, xlo_ref, xhi_ref, w1_ref, b1_ref, w2_ref, b2_ref, ylo_ref, yhi_ref):
    i = pl.program_id(0)

    @pl.when(i < nu_ref[0])
    def _():
        lo = xlo_ref[...]
        hi = xhi_ref[...]
        parts = (_unpack_lo(lo), _unpack_lo(hi), _unpack_hi(lo), _unpack_hi(hi))
        u = b1_ref[0]
        for p, xp in enumerate(parts):
            u = u + jnp.dot(xp.astype(bf16), w1_ref[0, p * QUART:(p + 1) * QUART, :], preferred_element_type=f32)
        glu = jnp.minimum(u[:, :D_EXPERT], SWIGLU_LIMIT)
        lin = jnp.clip(u[:, D_EXPERT:], -SWIGLU_LIMIT, SWIGLU_LIMIT)
        act = (glu * _sigmoid(SWIGLU_ALPHA * glu) * (lin + 1.0)).astype(bf16)
        y = jnp.dot(act, w2_ref[0], preferred_element_type=f32) + b2_ref[0]
        ylo, yhi = _pack_row(y)
        ylo_ref[...] = ylo
        yhi_ref[...] = yhi


def expert_ffn(block_expert, n_used, xs_lo, xs_hi, w1, b1, w2, b2):
    n_rows = xs_lo.shape[0]
    nb = n_rows // MOE_BLK
    rows = pl.BlockSpec((MOE_BLK, QUART), lambda i, be, nu: (jnp.minimum(i, nu[0] - 1), 0))
    wsel = lambda shape: pl.BlockSpec((1,) + shape, lambda i, be, nu: (be[i], 0, 0))
    out = jax.ShapeDtypeStruct((n_rows, QUART), i32)
    return pl.pallas_call(
        _expert_kernel,
        grid_spec=pltpu.PrefetchScalarGridSpec(
            num_scalar_prefetch=2, grid=(nb,),
            in_specs=[rows, rows, wsel((D_MODEL, 2 * D_EXPERT)), wsel((1, 2 * D_EXPERT)),
                      wsel((D_EXPERT, D_MODEL)), wsel((1, D_MODEL))],
            out_specs=[rows, rows]),
        out_shape=[out, out],
        compiler_params=_cparams("arbitrary"), name="expert_ffn",
    )(block_expert, n_used, xs_lo, xs_hi, w1, b1, w2, b2)


def _combine_kernel(ylo_ref, yhi_ref, gt_ref, hf_ref, g_ref, b_ref, of_ref, ob_ref):
    gt = gt_ref[...]
    parts = [None] * 4
    for k in range(TOP_K):
        gk = gt[:, k:k + 1]
        lo = ylo_ref[k]
        hi = yhi_ref[k]
        vals = (_unpack_lo(lo), _unpack_lo(hi), _unpack_hi(lo), _unpack_hi(hi))
        for p in range(4):
            parts[p] = gk * vals[p] if parts[p] is None else parts[p] + gk * vals[p]
    z = DN_ALPHA * hf_ref[...] + jnp.concatenate(parts, axis=1)
    y = _layer_norm(z, g_ref[...], b_ref[...])
    of_ref[...] = y
    ob_ref[...] = y.astype(bf16)


def combine(yk_lo, yk_hi, gates_t, hf, g, b):
    n, d = hf.shape
    t = MIX_TILE
    ysp = pl.BlockSpec((TOP_K, t, QUART), lambda i: (0, i, 0))
    rowb = lambda c: pl.BlockSpec((t, c), lambda i: (i, 0))
    vec = pl.BlockSpec((1, d), lambda i: (0, 0))
    return pl.pallas_call(
        _combine_kernel, grid=(n // t,),
        in_specs=[ysp, ysp, rowb(TOP_K), rowb(d), vec, vec],
        out_specs=[rowb(d), rowb(d)],
        out_shape=[jax.ShapeDtypeStruct((n, d), f32), jax.ShapeDtypeStruct((n, d), bf16)],
        compiler_params=_cparams("parallel"), name="combine",
    )(yk_lo, yk_hi, gates_t, hf, g, b)


def _rope_tables(positions):
    inv = ROPE_THETA ** (-jnp.arange(0, C_ROPE, 2, dtype=f32) / C_ROPE)
    ang = positions.reshape(-1).astype(f32)[:, None] * inv
    cos, sin = jnp.cos(ang), jnp.sin(ang)
    pad = jnp.zeros((ang.shape[0], LANES - C_ROPE), f32)
    return jnp.concatenate([cos, cos, pad], axis=1), jnp.concatenate([-sin, sin, pad], axis=1)


def _attn_a_bias(rel_tables):
    depth = rel_tables.shape[0]
    r = jnp.arange(A_QBLK)[:, None]
    c = jnp.arange(A_BAND)[None, :]
    rel = jnp.clip(c - A_LEFT - r, -A_MAX_REL, A_MAX_REL) + A_MAX_REL
    own = c - CHUNK * (r // CHUNK)
    valid = (own >= 0) & (own < A_LEFT + CHUNK)
    onehot = (rel[:, :, None] == jnp.arange(2 * A_MAX_REL + 1)[None, None, :]).astype(f32)
    bias = jnp.einsum("rck,lhk->lhrc", onehot, rel_tables.astype(f32), precision=lax.Precision.HIGHEST)
    bias = jnp.where(valid[None, None], bias, NEG_INF)
    return bias.reshape(depth, A_HEADS // 2, 2 * A_QBLK, A_BAND)


def _layout_w_uq(w_uq):
    w = w_uq.reshape(C_Q_RANK, C_HEADS, C_QK)
    out = jnp.zeros((C_Q_RANK, C_HEADS, C_SLOT), w_uq.dtype)
    for h in range(C_HEADS):
        off = (h % 2) * C_NOPE
        out = out.at[:, h, off:off + C_NOPE].set(w[:, h, :C_NOPE])
        out = out.at[:, h, LANES:LANES + C_ROPE].set(w[:, h, C_NOPE:])
    return out.reshape(C_Q_RANK, C_HEADS * C_SLOT)


def _layout_w_ukv(w_ukv):
    w = w_ukv.reshape(C_KV_RANK, C_HEADS, C_NOPE + C_V)
    return jnp.concatenate([w[:, :, :C_NOPE].reshape(C_KV_RANK, -1), w[:, :, C_NOPE:].reshape(C_KV_RANK, -1)], axis=1)


def _moe(hf, lo, hi, w_rt, b_r, w1, b1, w2, b2, g2, bb2):
    n = hf.shape[0]
    n_rows = n * TOP_K + N_EXPERTS * MOE_BLK
    nb = n_rows // MOE_BLK
    idx, gates, rank, cnt = router(hf, w_rt, b_r)
    counts = cnt[:, 0].astype(i32)
    padded = (counts + MOE_BLK - 1) // MOE_BLK * MOE_BLK
    pad_end = jnp.cumsum(padded)
    pad_start = (pad_end - padded).astype(f32).reshape(N_EXPERTS, 1)
    block_start = jnp.arange(nb, dtype=i32) * MOE_BLK
    block_expert = jnp.minimum(jnp.sum((pad_end[None, :] <= block_start[:, None]).astype(i32), axis=1),
                               N_EXPERTS - 1)
    n_used = (pad_end[-1:] // MOE_BLK).astype(i32)
    dest = dest_rows(idx, rank, pad_start)
    xs_lo = sc_scatter_rows(lo, dest, n_rows)
    xs_hi = sc_scatter_rows(hi, dest, n_rows)
    ys_lo, ys_hi = expert_ffn(block_expert, n_used, xs_lo, xs_hi, w1, b1, w2, b2)
    flat = dest.reshape(1, TOP_K * n)
    yk_lo = sc_gather_rows(ys_lo, flat)
    yk_hi = sc_gather_rows(ys_hi, flat)
    return combine(yk_lo.reshape(TOP_K, n, QUART), yk_hi.reshape(TOP_K, n, QUART), gates.T, hf, g2, bb2)


def kernel(x, positions, ln_in_g, ln_in_b, w_in, w_gate, b_gate, rel_bias, conv_w, conv_b, conv_ln_g, conv_ln_b, w_pw2, q_norm_g, kv_norm_g, w_uq, w_ukv, w_oa, w_oc, w_out, ln1_g, ln1_b, w_router, b_router, w1, b1, w2, b2, ln2_g, ln2_b):
    batch, seq, d = x.shape
    n = batch * seq
    cosm, sinm = _rope_tables(positions)
    a_bias = _attn_a_bias(rel_bias)
    hf, hb = ln_in(x.reshape(n, d), ln_in_g, ln_in_b)
    ab = 3 * A_WIDTH + 2 * CONV_CH
    row1 = lambda v: v.reshape(1, -1)
    for l in range(DEPTH):
        w_ab = w_in[l, :, :ab].astype(bf16)
        w_cq = w_in[l, :, ab:ab + C_Q_RANK].astype(bf16)
        w_ckv = w_in[l, :, ab + C_Q_RANK:ab + C_Q_RANK + C_KV_RANK].astype(bf16)
        w_kr = jnp.pad(w_in[l, :, ab + C_Q_RANK + C_KV_RANK:], ((0, 0), (0, LANES - C_ROPE))).astype(bf16)
        qkvg = proj(hb, w_ab)
        qc, kc, vc = mla_prep(hb, w_cq, w_ckv, w_kr, row1(q_norm_g[l]), row1(kv_norm_g[l]),
                              _layout_w_uq(w_uq[l]).astype(bf16), _layout_w_ukv(w_ukv[l]).astype(bf16), cosm, sinm)
        ya = attn_a(qkvg, a_bias[l], batch, seq)
        cb = conv_module(qkvg, conv_w[l], conv_b[l], conv_ln_g[l], conv_ln_b[l], batch, seq)
        yc = mla_attn(qc, kc, vc, batch, seq)
        h1f, lo, hi = mix_layer(ya, cb, yc, hb, hf, w_oa[l].astype(bf16), w_pw2[l].astype(bf16),
                                w_oc[l].astype(bf16), w_gate[l].astype(bf16), row1(b_gate[l]),
                                w_out[l].astype(bf16), row1(ln1_g[l]), row1(ln1_b[l]))
        hf, hb = _moe(h1f, lo, hi, w_router[l].T, b_router[l].reshape(N_EXPERTS, 1),
                      w1[l].astype(bf16), b1[l].reshape(N_EXPERTS, 1, -1), w2[l].astype(bf16),
                      b2[l].reshape(N_EXPERTS, 1, -1), row1(ln2_g[l]), row1(ln2_b[l]))
    return hf.reshape(batch, seq, d)
```

```python
import functools

import jax
import jax.numpy as jnp
from jax import lax
from jax.experimental import pallas as pl
from jax.experimental.pallas import tpu as pltpu
from jax.experimental.pallas import tpu_sc as plsc

f32 = jnp.float32
bf16 = jnp.bfloat16
i32 = jnp.int32

D_MODEL = 1024
DEPTH = 4
CHUNK = 64
A_HEADS = 8
A_HEAD_DIM = 64
A_WIDTH = A_HEADS * A_HEAD_DIM
A_LEFT = 8 * CHUNK
A_MAX_REL = 128
CONV_CH = 512
CONV_WIDTH = 31
C_HEADS = 8
C_NOPE = 64
C_ROPE = 32
C_V = 64
C_QK = C_NOPE + C_ROPE
C_Q_RANK = 384
C_KV_RANK = 256
ROPE_THETA = 10000.0
N_EXPERTS = 32
TOP_K = 4
D_EXPERT = 1024
SWIGLU_ALPHA = 1.702
SWIGLU_LIMIT = 7.0
DN_ALPHA = (2 * DEPTH) ** 0.25
LN_EPS = 1e-5
RMS_EPS = 1e-6
NEG_INF = -1e30
LOG2E = 1.4426950408889634

LANES = 128
A_QBLK = 2 * CHUNK
A_STEP = 2 * A_QBLK
A_BAND = A_LEFT + A_QBLK
C_BLK = 256
C_STEP = 2 * C_BLK
C_SLOT = 256
CONV_BLK = 256
CONV_HALO = 32
MOE_BLK = 512
ROW_TILE = 512
MIX_TILE = 512
SUB_ROWS = 256
ROUTER_TILE = 512
HALF = D_MODEL // 2
QUART = D_MODEL // 4
SC_WINDOW = 128
VMEM_LIMIT = 56 * 1024 * 1024


def _cparams(*sem):
    return pltpu.CompilerParams(dimension_semantics=tuple(sem), vmem_limit_bytes=VMEM_LIMIT)


def _layer_norm(x, g, b):
    mu = jnp.mean(x, axis=-1, keepdims=True)
    xc = x - mu
    var = jnp.mean(xc * xc, axis=-1, keepdims=True)
    return xc * lax.rsqrt(var + LN_EPS) * g + b


def _sigmoid(x):
    return 1.0 / (1.0 + jnp.exp(-x))


def _pack2(a, b):
    ab = lax.bitcast_convert_type(a.astype(bf16).astype(f32), i32)
    bb = lax.bitcast_convert_type(b.astype(bf16).astype(f32), i32)
    return lax.shift_right_logical(ab, 16) | (bb & jnp.int32(-65536))


def _unpack_lo(w):
    return lax.bitcast_convert_type(lax.shift_left(w, 16), f32)


def _unpack_hi(w):
    return lax.bitcast_convert_type(w & jnp.int32(-65536), f32)


def _pack_row(y):
    return (_pack2(y[:, 0:QUART], y[:, 2 * QUART:3 * QUART]),
            _pack2(y[:, QUART:2 * QUART], y[:, 3 * QUART:4 * QUART]))


def _ln_in_kernel(x_ref, g_ref, b_ref, hf_ref, hb_ref):
    y = _layer_norm(x_ref[...], g_ref[...], b_ref[...])
    hf_ref[...] = y
    hb_ref[...] = y.astype(bf16)


def ln_in(x2d, g, b):
    n, d = x2d.shape
    row = pl.BlockSpec((ROW_TILE, d), lambda i: (i, 0))
    vec = pl.BlockSpec((1, d), lambda i: (0, 0))
    return pl.pallas_call(
        _ln_in_kernel, grid=(n // ROW_TILE,),
        in_specs=[row, vec, vec], out_specs=[row, row],
        out_shape=[jax.ShapeDtypeStruct((n, d), f32), jax.ShapeDtypeStruct((n, d), bf16)],
        compiler_params=_cparams("parallel"), name="ln_in",
    )(x2d, g.reshape(1, d), b.reshape(1, d))


def _proj_kernel(x_ref, w_ref, o_ref):
    acc = jnp.dot(x_ref[...], w_ref[...], preferred_element_type=f32)
    o_ref[:, :A_WIDTH] = (acc[:, :A_WIDTH] * (A_HEAD_DIM ** -0.5 * LOG2E)).astype(o_ref.dtype)
    o_ref[:, A_WIDTH:] = acc[:, A_WIDTH:].astype(o_ref.dtype)


def proj(hb, w):
    n, d = hb.shape
    c = w.shape[1]
    return pl.pallas_call(
        _proj_kernel, grid=(n // ROW_TILE,),
        in_specs=[pl.BlockSpec((ROW_TILE, d), lambda i: (i, 0)), pl.BlockSpec((d, c), lambda i: (0, 0))],
        out_specs=pl.BlockSpec((ROW_TILE, c), lambda i: (i, 0)),
        out_shape=jax.ShapeDtypeStruct((n, c), bf16),
        compiler_params=_cparams("parallel"), name="proj",
    )(hb, w)


def _mla_prep_kernel(hb_ref, wcq_ref, wckv_ref, wkr_ref, gq_ref, gkv_ref, wuq_ref, wukv_ref, cos_ref, sin_ref,
                     qc_ref, kc_ref, vc_ref):
    hb = hb_ref[...]
    cosm = cos_ref[...]
    sinm = sin_ref[...]
    lane = lax.broadcasted_iota(i32, cosm.shape, 1)
    half = C_ROPE // 2
    scale = C_QK ** -0.5 * LOG2E

    def rope(x):
        swapped = jnp.where(lane < half, pltpu.roll(x, LANES - half, 1), pltpu.roll(x, half, 1))
        return x * cosm + swapped * sinm

    def rms(x, g):
        return (x * lax.rsqrt(jnp.mean(x * x, axis=-1, keepdims=True) + RMS_EPS) * g).astype(bf16)

    cqn = rms(jnp.dot(hb, wcq_ref[...], preferred_element_type=f32), gq_ref[...])
    q = jnp.dot(cqn, wuq_ref[...], preferred_element_type=f32)
    for h in range(C_HEADS):
        lo = h * C_SLOT
        qc_ref[:, lo:lo + LANES] = (q[:, lo:lo + LANES] * scale).astype(bf16)
        qc_ref[:, lo + LANES:lo + C_SLOT] = (rope(q[:, lo + LANES:lo + C_SLOT]) * scale).astype(bf16)
    ckvn = rms(jnp.dot(hb, wckv_ref[...], preferred_element_type=f32), gkv_ref[...])
    kv = jnp.dot(ckvn, wukv_ref[...], preferred_element_type=f32)
    kr = rope(jnp.dot(hb, wkr_ref[...], preferred_element_type=f32)).astype(bf16)
    for j in range(C_HEADS // 2):
        lo = j * C_SLOT
        kc_ref[:, lo:lo + LANES] = kv[:, j * LANES:(j + 1) * LANES].astype(bf16)
        kc_ref[:, lo + LANES:lo + C_SLOT] = kr
    vc_ref[...] = kv[:, C_HEADS * C_NOPE:].astype(bf16)


def mla_prep(hb, wcq, wckv, wkr, gq, gkv, wuq, wukv, cosm, sinm):
    n, d = hb.shape
    t = ROW_TILE
    full = lambda a: pl.BlockSpec(a.shape, lambda i: (0,) * a.ndim)
    rowb = lambda c: pl.BlockSpec((t, c), lambda i: (i, 0))
    qw, kw, vw = C_HEADS * C_SLOT, (C_HEADS // 2) * C_SLOT, C_HEADS * C_V
    return pl.pallas_call(
        _mla_prep_kernel, grid=(n // t,),
        in_specs=[rowb(d), full(wcq), full(wckv), full(wkr), full(gq), full(gkv), full(wuq), full(wukv),
                  rowb(LANES), rowb(LANES)],
        out_specs=[rowb(qw), rowb(kw), rowb(vw)],
        out_shape=[jax.ShapeDtypeStruct((n, qw), bf16), jax.ShapeDtypeStruct((n, kw), bf16),
                   jax.ShapeDtypeStruct((n, vw), bf16)],
        compiler_params=_cparams("parallel"), name="mla_prep",
    )(hb, wcq, wckv, wkr, gq, gkv, wuq, wukv, cosm, sinm)


def _attn_a_kernel(q_ref, k_ref, v_ref, bias_ref, o_ref, kpad, vpad):
    qi = pl.program_id(1)
    seq = k_ref.shape[0]

    @pl.when(qi == 0)
    def _():
        kpad[0:A_LEFT, :] = jnp.zeros((A_LEFT, A_WIDTH), bf16)
        vpad[0:A_LEFT, :] = jnp.zeros((A_LEFT, A_WIDTH), bf16)
        kpad[A_LEFT:A_LEFT + seq, :] = k_ref[...]
        vpad[A_LEFT:A_LEFT + seq, :] = v_ref[...]

    col = lax.broadcasted_iota(i32, (2 * A_QBLK, A_BAND), 1)
    lane = lax.broadcasted_iota(i32, (A_QBLK, LANES), 1)
    ones = jnp.ones((A_BAND, LANES), bf16)
    for sub in range(A_STEP // A_QBLK):
        start = pl.multiple_of(qi * A_STEP + sub * A_QBLK, A_QBLK)
        before_start = jnp.where(col + start >= A_LEFT, 0.0, NEG_INF)
        rows = slice(sub * A_QBLK, (sub + 1) * A_QBLK)
        for j in range(A_HEADS // 2):
            cs = slice(j * LANES, (j + 1) * LANES)
            qp = q_ref[rows, cs].astype(f32)
            qs = jnp.concatenate([jnp.where(lane < A_HEAD_DIM, qp, 0.0), jnp.where(lane >= A_HEAD_DIM, qp, 0.0)],
                                 axis=0).astype(bf16)
            kb = kpad[pl.ds(start, A_BAND), cs]
            vb = jnp.concatenate([vpad[pl.ds(start, A_BAND), cs], ones], axis=1)
            s = lax.dot_general(qs, kb, (((1,), (1,)), ((), ())), preferred_element_type=f32)
            s = s + bias_ref[j] + before_start
            p = jnp.exp2(s - jnp.max(s, axis=-1, keepdims=True)).astype(bf16)
            o = jnp.dot(p, vb, preferred_element_type=f32)
            o = o[:, :LANES] / o[:, LANES:]
            o_ref[rows, cs] = jnp.where(lane < A_HEAD_DIM, o[:A_QBLK], o[A_QBLK:]).astype(bf16)


def attn_a(qkvg, bias, batch, seq):
    n = batch * seq
    nq = seq // A_STEP
    return pl.pallas_call(
        _attn_a_kernel, grid=(batch, nq),
        in_specs=[pl.BlockSpec((A_STEP, A_WIDTH), lambda b, i: (b * nq + i, 0)),
                  pl.BlockSpec((seq, A_WIDTH), lambda b, i: (b, 1)),
                  pl.BlockSpec((seq, A_WIDTH), lambda b, i: (b, 2)),
                  pl.BlockSpec(bias.shape, lambda b, i: (0, 0, 0))],
        out_specs=pl.BlockSpec((A_STEP, A_WIDTH), lambda b, i: (b * nq + i, 0)),
        out_shape=jax.ShapeDtypeStruct((n, A_WIDTH), bf16),
        scratch_shapes=[pltpu.VMEM((A_LEFT + seq, A_WIDTH), bf16), pltpu.VMEM((A_LEFT + seq, A_WIDTH), bf16)],
        compiler_params=_cparams("parallel", "arbitrary"), name="attn_a",
    )(qkvg, qkvg, qkvg, bias)


def _conv_kernel(a_ref, g_ref, ap_ref, gp_ref, w_ref, cb_ref, lg_ref, lb_ref, o_ref, hs, sh):
    i = pl.program_id(1)
    t = CONV_BLK
    tail = slice(t - CONV_HALO, t)
    prev = ap_ref[tail, :].astype(f32) * _sigmoid(gp_ref[tail, :].astype(f32))
    hs[0:CONV_HALO, :] = jnp.where(i > 0, prev, 0.0)
    hs[CONV_HALO:CONV_HALO + t, :] = a_ref[...].astype(f32) * _sigmoid(g_ref[...].astype(f32))
    first = CONV_HALO - CONV_WIDTH + 1
    a_of = [[a for a in range(CONV_HALO // 8 + 1) if first <= 8 * a + b <= CONV_HALO] for b in range(8)]
    for b in range(1, 8):
        length = 8 * max(a_of[b]) + t
        sh[b - 1, 0:length, :] = hs[b:b + length, :]
    rows = 64
    for r0 in range(0, t, rows):
        acc = jnp.zeros((rows, CONV_CH), f32) + cb_ref[...]
        for b in range(8):
            for a in a_of[b]:
                tap = 8 * a + b - first
                lo = r0 + 8 * a
                src = hs[lo:lo + rows, :] if b == 0 else sh[b - 1, lo:lo + rows, :]
                acc = acc + src * w_ref[tap:tap + 1, :]
        y = _layer_norm(acc, lg_ref[...], lb_ref[...])
        o_ref[r0:r0 + rows, :] = (y * _sigmoid(y)).astype(bf16)


def conv_module(qkvg, w_dw, b_dw, ln_g, ln_b, batch, seq):
    n = batch * seq
    nb = seq // CONV_BLK
    a_col = 3 * A_WIDTH // CONV_CH
    cur = lambda c: pl.BlockSpec((CONV_BLK, CONV_CH), lambda b, i: (b * nb + i, c))
    prv = lambda c: pl.BlockSpec((CONV_BLK, CONV_CH), lambda b, i: (b * nb + jnp.maximum(i - 1, 0), c))
    vec = pl.BlockSpec((1, CONV_CH), lambda b, i: (0, 0))
    return pl.pallas_call(
        _conv_kernel, grid=(batch, nb),
        in_specs=[cur(a_col), cur(a_col + 1), prv(a_col), prv(a_col + 1),
                  pl.BlockSpec((CONV_WIDTH, CONV_CH), lambda b, i: (0, 0)), vec, vec, vec],
        out_specs=pl.BlockSpec((CONV_BLK, CONV_CH), lambda b, i: (b * nb + i, 0)),
        out_shape=jax.ShapeDtypeStruct((n, CONV_CH), bf16),
        scratch_shapes=[pltpu.VMEM((CONV_HALO + CONV_BLK, CONV_CH), f32),
                        pltpu.VMEM((7, CONV_HALO + CONV_BLK, CONV_CH), f32)],
        compiler_params=_cparams("parallel", "parallel"), name="conv_module",
    )(qkvg, qkvg, qkvg, qkvg, w_dw, b_dw.reshape(1, -1), ln_g.reshape(1, -1), ln_b.reshape(1, -1))


def _mla_attn_kernel(q_ref, k_ref, v_ref, o_ref, acc_ref):
    qi = pl.program_id(1)
    t = C_BLK
    pairs = C_HEADS // 2
    row = lax.broadcasted_iota(i32, (2 * t, t), 0)
    col = lax.broadcasted_iota(i32, (2 * t, t), 1)
    diag_ok = (col // CHUNK) <= ((row % t) // CHUNK)
    lane = lax.broadcasted_iota(i32, (t, LANES), 1)
    ones = jnp.ones((t, LANES), bf16)
    subs = C_STEP // t
    acc_ref[...] = jnp.zeros(acc_ref.shape, f32)

    def step(kb, ms, plan):
        k0 = pl.multiple_of(kb * t, t)
        ms = list(ms)
        for j in range(pairs):
            kblk = k_ref[pl.ds(k0, t), j * C_SLOT:(j + 1) * C_SLOT]
            vext = jnp.concatenate([v_ref[pl.ds(k0, t), j * LANES:(j + 1) * LANES], ones], axis=1)
            for sub, masked in plan:
                c = sub * pairs + j
                rows = slice(sub * t, (sub + 1) * t)
                qs = jnp.concatenate([q_ref[rows, (2 * j) * C_SLOT:(2 * j + 1) * C_SLOT],
                                      q_ref[rows, (2 * j + 1) * C_SLOT:(2 * j + 2) * C_SLOT]], axis=0)
                s = lax.dot_general(qs, kblk, (((1,), (1,)), ((), ())), preferred_element_type=f32)
                if masked:
                    s = jnp.where(diag_ok, s, NEG_INF)
                m_new = jnp.maximum(ms[c], jnp.max(s, axis=-1, keepdims=True))
                a = jnp.exp2(ms[c] - m_new)
                p = jnp.exp2(s - m_new).astype(bf16)
                acc_ref[c] = a * acc_ref[c] + jnp.dot(p, vext, preferred_element_type=f32)
                ms[c] = m_new
        return tuple(ms)

    init = tuple(jnp.full((2 * t, 1), NEG_INF, f32) for _ in range(subs * pairs))
    ms = lax.fori_loop(0, subs * qi, lambda kb, c: step(kb, c, [(0, False), (1, False)]), init)
    ms = step(subs * qi, ms, [(0, True), (1, False)])
    step(subs * qi + 1, ms, [(1, True)])
    for sub in range(subs):
        for j in range(pairs):
            acc = acc_ref[sub * pairs + j]
            o = acc[:, :LANES] / acc[:, LANES:]
            o_ref[sub * t:(sub + 1) * t, j * LANES:(j + 1) * LANES] = (
                jnp.where(lane < C_V, o[:t], o[t:]).astype(bf16))


def mla_attn(qc, kc, vc, batch, seq):
    n = batch * seq
    nq = seq // C_STEP
    return pl.pallas_call(
        _mla_attn_kernel, grid=(batch, nq),
        in_specs=[pl.BlockSpec((C_STEP, qc.shape[1]), lambda b, i: (b * nq + i, 0)),
                  pl.BlockSpec((seq, kc.shape[1]), lambda b, i: (b, 0)),
                  pl.BlockSpec((seq, vc.shape[1]), lambda b, i: (b, 0))],
        out_specs=pl.BlockSpec((C_STEP, C_HEADS * C_V), lambda b, i: (b * nq + i, 0)),
        out_shape=jax.ShapeDtypeStruct((n, C_HEADS * C_V), bf16),
        scratch_shapes=[pltpu.VMEM((C_STEP // C_BLK * (C_HEADS // 2), 2 * C_BLK, 2 * LANES), f32)],
        compiler_params=_cparams("parallel", "parallel"), name="mla_attn",
    )(qc, kc, vc)


def _mix_kernel(ya_ref, cb_ref, yc_ref, hb_ref, hf_ref, woa_ref, wpw_ref, woc_ref, wg_ref, bg_ref, wout_ref,
                g_ref, b_ref, of_ref, lo_ref, hi_ref):
    d = D_MODEL
    for r0 in range(0, MIX_TILE, SUB_ROWS):
        rows = slice(r0, r0 + SUB_ROWS)
        gates = _sigmoid(jnp.dot(hb_ref[rows, :], wg_ref[...], preferred_element_type=f32) + bg_ref[...])
        mix = (gates[:, 0:d] * jnp.dot(ya_ref[rows, :], woa_ref[...], preferred_element_type=f32)
               + gates[:, d:2 * d] * jnp.dot(cb_ref[rows, :], wpw_ref[...], preferred_element_type=f32)
               + gates[:, 2 * d:3 * d] * jnp.dot(yc_ref[rows, :], woc_ref[...], preferred_element_type=f32))
        z = DN_ALPHA * hf_ref[rows, :] + jnp.dot(mix.astype(bf16), wout_ref[...], preferred_element_type=f32)
        y = _layer_norm(z, g_ref[...], b_ref[...])
        of_ref[rows, :] = y
        lo, hi = _pack_row(y)
        lo_ref[rows, :] = lo
        hi_ref[rows, :] = hi


def mix_layer(ya, cb, yc, hb, hf, woa, wpw, woc, wg, bg, wout, g, b):
    n, d = hf.shape
    t = MIX_TILE
    full = lambda a: pl.BlockSpec(a.shape, lambda i: (0,) * a.ndim)
    rowb = lambda c: pl.BlockSpec((t, c), lambda i: (i, 0))
    return pl.pallas_call(
        _mix_kernel, grid=(n // t,),
        in_specs=[rowb(ya.shape[1]), rowb(cb.shape[1]), rowb(yc.shape[1]), rowb(d), rowb(d),
                  full(woa), full(wpw), full(woc), full(wg), full(bg), full(wout), full(g), full(b)],
        out_specs=[rowb(d), rowb(QUART), rowb(QUART)],
        out_shape=[jax.ShapeDtypeStruct((n, d), f32), jax.ShapeDtypeStruct((n, QUART), i32),
                   jax.ShapeDtypeStruct((n, QUART), i32)],
        compiler_params=_cparams("parallel"), name="mix_layer",
    )(ya, cb, yc, hb, hf, woa, wpw, woc, wg, bg, wout, g, b)


def _router_kernel(h_ref, wr_ref, br_ref, idx_ref, gate_ref, rank_ref, cnt_ref, base):
    i = pl.program_id(0)
    t = ROUTER_TILE
    e = N_EXPERTS

    @pl.when(i == 0)
    def _():
        base[...] = jnp.zeros_like(base)

    logits = lax.dot_general(wr_ref[...], h_ref[...], (((1,), (1,)), ((), ())),
                             precision=lax.Precision.HIGHEST, preferred_element_type=f32) + br_ref[...]
    row = lax.broadcasted_iota(i32, (e, t), 0).astype(f32)
    vals, hots = [], []
    cur = logits
    for k in range(TOP_K):
        m = jnp.max(cur, axis=0, keepdims=True)
        first = jnp.min(jnp.where(cur == m, row, float(e)), axis=0, keepdims=True)
        hot = row == first
        cur = jnp.where(hot, -jnp.inf, cur)
        vals.append(m)
        hots.append(hot)
        idx_ref[k:k + 1, :] = first.astype(i32)
    ex = [jnp.exp(v - vals[0]) for v in vals]
    den = ex[0] + ex[1] + ex[2] + ex[3]
    for k in range(TOP_K):
        gate_ref[k:k + 1, :] = ex[k] / den
    onehot = jnp.concatenate([jnp.where(h, 1.0, 0.0) for h in hots], axis=0)
    r = lax.broadcasted_iota(i32, (t, t), 0)
    c = lax.broadcasted_iota(i32, (t, t), 1)
    upper = jnp.where(r <= c, 1.0, 0.0).astype(bf16)
    prefix = jnp.dot(onehot.astype(bf16), upper, preferred_element_type=f32)
    counts = jnp.sum(onehot, axis=1, keepdims=True)
    offset = base[:, 0:1]
    for k in range(TOP_K):
        sel = jnp.where(hots[k], prefix[k * e:(k + 1) * e, :] - 1.0 + offset, 0.0)
        rank_ref[k:k + 1, :] = jnp.sum(sel, axis=0, keepdims=True).astype(i32)
        offset = offset + counts[k * e:(k + 1) * e, :]
    base[...] = jnp.broadcast_to(offset, base.shape)
    cnt_ref[...] = base[...]


def router(hf, w_rt, b_r):
    n, d = hf.shape
    t = ROUTER_TILE
    tok = pl.BlockSpec((TOP_K, t), lambda i: (0, i))
    return pl.pallas_call(
        _router_kernel, grid=(n // t,),
        in_specs=[pl.BlockSpec((t, d), lambda i: (i, 0)), pl.BlockSpec((N_EXPERTS, d), lambda i: (0, 0)),
                  pl.BlockSpec((N_EXPERTS, 1), lambda i: (0, 0))],
        out_specs=[tok, tok, tok, pl.BlockSpec((N_EXPERTS, LANES), lambda i: (0, 0))],
        out_shape=[jax.ShapeDtypeStruct((TOP_K, n), i32), jax.ShapeDtypeStruct((TOP_K, n), f32),
                   jax.ShapeDtypeStruct((TOP_K, n), i32), jax.ShapeDtypeStruct((N_EXPERTS, LANES), f32)],
        scratch_shapes=[pltpu.VMEM((N_EXPERTS, LANES), f32)],
        compiler_params=_cparams("arbitrary"), name="router",
    )(hf, w_rt, b_r)


def _dest_kernel(idx_ref, rank_ref, start_ref, dest_ref):
    t = idx_ref.shape[1]
    row = lax.broadcasted_iota(i32, (N_EXPERTS, t), 0)
    for k in range(TOP_K):
        hot = row == idx_ref[k:k + 1, :]
        off = jnp.sum(jnp.where(hot, start_ref[...], 0.0), axis=0, keepdims=True)
        dest_ref[k:k + 1, :] = rank_ref[k:k + 1, :] + off.astype(i32)


def dest_rows(idx, rank, pad_start):
    n = idx.shape[1]
    t = ROUTER_TILE
    tok = pl.BlockSpec((TOP_K, t), lambda i: (0, i))
    return pl.pallas_call(
        _dest_kernel, grid=(n // t,),
        in_specs=[tok, tok, pl.BlockSpec((N_EXPERTS, 1), lambda i: (0, 0))],
        out_specs=tok, out_shape=jax.ShapeDtypeStruct((TOP_K, n), i32),
        compiler_params=_cparams("parallel"), name="dest_rows",
    )(idx, rank, pad_start)


def _sc_mesh():
    return plsc.VectorSubcoreMesh(core_axis_name="c", subcore_axis_name="s")


def sc_scatter_rows(x, dest, n_rows):
    n, d = x.shape
    kk = dest.shape[0]

    @functools.partial(pl.kernel, out_type=jax.ShapeDtypeStruct((n_rows, d), x.dtype), mesh=_sc_mesh(),
                       scratch_types=[])
    def k(x_hbm, i_hbm, o_hbm):
        def body(x_vmem, i_vmem):
            for j in range(kk):
                pltpu.sync_copy(x_vmem, o_hbm.at[i_vmem.at[j]])

        pltpu.emit_pipeline(
            body, grid=(n // SC_WINDOW,),
            in_specs=[pl.BlockSpec((SC_WINDOW, d), lambda i: (i, 0)),
                      pl.BlockSpec((kk, SC_WINDOW), lambda i: (0, i))],
            out_specs=[], core_axis_name=("c", "s"), dimension_semantics=(pltpu.PARALLEL,),
        )(x_hbm, i_hbm)

    return k(x, dest)


def sc_gather_rows(table, idx):
    m = idx.shape[1]
    d = table.shape[1]

    @functools.partial(pl.kernel, out_type=jax.ShapeDtypeStruct((m, d), table.dtype), mesh=_sc_mesh(),
                       scratch_types=[])
    def k(t_hbm, i_hbm, o_hbm):
        def body(i_vmem, o_vmem):
            pltpu.sync_copy(t_hbm.at[i_vmem.at[0]], o_vmem)

        pltpu.emit_pipeline(
            body, grid=(m // SC_WINDOW,),
            in_specs=[pl.BlockSpec((1, SC_WINDOW), lambda i: (0, i))],
            out_specs=[pl.BlockSpec((SC_WINDOW, d), lambda i: (i, 0))],
            core_axis_name=("c", "s"), dimension_semantics=(pltpu.PARALLEL,),
        )(i_hbm, o_hbm)

    return k(table, idx)


def _expert_kernel(be_ref, nu_ref, xlo_ref, xhi_ref, w1_ref, b1_ref, w2_ref, b2_ref, ylo_ref, yhi_ref):
    i = pl.program_id(0)

    @pl.when(i < nu_ref[0])
    def _():
        for r0 in range(0, MOE_BLK, SUB_ROWS):
            rows = slice(r0, r0 + SUB_ROWS)
            lo = xlo_ref[rows, :]
            hi = xhi_ref[rows, :]
            x = jnp.concatenate([_unpack_lo(lo).astype(bf16), _unpack_lo(hi).astype(bf16),
                                 _unpack_hi(lo).astype(bf16), _unpack_hi(hi).astype(bf16)], axis=1)
            u = jnp.dot(x, w1_ref[0], preferred_element_type=f32) + b1_ref[0]
            glu = jnp.minimum(u[:, :D_EXPERT], SWIGLU_LIMIT)
            lin = jnp.clip(u[:, D_EXPERT:], -SWIGLU_LIMIT, SWIGLU_LIMIT)
            act = (glu * _sigmoid(SWIGLU_ALPHA * glu) * (lin + 1.0)).astype(bf16)
            y = jnp.dot(act, w2_ref[0], preferred_element_type=f32) + b2_ref[0]
            ylo, yhi = _pack_row(y)
            ylo_ref[rows, :] = ylo
            yhi_ref[rows, :] = yhi


def expert_ffn(block_expert, n_used, xs_lo, xs_hi, w1, b1, w2, b2):
    n_rows = xs_lo.shape[0]
    nb = n_rows // MOE_BLK
    rows = pl.BlockSpec((MOE_BLK, QUART), lambda i, be, nu: (jnp.minimum(i, nu[0] - 1), 0))
    wsel = lambda shape: pl.BlockSpec((1,) + shape, lambda i, be, nu: (be[i], 0, 0))
    out = jax.ShapeDtypeStruct((n_rows, QUART), i32)
    return pl.pallas_call(
        _expert_kernel,
        grid_spec=pltpu.PrefetchScalarGridSpec(
            num_scalar_prefetch=2, grid=(nb,),
            in_specs=[rows, rows, wsel((D_MODEL, 2 * D_EXPERT)), wsel((1, 2 * D_EXPERT)),
                      wsel((D_EXPERT, D_MODEL)), wsel((1, D_MODEL))],
            out_specs=[rows, rows]),
        out_shape=[out, out],
        compiler_params=_cparams("arbitrary"), name="expert_ffn",
    )(block_expert, n_used, xs_lo, xs_hi, w1, b1, w2, b2)


def _combine_kernel(ylo_ref, yhi_ref, gt_ref, hf_ref, g_ref, b_ref, of_ref, ob_ref):
    gt = gt_ref[...]
    parts = [None] * 4
    for k in range(TOP_K):
        gk = gt[:, k:k + 1]
        lo = ylo_ref[k]
        hi = yhi_ref[k]
        vals = (_unpack_lo(lo), _unpack_lo(hi), _unpack_hi(lo), _unpack_hi(hi))
        for p in range(4):
            parts[p] = gk * vals[p] if parts[p] is None else parts[p] + gk * vals[p]
    z = DN_ALPHA * hf_ref[...] + jnp.concatenate(parts, axis=1)
    y = _layer_norm(z, g_ref[...], b_ref[...])
    of_ref[...] = y
    ob_ref[...] = y.astype(bf16)


def combine(yk_lo, yk_hi, gates_t, hf, g, b):
    n, d = hf.shape
    t = MIX_TILE
    ysp = pl.BlockSpec((TOP_K, t, QUART), lambda i: (0, i, 0))
    rowb = lambda c: pl.BlockSpec((t, c), lambda i: (i, 0))
    vec = pl.BlockSpec((1, d), lambda i: (0, 0))
    return pl.pallas_call(
        _combine_kernel, grid=(n // t,),
        in_specs=[ysp, ysp, rowb(TOP_K), rowb(d), vec, vec],
        out_specs=[rowb(d), rowb(d)],
        out_shape=[jax.ShapeDtypeStruct((n, d), f32), jax.ShapeDtypeStruct((n, d), bf16)],
        compiler_params=_cparams("parallel"), name="combine",
    )(yk_lo, yk_hi, gates_t, hf, g, b)


def _rope_tables(positions):
    inv = ROPE_THETA ** (-jnp.arange(0, C_ROPE, 2, dtype=f32) / C_ROPE)
    ang = positions.reshape(-1).astype(f32)[:, None] * inv
    cos, sin = jnp.cos(ang), jnp.sin(ang)
    pad = jnp.zeros((ang.shape[0], LANES - C_ROPE), f32)
    return jnp.concatenate([cos, cos, pad], axis=1), jnp.concatenate([-sin, sin, pad], axis=1)


def _attn_a_bias(rel_tables):
    depth = rel_tables.shape[0]
    r = jnp.arange(A_QBLK)[:, None]
    c = jnp.arange(A_BAND)[None, :]
    rel = jnp.clip(c - A_LEFT - r, -A_MAX_REL, A_MAX_REL) + A_MAX_REL
    own = c - CHUNK * (r // CHUNK)
    valid = (own >= 0) & (own < A_LEFT + CHUNK)
    onehot = (rel[:, :, None] == jnp.arange(2 * A_MAX_REL + 1)[None, None, :]).astype(f32)
    bias = jnp.einsum("rck,lhk->lhrc", onehot, rel_tables.astype(f32), precision=lax.Precision.HIGHEST)
    bias = jnp.where(valid[None, None], bias * LOG2E, NEG_INF)
    return bias.reshape(depth, A_HEADS // 2, 2 * A_QBLK, A_BAND)


def _layout_w_uq(w_uq):
    w = w_uq.reshape(C_Q_RANK, C_HEADS, C_QK)
    out = jnp.zeros((C_Q_RANK, C_HEADS, C_SLOT), w_uq.dtype)
    for h in range(C_HEADS):
        off = (h % 2) * C_NOPE
        out = out.at[:, h, off:off + C_NOPE].set(w[:, h, :C_NOPE])
        out = out.at[:, h, LANES:LANES + C_ROPE].set(w[:, h, C_NOPE:])
    return out.reshape(C_Q_RANK, C_HEADS * C_SLOT)


def _layout_w_ukv(w_ukv):
    w = w_ukv.reshape(C_KV_RANK, C_HEADS, C_NOPE + C_V)
    return jnp.concatenate([w[:, :, :C_NOPE].reshape(C_KV_RANK, -1), w[:, :, C_NOPE:].reshape(C_KV_RANK, -1)], axis=1)


def _moe(hf, lo, hi, w_rt, b_r, w1, b1, w2, b2, g2, bb2):
    n = hf.shape[0]
    n_rows = n * TOP_K + N_EXPERTS * MOE_BLK
    nb = n_rows // MOE_BLK
    idx, gates, rank, cnt = router(hf, w_rt, b_r)
    counts = cnt[:, 0].astype(i32)
    padded = (counts + MOE_BLK - 1) // MOE_BLK * MOE_BLK
    pad_end = jnp.cumsum(padded)
    pad_start = (pad_end - padded).astype(f32).reshape(N_EXPERTS, 1)
    block_start = jnp.arange(nb, dtype=i32) * MOE_BLK
    block_expert = jnp.minimum(jnp.sum((pad_end[None, :] <= block_start[:, None]).astype(i32), axis=1),
                               N_EXPERTS - 1)
    n_used = (pad_end[-1:] // MOE_BLK).astype(i32)
    dest = dest_rows(idx, rank, pad_start)
    xs_lo = sc_scatter_rows(lo, dest, n_rows)
    xs_hi = sc_scatter_rows(hi, dest, n_rows)
    ys_lo, ys_hi = expert_ffn(block_expert, n_used, xs_lo, xs_hi, w1, b1, w2, b2)
    flat = dest.reshape(1, TOP_K * n)
    yk_lo = sc_gather_rows(ys_lo, flat)
    yk_hi = sc_gather_rows(ys_hi, flat)
    return combine(yk_lo.reshape(TOP_K, n, QUART), yk_hi.reshape(TOP_K, n, QUART), gates.T, hf, g2, bb2)


def kernel(x, positions, ln_in_g, ln_in_b, w_in, w_gate, b_gate, rel_bias, conv_w, conv_b, conv_ln_g, conv_ln_b, w_pw2, q_norm_g, kv_norm_g, w_uq, w_ukv, w_oa, w_oc, w_out, ln1_g, ln1_b, w_router, b_router, w1, b1, w2, b2, ln2_g, ln2_b):
    batch, seq, d = x.shape
    n = batch * seq
    cosm, sinm = _rope_tables(positions)
    a_bias = _attn_a_bias(rel_bias)
    hf, hb = ln_in(x.reshape(n, d), ln_in_g, ln_in_b)
    ab = 3 * A_WIDTH + 2 * CONV_CH
    row1 = lambda v: v.reshape(1, -1)
    for l in range(DEPTH):
        w_ab = w_in[l, :, :ab].astype(bf16)
        w_cq = w_in[l, :, ab:ab + C_Q_RANK].astype(bf16)
        w_ckv = w_in[l, :, ab + C_Q_RANK:ab + C_Q_RANK + C_KV_RANK].astype(bf16)
        w_kr = jnp.pad(w_in[l, :, ab + C_Q_RANK + C_KV_RANK:], ((0, 0), (0, LANES - C_ROPE))).astype(bf16)
        qkvg = proj(hb, w_ab)
        qc, kc, vc = mla_prep(hb, w_cq, w_ckv, w_kr, row1(q_norm_g[l]), row1(kv_norm_g[l]),
                              _layout_w_uq(w_uq[l]).astype(bf16), _layout_w_ukv(w_ukv[l]).astype(bf16), cosm, sinm)
        ya = attn_a(qkvg, a_bias[l], batch, seq)
        cb = conv_module(qkvg, conv_w[l], conv_b[l], conv_ln_g[l], conv_ln_b[l], batch, seq)
        yc = mla_attn(qc, kc, vc, batch, seq)
        h1f, lo, hi = mix_layer(ya, cb, yc, hb, hf, w_oa[l].astype(bf16), w_pw2[l].astype(bf16),
                                w_oc[l].astype(bf16), w_gate[l].astype(bf16), row1(b_gate[l]),
                                w_out[l].astype(bf16), row1(ln1_g[l]), row1(ln1_b[l]))
        hf, hb = _moe(h1f, lo, hi, w_router[l].T, b_router[l].reshape(N_EXPERTS, 1),
                      w1[l].astype(bf16), b1[l].reshape(N_EXPERTS, 1, -1), w2[l].astype(bf16),
                      b2[l].reshape(N_EXPERTS, 1, -1), row1(ln2_g[l]), row1(ln2_b[l]))
    return hf.reshape(batch, seq, d)
```

```python
import functools

import jax
import jax.numpy as jnp
from jax import lax
from jax.experimental import pallas as pl
from jax.experimental.pallas import tpu as pltpu
from jax.experimental.pallas import tpu_sc as plsc

f32 = jnp.float32
bf16 = jnp.bfloat16
i32 = jnp.int32

D_MODEL = 1024
DEPTH = 4
CHUNK = 64
A_HEADS = 8
A_HEAD_DIM = 64
A_WIDTH = A_HEADS * A_HEAD_DIM
A_LEFT = 8 * CHUNK
A_MAX_REL = 128
CONV_CH = 512
CONV_WIDTH = 31
C_HEADS = 8
C_NOPE = 64
C_ROPE = 32
C_V = 64
C_QK = C_NOPE + C_ROPE
C_Q_RANK = 384
C_KV_RANK = 256
ROPE_THETA = 10000.0
N_EXPERTS = 32
TOP_K = 4
D_EXPERT = 1024
SWIGLU_ALPHA = 1.702
SWIGLU_LIMIT = 7.0
DN_ALPHA = (2 * DEPTH) ** 0.25
LN_EPS = 1e-5
RMS_EPS = 1e-6
NEG_INF = -1e30
LOG2E = 1.4426950408889634

LANES = 128
A_QBLK = 2 * CHUNK
A_STEP = 2 * A_QBLK
A_BAND = A_LEFT + A_QBLK
C_BLK = 256
C_STEP = 2 * C_BLK
C_SLOT = 256
CONV_BLK = 256
CONV_HALO = 32
MOE_BLK = 512
ROW_TILE = 512
MIX_TILE = 512
SUB_ROWS = 256
ROUTER_TILE = 512
HALF = D_MODEL // 2
QUART = D_MODEL // 4
SC_WINDOW = 128
VMEM_LIMIT = 56 * 1024 * 1024


def _cparams(*sem):
    return pltpu.CompilerParams(dimension_semantics=tuple(sem), vmem_limit_bytes=VMEM_LIMIT)


def _layer_norm(x, g, b):
    mu = jnp.mean(x, axis=-1, keepdims=True)
    xc = x - mu
    var = jnp.mean(xc * xc, axis=-1, keepdims=True)
    return xc * lax.rsqrt(var + LN_EPS) * g + b


def _sigmoid(x):
    return 1.0 / (1.0 + jnp.exp(-x))


def _pack2(a, b):
    ab = lax.bitcast_convert_type(a.astype(bf16).astype(f32), i32)
    bb = lax.bitcast_convert_type(b.astype(bf16).astype(f32), i32)
    return lax.shift_right_logical(ab, 16) | (bb & jnp.int32(-65536))


def _unpack_lo(w):
    return lax.bitcast_convert_type(lax.shift_left(w, 16), f32)


def _unpack_hi(w):
    return lax.bitcast_convert_type(w & jnp.int32(-65536), f32)


def _pack_row(y):
    return (_pack2(y[:, 0:QUART], y[:, 2 * QUART:3 * QUART]),
            _pack2(y[:, QUART:2 * QUART], y[:, 3 * QUART:4 * QUART]))


def _ln_in_kernel(x_ref, g_ref, b_ref, hf_ref, hb_ref):
    y = _layer_norm(x_ref[...], g_ref[...], b_ref[...])
    hf_ref[...] = y
    hb_ref[...] = y.astype(bf16)


def ln_in(x2d, g, b):
    n, d = x2d.shape
    row = pl.BlockSpec((ROW_TILE, d), lambda i: (i, 0))
    vec = pl.BlockSpec((1, d), lambda i: (0, 0))
    return pl.pallas_call(
        _ln_in_kernel, grid=(n // ROW_TILE,),
        in_specs=[row, vec, vec], out_specs=[row, row],
        out_shape=[jax.ShapeDtypeStruct((n, d), f32), jax.ShapeDtypeStruct((n, d), bf16)],
        compiler_params=_cparams("parallel"), name="ln_in",
    )(x2d, g.reshape(1, d), b.reshape(1, d))


def _proj_kernel(x_ref, w_ref, o_ref):
    acc = jnp.dot(x_ref[...], w_ref[...], preferred_element_type=f32)
    o_ref[:, :A_WIDTH] = (acc[:, :A_WIDTH] * (A_HEAD_DIM ** -0.5 * LOG2E)).astype(o_ref.dtype)
    o_ref[:, A_WIDTH:] = acc[:, A_WIDTH:].astype(o_ref.dtype)


def proj(hb, w):
    n, d = hb.shape
    c = w.shape[1]
    return pl.pallas_call(
        _proj_kernel, grid=(n // ROW_TILE,),
        in_specs=[pl.BlockSpec((ROW_TILE, d), lambda i: (i, 0)), pl.BlockSpec((d, c), lambda i: (0, 0))],
        out_specs=pl.BlockSpec((ROW_TILE, c), lambda i: (i, 0)),
        out_shape=jax.ShapeDtypeStruct((n, c), bf16),
        compiler_params=_cparams("parallel"), name="proj",
    )(hb, w)


def _mla_prep_kernel(hb_ref, wcq_ref, wckv_ref, wkr_ref, gq_ref, gkv_ref, wuq_ref, wukv_ref, cos_ref, sin_ref,
                     qc_ref, kc_ref, vc_ref):
    hb = hb_ref[...]
    cosm = cos_ref[...]
    sinm = sin_ref[...]
    lane = lax.broadcasted_iota(i32, cosm.shape, 1)
    half = C_ROPE // 2
    scale = C_QK ** -0.5 * LOG2E

    def rope(x):
        swapped = jnp.where(lane < half, pltpu.roll(x, LANES - half, 1), pltpu.roll(x, half, 1))
        return x * cosm + swapped * sinm

    def rms(x, g):
        return (x * lax.rsqrt(jnp.mean(x * x, axis=-1, keepdims=True) + RMS_EPS) * g).astype(bf16)

    cqn = rms(jnp.dot(hb, wcq_ref[...], preferred_element_type=f32), gq_ref[...])
    q = jnp.dot(cqn, wuq_ref[...], preferred_element_type=f32)
    for h in range(C_HEADS):
        lo = h * C_SLOT
        qc_ref[:, lo:lo + LANES] = (q[:, lo:lo + LANES] * scale).astype(bf16)
        qc_ref[:, lo + LANES:lo + C_SLOT] = (rope(q[:, lo + LANES:lo + C_SLOT]) * scale).astype(bf16)
    ckvn = rms(jnp.dot(hb, wckv_ref[...], preferred_element_type=f32), gkv_ref[...])
    kv = jnp.dot(ckvn, wukv_ref[...], preferred_element_type=f32)
    kr = rope(jnp.dot(hb, wkr_ref[...], preferred_element_type=f32)).astype(bf16)
    for j in range(C_HEADS // 2):
        lo = j * C_SLOT
        kc_ref[:, lo:lo + LANES] = kv[:, j * LANES:(j + 1) * LANES].astype(bf16)
        kc_ref[:, lo + LANES:lo + C_SLOT] = kr
    vc_ref[...] = kv[:, C_HEADS * C_NOPE:].astype(bf16)


def mla_prep(hb, wcq, wckv, wkr, gq, gkv, wuq, wukv, cosm, sinm):
    n, d = hb.shape
    t = ROW_TILE
    full = lambda a: pl.BlockSpec(a.shape, lambda i: (0,) * a.ndim)
    rowb = lambda c: pl.BlockSpec((t, c), lambda i: (i, 0))
    qw, kw, vw = C_HEADS * C_SLOT, (C_HEADS // 2) * C_SLOT, C_HEADS * C_V
    return pl.pallas_call(
        _mla_prep_kernel, grid=(n // t,),
        in_specs=[rowb(d), full(wcq), full(wckv), full(wkr), full(gq), full(gkv), full(wuq), full(wukv),
                  rowb(LANES), rowb(LANES)],
        out_specs=[rowb(qw), rowb(kw), rowb(vw)],
        out_shape=[jax.ShapeDtypeStruct((n, qw), bf16), jax.ShapeDtypeStruct((n, kw), bf16),
                   jax.ShapeDtypeStruct((n, vw), bf16)],
        compiler_params=_cparams("parallel"), name="mla_prep",
    )(hb, wcq, wckv, wkr, gq, gkv, wuq, wukv, cosm, sinm)


def _attn_a_kernel(q_ref, k_ref, v_ref, bias_ref, o_ref, kpad, vpad):
    qi = pl.program_id(1)
    seq = k_ref.shape[0]

    @pl.when(qi == 0)
    def _():
        kpad[0:A_LEFT, :] = jnp.zeros((A_LEFT, A_WIDTH), bf16)
        vpad[0:A_LEFT, :] = jnp.zeros((A_LEFT, A_WIDTH), bf16)
        kpad[A_LEFT:A_LEFT + seq, :] = k_ref[...]
        vpad[A_LEFT:A_LEFT + seq, :] = v_ref[...]

    col = lax.broadcasted_iota(i32, (2 * A_QBLK, A_BAND), 1)
    lane = lax.broadcasted_iota(i32, (A_QBLK, LANES), 1)
    ones = jnp.ones((A_BAND, LANES), bf16)
    for sub in range(A_STEP // A_QBLK):
        start = pl.multiple_of(qi * A_STEP + sub * A_QBLK, A_QBLK)
        before_start = jnp.where(col + start >= A_LEFT, 0.0, NEG_INF)
        rows = slice(sub * A_QBLK, (sub + 1) * A_QBLK)
        for j in range(A_HEADS // 2):
            cs = slice(j * LANES, (j + 1) * LANES)
            qp = q_ref[rows, cs].astype(f32)
            qs = jnp.concatenate([jnp.where(lane < A_HEAD_DIM, qp, 0.0), jnp.where(lane >= A_HEAD_DIM, qp, 0.0)],
                                 axis=0).astype(bf16)
            kb = kpad[pl.ds(start, A_BAND), cs]
            vb = jnp.concatenate([vpad[pl.ds(start, A_BAND), cs], ones], axis=1)
            s = lax.dot_general(qs, kb, (((1,), (1,)), ((), ())), preferred_element_type=f32)
            s = s + bias_ref[j] + before_start
            p = jnp.exp2((s - jnp.max(s, axis=-1, keepdims=True)).astype(bf16))
            o = jnp.dot(p, vb, preferred_element_type=f32)
            o = o[:, :LANES] / o[:, LANES:]
            o_ref[rows, cs] = jnp.where(lane < A_HEAD_DIM, o[:A_QBLK], o[A_QBLK:]).astype(bf16)


def attn_a(qkvg, bias, batch, seq):
    n = batch * seq
    nq = seq // A_STEP
    return pl.pallas_call(
        _attn_a_kernel, grid=(batch, nq),
        in_specs=[pl.BlockSpec((A_STEP, A_WIDTH), lambda b, i: (b * nq + i, 0)),
                  pl.BlockSpec((seq, A_WIDTH), lambda b, i: (b, 1)),
                  pl.BlockSpec((seq, A_WIDTH), lambda b, i: (b, 2)),
                  pl.BlockSpec(bias.shape, lambda b, i: (0, 0, 0))],
        out_specs=pl.BlockSpec((A_STEP, A_WIDTH), lambda b, i: (b * nq + i, 0)),
        out_shape=jax.ShapeDtypeStruct((n, A_WIDTH), bf16),
        scratch_shapes=[pltpu.VMEM((A_LEFT + seq, A_WIDTH), bf16), pltpu.VMEM((A_LEFT + seq, A_WIDTH), bf16)],
        compiler_params=_cparams("parallel", "arbitrary"), name="attn_a",
    )(qkvg, qkvg, qkvg, bias)


def _conv_kernel(a_ref, g_ref, ap_ref, gp_ref, w_ref, cb_ref, lg_ref, lb_ref, o_ref, hs, sh):
    i = pl.program_id(1)
    t = CONV_BLK
    tail = slice(t - CONV_HALO, t)
    prev = ap_ref[tail, :].astype(f32) * _sigmoid(gp_ref[tail, :].astype(f32))
    hs[0:CONV_HALO, :] = jnp.where(i > 0, prev, 0.0)
    hs[CONV_HALO:CONV_HALO + t, :] = a_ref[...].astype(f32) * _sigmoid(g_ref[...].astype(f32))
    first = CONV_HALO - CONV_WIDTH + 1
    a_of = [[a for a in range(CONV_HALO // 8 + 1) if first <= 8 * a + b <= CONV_HALO] for b in range(8)]
    for b in range(1, 8):
        length = 8 * max(a_of[b]) + t
        sh[b - 1, 0:length, :] = hs[b:b + length, :]
    rows = 64
    for r0 in range(0, t, rows):
        acc = jnp.zeros((rows, CONV_CH), f32) + cb_ref[...]
        for b in range(8):
            for a in a_of[b]:
                tap = 8 * a + b - first
                lo = r0 + 8 * a
                src = hs[lo:lo + rows, :] if b == 0 else sh[b - 1, lo:lo + rows, :]
                acc = acc + src * w_ref[tap:tap + 1, :]
        y = _layer_norm(acc, lg_ref[...], lb_ref[...])
        o_ref[r0:r0 + rows, :] = (y * _sigmoid(y)).astype(bf16)


def conv_module(qkvg, w_dw, b_dw, ln_g, ln_b, batch, seq):
    n = batch * seq
    nb = seq // CONV_BLK
    a_col = 3 * A_WIDTH // CONV_CH
    cur = lambda c: pl.BlockSpec((CONV_BLK, CONV_CH), lambda b, i: (b * nb + i, c))
    prv = lambda c: pl.BlockSpec((CONV_BLK, CONV_CH), lambda b, i: (b * nb + jnp.maximum(i - 1, 0), c))
    vec = pl.BlockSpec((1, CONV_CH), lambda b, i: (0, 0))
    return pl.pallas_call(
        _conv_kernel, grid=(batch, nb),
        in_specs=[cur(a_col), cur(a_col + 1), prv(a_col), prv(a_col + 1),
                  pl.BlockSpec((CONV_WIDTH, CONV_CH), lambda b, i: (0, 0)), vec, vec, vec],
        out_specs=pl.BlockSpec((CONV_BLK, CONV_CH), lambda b, i: (b * nb + i, 0)),
        out_shape=jax.ShapeDtypeStruct((n, CONV_CH), bf16),
        scratch_shapes=[pltpu.VMEM((CONV_HALO + CONV_BLK, CONV_CH), f32),
                        pltpu.VMEM((7, CONV_HALO + CONV_BLK, CONV_CH), f32)],
        compiler_params=_cparams("parallel", "parallel"), name="conv_module",
    )(qkvg, qkvg, qkvg, qkvg, w_dw, b_dw.reshape(1, -1), ln_g.reshape(1, -1), ln_b.reshape(1, -1))


def _mla_attn_kernel(q_ref, k_ref, v_ref, o_ref, acc_ref, sa_ref, sb_ref):
    qi = pl.program_id(1)
    t = C_BLK
    pairs = C_HEADS // 2
    row = lax.broadcasted_iota(i32, (2 * t, t), 0)
    col = lax.broadcasted_iota(i32, (2 * t, t), 1)
    diag_ok = (col // CHUNK) <= ((row % t) // CHUNK)
    lane = lax.broadcasted_iota(i32, (t, LANES), 1)
    ones = jnp.ones((t, LANES), bf16)
    subs = C_STEP // t
    both = tuple(range(subs))
    acc_ref[...] = jnp.zeros(acc_ref.shape, f32)

    def scores(kb, s_ref, sub_list):
        k0 = pl.multiple_of(kb * t, t)
        for j in range(pairs):
            kblk = k_ref[pl.ds(k0, t), j * C_SLOT:(j + 1) * C_SLOT]
            for sub in sub_list:
                rows = slice(sub * t, (sub + 1) * t)
                qs = jnp.concatenate([q_ref[rows, (2 * j) * C_SLOT:(2 * j + 1) * C_SLOT],
                                      q_ref[rows, (2 * j + 1) * C_SLOT:(2 * j + 2) * C_SLOT]], axis=0)
                s_ref[sub * pairs + j] = lax.dot_general(qs, kblk, (((1,), (1,)), ((), ())),
                                                         preferred_element_type=f32)

    def absorb(kb, s_ref, ms, plan):
        k0 = pl.multiple_of(kb * t, t)
        ms = list(ms)
        for j in range(pairs):
            vext = jnp.concatenate([v_ref[pl.ds(k0, t), j * LANES:(j + 1) * LANES], ones], axis=1)
            for sub, masked in plan:
                c = sub * pairs + j
                s = s_ref[c]
                if masked:
                    s = jnp.where(diag_ok, s, NEG_INF)
                m_new = jnp.maximum(ms[c], jnp.max(s, axis=-1, keepdims=True))
                a = jnp.exp2(ms[c] - m_new)
                p = jnp.exp2((s - m_new).astype(bf16))
                acc_ref[c] = a * acc_ref[c] + jnp.dot(p, vext, preferred_element_type=f32)
                ms[c] = m_new
        return tuple(ms)

    def two_blocks(i, ms):
        kb = subs * i
        scores(kb + 1, sb_ref, both)
        ms = absorb(kb, sa_ref, ms, [(0, False), (1, False)])
        scores(kb + 2, sa_ref, both)
        return absorb(kb + 1, sb_ref, ms, [(0, False), (1, False)])

    init = tuple(jnp.full((2 * t, 1), NEG_INF, f32) for _ in range(subs * pairs))
    scores(0, sa_ref, both)
    ms = lax.fori_loop(0, qi, two_blocks, init)
    scores(subs * qi + 1, sb_ref, (1,))
    ms = absorb(subs * qi, sa_ref, ms, [(0, True), (1, False)])
    absorb(subs * qi + 1, sb_ref, ms, [(1, True)])
    for sub in range(subs):
        for j in range(pairs):
            acc = acc_ref[sub * pairs + j]
            o = acc[:, :LANES] / acc[:, LANES:]
            o_ref[sub * t:(sub + 1) * t, j * LANES:(j + 1) * LANES] = (
                jnp.where(lane < C_V, o[:t], o[t:]).astype(bf16))


def mla_attn(qc, kc, vc, batch, seq):
    n = batch * seq
    nq = seq // C_STEP
    return pl.pallas_call(
        _mla_attn_kernel, grid=(batch, nq),
        in_specs=[pl.BlockSpec((C_STEP, qc.shape[1]), lambda b, i: (b * nq + i, 0)),
                  pl.BlockSpec((seq, kc.shape[1]), lambda b, i: (b, 0)),
                  pl.BlockSpec((seq, vc.shape[1]), lambda b, i: (b, 0))],
        out_specs=pl.BlockSpec((C_STEP, C_HEADS * C_V), lambda b, i: (b * nq + i, 0)),
        out_shape=jax.ShapeDtypeStruct((n, C_HEADS * C_V), bf16),
        scratch_shapes=[pltpu.VMEM((C_STEP // C_BLK * (C_HEADS // 2), 2 * C_BLK, 2 * LANES), f32),
                        pltpu.VMEM((C_STEP // C_BLK * (C_HEADS // 2), 2 * C_BLK, C_BLK), f32),
                        pltpu.VMEM((C_STEP // C_BLK * (C_HEADS // 2), 2 * C_BLK, C_BLK), f32)],
        compiler_params=_cparams("parallel", "parallel"), name="mla_attn",
    )(qc, kc, vc)


def _mix_kernel(ya_ref, cb_ref, yc_ref, hb_ref, hf_ref, woa_ref, wpw_ref, woc_ref, wg_ref, bg_ref, wout_ref,
                g_ref, b_ref, of_ref, lo_ref, hi_ref):
    d = D_MODEL
    for r0 in range(0, MIX_TILE, SUB_ROWS):
        rows = slice(r0, r0 + SUB_ROWS)
        gates = _sigmoid(jnp.dot(hb_ref[rows, :], wg_ref[...], preferred_element_type=f32) + bg_ref[...])
        mix = (gates[:, 0:d] * jnp.dot(ya_ref[rows, :], woa_ref[...], preferred_element_type=f32)
               + gates[:, d:2 * d] * jnp.dot(cb_ref[rows, :], wpw_ref[...], preferred_element_type=f32)
               + gates[:, 2 * d:3 * d] * jnp.dot(yc_ref[rows, :], woc_ref[...], preferred_element_type=f32))
        z = DN_ALPHA * hf_ref[rows, :] + jnp.dot(mix.astype(bf16), wout_ref[...], preferred_element_type=f32)
        y = _layer_norm(z, g_ref[...], b_ref[...])
        of_ref[rows, :] = y
        lo, hi = _pack_row(y)
        lo_ref[rows, :] = lo
        hi_ref[rows, :] = hi


def mix_layer(ya, cb, yc, hb, hf, woa, wpw, woc, wg, bg, wout, g, b):
    n, d = hf.shape
    t = MIX_TILE
    full = lambda a: pl.BlockSpec(a.shape, lambda i: (0,) * a.ndim)
    rowb = lambda c: pl.BlockSpec((t, c), lambda i: (i, 0))
    return pl.pallas_call(
        _mix_kernel, grid=(n // t,),
        in_specs=[rowb(ya.shape[1]), rowb(cb.shape[1]), rowb(yc.shape[1]), rowb(d), rowb(d),
                  full(woa), full(wpw), full(woc), full(wg), full(bg), full(wout), full(g), full(b)],
        out_specs=[rowb(d), rowb(QUART), rowb(QUART)],
        out_shape=[jax.ShapeDtypeStruct((n, d), f32), jax.ShapeDtypeStruct((n, QUART), i32),
                   jax.ShapeDtypeStruct((n, QUART), i32)],
        compiler_params=_cparams("parallel"), name="mix_layer",
    )(ya, cb, yc, hb, hf, woa, wpw, woc, wg, bg, wout, g, b)


def _router_kernel(h_ref, wr_ref, br_ref, idx_ref, gate_ref, rank_ref, cnt_ref, base):
    i = pl.program_id(0)
    t = ROUTER_TILE
    e = N_EXPERTS

    @pl.when(i == 0)
    def _():
        base[...] = jnp.zeros_like(base)

    def split(x):
        hi = x.astype(bf16)
        return hi, (x - hi.astype(f32)).astype(bf16)

    nt = lambda a, b: lax.dot_general(a, b, (((1,), (1,)), ((), ())), preferred_element_type=f32)
    w_hi, w_lo = split(wr_ref[...])
    h_hi, h_lo = split(h_ref[...])
    logits = nt(w_hi, h_hi) + nt(w_lo, h_hi) + nt(w_hi, h_lo) + br_ref[...]
    row = lax.broadcasted_iota(i32, (e, t), 0).astype(f32)
    vals, hots = [], []
    cur = logits
    for k in range(TOP_K):
        m = jnp.max(cur, axis=0, keepdims=True)
        first = jnp.min(jnp.where(cur == m, row, float(e)), axis=0, keepdims=True)
        hot = row == first
        cur = jnp.where(hot, -jnp.inf, cur)
        vals.append(m)
        hots.append(hot)
        idx_ref[k:k + 1, :] = first.astype(i32)
    ex = [jnp.exp(v - vals[0]) for v in vals]
    den = ex[0] + ex[1] + ex[2] + ex[3]
    for k in range(TOP_K):
        gate_ref[k:k + 1, :] = ex[k] / den
    onehot = jnp.concatenate([jnp.where(h, 1.0, 0.0) for h in hots], axis=0)
    r = lax.broadcasted_iota(i32, (t, t), 0)
    c = lax.broadcasted_iota(i32, (t, t), 1)
    upper = jnp.where(r <= c, 1.0, 0.0).astype(bf16)
    prefix = jnp.dot(onehot.astype(bf16), upper, preferred_element_type=f32)
    counts = jnp.sum(onehot, axis=1, keepdims=True)
    offset = base[:, 0:1]
    for k in range(TOP_K):
        sel = jnp.where(hots[k], prefix[k * e:(k + 1) * e, :] - 1.0 + offset, 0.0)
        rank_ref[k:k + 1, :] = jnp.sum(sel, axis=0, keepdims=True).astype(i32)
        offset = offset + counts[k * e:(k + 1) * e, :]
    base[...] = jnp.broadcast_to(offset, base.shape)
    cnt_ref[...] = base[...]


def router(hf, w_rt, b_r):
    n, d = hf.shape
    t = ROUTER_TILE
    tok = pl.BlockSpec((TOP_K, t), lambda i: (0, i))
    return pl.pallas_call(
        _router_kernel, grid=(n // t,),
        in_specs=[pl.BlockSpec((t, d), lambda i: (i, 0)), pl.BlockSpec((N_EXPERTS, d), lambda i: (0, 0)),
                  pl.BlockSpec((N_EXPERTS, 1), lambda i: (0, 0))],
        out_specs=[tok, tok, tok, pl.BlockSpec((N_EXPERTS, LANES), lambda i: (0, 0))],
        out_shape=[jax.ShapeDtypeStruct((TOP_K, n), i32), jax.ShapeDtypeStruct((TOP_K, n), f32),
                   jax.ShapeDtypeStruct((TOP_K, n), i32), jax.ShapeDtypeStruct((N_EXPERTS, LANES), f32)],
        scratch_shapes=[pltpu.VMEM((N_EXPERTS, LANES), f32)],
        compiler_params=_cparams("arbitrary"), name="router",
    )(hf, w_rt, b_r)


def _dest_kernel(idx_ref, rank_ref, start_ref, dest_ref):
    t = idx_ref.shape[1]
    row = lax.broadcasted_iota(i32, (N_EXPERTS, t), 0)
    for k in range(TOP_K):
        hot = row == idx_ref[k:k + 1, :]
        off = jnp.sum(jnp.where(hot, start_ref[...], 0.0), axis=0, keepdims=True)
        dest_ref[k:k + 1, :] = rank_ref[k:k + 1, :] + off.astype(i32)


def dest_rows(idx, rank, pad_start):
    n = idx.shape[1]
    t = ROUTER_TILE
    tok = pl.BlockSpec((TOP_K, t), lambda i: (0, i))
    return pl.pallas_call(
        _dest_kernel, grid=(n // t,),
        in_specs=[tok, tok, pl.BlockSpec((N_EXPERTS, 1), lambda i: (0, 0))],
        out_specs=tok, out_shape=jax.ShapeDtypeStruct((TOP_K, n), i32),
        compiler_params=_cparams("parallel"), name="dest_rows",
    )(idx, rank, pad_start)


def _sc_mesh():
    return plsc.VectorSubcoreMesh(core_axis_name="c", subcore_axis_name="s")


def sc_scatter_rows(x, dest, n_rows):
    n, d = x.shape
    kk = dest.shape[0]

    @functools.partial(pl.kernel, out_type=jax.ShapeDtypeStruct((n_rows, d), x.dtype), mesh=_sc_mesh(),
                       scratch_types=[])
    def k(x_hbm, i_hbm, o_hbm):
        def body(x_vmem, i_vmem):
            for j in range(kk):
                pltpu.sync_copy(x_vmem, o_hbm.at[i_vmem.at[j]])

        pltpu.emit_pipeline(
            body, grid=(n // SC_WINDOW,),
            in_specs=[pl.BlockSpec((SC_WINDOW, d), lambda i: (i, 0)),
                      pl.BlockSpec((kk, SC_WINDOW), lambda i: (0, i))],
            out_specs=[], core_axis_name=("c", "s"), dimension_semantics=(pltpu.PARALLEL,),
        )(x_hbm, i_hbm)

    return k(x, dest)


def sc_gather_rows(table, idx):
    m = idx.shape[1]
    d = table.shape[1]

    @functools.partial(pl.kernel, out_type=jax.ShapeDtypeStruct((m, d), table.dtype), mesh=_sc_mesh(),
                       scratch_types=[])
    def k(t_hbm, i_hbm, o_hbm):
        def body(i_vmem, o_vmem):
            pltpu.sync_copy(t_hbm.at[i_vmem.at[0]], o_vmem)

        pltpu.emit_pipeline(
            body, grid=(m // SC_WINDOW,),
            in_specs=[pl.BlockSpec((1, SC_WINDOW), lambda i: (0, i))],
            out_specs=[pl.BlockSpec((SC_WINDOW, d), lambda i: (i, 0))],
            core_axis_name=("c", "s"), dimension_semantics=(pltpu.PARALLEL,),
        )(i_hbm, o_hbm)

    return k(table, idx)


def _expert_kernel(be_ref, nu_ref, xlo_ref, xhi_ref, w1_ref, b1_ref, w2_ref, b2_ref, ylo_ref, yhi_ref, w1b, w2b):
    i = pl.program_id(0)
    new_expert = jnp.logical_or(i == 0, be_ref[i] != be_ref[jnp.maximum(i - 1, 0)])

    @pl.when(jnp.logical_and(i < nu_ref[0], new_expert))
    def _():
        w1b[...] = w1_ref[0, 0].astype(bf16)
        w2b[...] = w2_ref[0, 0].astype(bf16)

    @pl.when(i < nu_ref[0])
    def _():
        for r0 in range(0, MOE_BLK, SUB_ROWS):
            rows = slice(r0, r0 + SUB_ROWS)
            lo = xlo_ref[rows, :]
            hi = xhi_ref[rows, :]
            x = jnp.concatenate([_unpack_lo(lo).astype(bf16), _unpack_lo(hi).astype(bf16),
                                 _unpack_hi(lo).astype(bf16), _unpack_hi(hi).astype(bf16)], axis=1)
            u = jnp.dot(x, w1b[...], preferred_element_type=f32) + b1_ref[0, 0]
            glu = jnp.minimum(u[:, :D_EXPERT], SWIGLU_LIMIT)
            lin = jnp.clip(u[:, D_EXPERT:], -SWIGLU_LIMIT, SWIGLU_LIMIT)
            act = (glu * _sigmoid(SWIGLU_ALPHA * glu) * (lin + 1.0)).astype(bf16)
            y = jnp.dot(act, w2b[...], preferred_element_type=f32) + b2_ref[0, 0]
            ylo, yhi = _pack_row(y)
            ylo_ref[rows, :] = ylo
            yhi_ref[rows, :] = yhi


def expert_ffn(layer, block_expert, n_used, xs_lo, xs_hi, w1, b1, w2, b2):
    n_rows = xs_lo.shape[0]
    nb = n_rows // MOE_BLK
    rows = pl.BlockSpec((MOE_BLK, QUART), lambda i, be, nu: (jnp.minimum(i, nu[0] - 1), 0))
    wsel = lambda shape: pl.BlockSpec((1, 1) + shape, lambda i, be, nu: (layer, be[i], 0, 0))
    out = jax.ShapeDtypeStruct((n_rows, QUART), i32)
    return pl.pallas_call(
        _expert_kernel,
        grid_spec=pltpu.PrefetchScalarGridSpec(
            num_scalar_prefetch=2, grid=(nb,),
            in_specs=[rows, rows, wsel((D_MODEL, 2 * D_EXPERT)), wsel((1, 2 * D_EXPERT)),
                      wsel((D_EXPERT, D_MODEL)), wsel((1, D_MODEL))],
            out_specs=[rows, rows],
            scratch_shapes=[pltpu.VMEM((D_MODEL, 2 * D_EXPERT), bf16), pltpu.VMEM((D_EXPERT, D_MODEL), bf16)]),
        out_shape=[out, out],
        compiler_params=_cparams("arbitrary"), name="expert_ffn",
    )(block_expert, n_used, xs_lo, xs_hi, w1, b1, w2, b2)


def _combine_kernel(ylo_ref, yhi_ref, gt_ref, hf_ref, g_ref, b_ref, of_ref, ob_ref):
    gt = gt_ref[...]
    parts = [None] * 4
    for k in range(TOP_K):
        gk = gt[:, k:k + 1]
        lo = ylo_ref[k]
        hi = yhi_ref[k]
        vals = (_unpack_lo(lo), _unpack_lo(hi), _unpack_hi(lo), _unpack_hi(hi))
        for p in range(4):
            parts[p] = gk * vals[p] if parts[p] is None else parts[p] + gk * vals[p]
    z = DN_ALPHA * hf_ref[...] + jnp.concatenate(parts, axis=1)
    y = _layer_norm(z, g_ref[...], b_ref[...])
    of_ref[...] = y
    ob_ref[...] = y.astype(bf16)


def combine(yk_lo, yk_hi, gates_t, hf, g, b):
    n, d = hf.shape
    t = MIX_TILE
    ysp = pl.BlockSpec((TOP_K, t, QUART), lambda i: (0, i, 0))
    rowb = lambda c: pl.BlockSpec((t, c), lambda i: (i, 0))
    vec = pl.BlockSpec((1, d), lambda i: (0, 0))
    return pl.pallas_call(
        _combine_kernel, grid=(n // t,),
        in_specs=[ysp, ysp, rowb(TOP_K), rowb(d), vec, vec],
        out_specs=[rowb(d), rowb(d)],
        out_shape=[jax.ShapeDtypeStruct((n, d), f32), jax.ShapeDtypeStruct((n, d), bf16)],
        compiler_params=_cparams("parallel"), name="combine",
    )(yk_lo, yk_hi, gates_t, hf, g, b)


def _rope_tables(positions):
    inv = ROPE_THETA ** (-jnp.arange(0, C_ROPE, 2, dtype=f32) / C_ROPE)
    ang = positions.reshape(-1).astype(f32)[:, None] * inv
    cos, sin = jnp.cos(ang), jnp.sin(ang)
    pad = jnp.zeros((ang.shape[0], LANES - C_ROPE), f32)
    return jnp.concatenate([cos, cos, pad], axis=1), jnp.concatenate([-sin, sin, pad], axis=1)


def _attn_a_bias(rel_tables):
    depth = rel_tables.shape[0]
    r = jnp.arange(A_QBLK)[:, None]
    c = jnp.arange(A_BAND)[None, :]
    rel = jnp.clip(c - A_LEFT - r, -A_MAX_REL, A_MAX_REL) + A_MAX_REL
    own = c - CHUNK * (r // CHUNK)
    valid = (own >= 0) & (own < A_LEFT + CHUNK)
    onehot = (rel[:, :, None] == jnp.arange(2 * A_MAX_REL + 1)[None, None, :]).astype(f32)
    bias = jnp.einsum("rck,lhk->lhrc", onehot, rel_tables.astype(f32), precision=lax.Precision.HIGHEST)
    bias = jnp.where(valid[None, None], bias * LOG2E, NEG_INF)
    return bias.reshape(depth, A_HEADS // 2, 2 * A_QBLK, A_BAND)


def _layout_w_uq(w_uq):
    w = w_uq.reshape(C_Q_RANK, C_HEADS, C_QK)
    out = jnp.zeros((C_Q_RANK, C_HEADS, C_SLOT), w_uq.dtype)
    for h in range(C_HEADS):
        off = (h % 2) * C_NOPE
        out = out.at[:, h, off:off + C_NOPE].set(w[:, h, :C_NOPE])
        out = out.at[:, h, LANES:LANES + C_ROPE].set(w[:, h, C_NOPE:])
    return out.reshape(C_Q_RANK, C_HEADS * C_SLOT)


def _layout_w_ukv(w_ukv):
    w = w_ukv.reshape(C_KV_RANK, C_HEADS, C_NOPE + C_V)
    return jnp.concatenate([w[:, :, :C_NOPE].reshape(C_KV_RANK, -1), w[:, :, C_NOPE:].reshape(C_KV_RANK, -1)], axis=1)


def _moe(layer, hf, lo, hi, w_rt, b_r, w1, b1, w2, b2, g2, bb2):
    n = hf.shape[0]
    n_rows = n * TOP_K + N_EXPERTS * MOE_BLK
    nb = n_rows // MOE_BLK
    idx, gates, rank, cnt = router(hf, w_rt, b_r)
    counts = cnt[:, 0].astype(i32)
    padded = (counts + MOE_BLK - 1) // MOE_BLK * MOE_BLK
    pad_end = jnp.cumsum(padded)
    pad_start = (pad_end - padded).astype(f32).reshape(N_EXPERTS, 1)
    block_start = jnp.arange(nb, dtype=i32) * MOE_BLK
    block_expert = jnp.minimum(jnp.sum((pad_end[None, :] <= block_start[:, None]).astype(i32), axis=1),
                               N_EXPERTS - 1)
    n_used = (pad_end[-1:] // MOE_BLK).astype(i32)
    dest = dest_rows(idx, rank, pad_start)
    xs_lo = sc_scatter_rows(lo, dest, n_rows)
    xs_hi = sc_scatter_rows(hi, dest, n_rows)
    ys_lo, ys_hi = expert_ffn(layer, block_expert, n_used, xs_lo, xs_hi, w1, b1, w2, b2)
    flat = dest.reshape(1, TOP_K * n)
    yk_lo = sc_gather_rows(ys_lo, flat)
    yk_hi = sc_gather_rows(ys_hi, flat)
    return combine(yk_lo.reshape(TOP_K, n, QUART), yk_hi.reshape(TOP_K, n, QUART), gates.T, hf, g2, bb2)


def kernel(x, positions, ln_in_g, ln_in_b, w_in, w_gate, b_gate, rel_bias, conv_w, conv_b, conv_ln_g, conv_ln_b, w_pw2, q_norm_g, kv_norm_g, w_uq, w_ukv, w_oa, w_oc, w_out, ln1_g, ln1_b, w_router, b_router, w1, b1, w2, b2, ln2_g, ln2_b):
    batch, seq, d = x.shape
    n = batch * seq
    cosm, sinm = _rope_tables(positions)
    a_bias = _attn_a_bias(rel_bias)
    hf, hb = ln_in(x.reshape(n, d), ln_in_g, ln_in_b)
    ab = 3 * A_WIDTH + 2 * CONV_CH
    row1 = lambda v: v.reshape(1, -1)
    for l in range(DEPTH):
        w_ab = w_in[l, :, :ab].astype(bf16)
        w_cq = w_in[l, :, ab:ab + C_Q_RANK].astype(bf16)
        w_ckv = w_in[l, :, ab + C_Q_RANK:ab + C_Q_RANK + C_KV_RANK].astype(bf16)
        w_kr = jnp.pad(w_in[l, :, ab + C_Q_RANK + C_KV_RANK:], ((0, 0), (0, LANES - C_ROPE))).astype(bf16)
        qkvg = proj(hb, w_ab)
        qc, kc, vc = mla_prep(hb, w_cq, w_ckv, w_kr, row1(q_norm_g[l]), row1(kv_norm_g[l]),
                              _layout_w_uq(w_uq[l]).astype(bf16), _layout_w_ukv(w_ukv[l]).astype(bf16), cosm, sinm)
        ya = attn_a(qkvg, a_bias[l], batch, seq)
        cb = conv_module(qkvg, conv_w[l], conv_b[l], conv_ln_g[l], conv_ln_b[l], batch, seq)
        yc = mla_attn(qc, kc, vc, batch, seq)
        h1f, lo, hi = mix_layer(ya, cb, yc, hb, hf, w_oa[l].astype(bf16), w_pw2[l].astype(bf16),
                                w_oc[l].astype(bf16), w_gate[l].astype(bf16), row1(b_gate[l]),
                                w_out[l].astype(bf16), row1(ln1_g[l]), row1(ln1_b[l]))
        hf, hb = _moe(l, h1f, lo, hi, w_router[l].T, b_router[l].reshape(N_EXPERTS, 1),
                      w1, b1.reshape(DEPTH, N_EXPERTS, 1, -1), w2, b2.reshape(DEPTH, N_EXPERTS, 1, -1),
                      row1(ln2_g[l]), row1(ln2_b[l]))
    return hf.reshape(batch, seq, d)
```

```python
import functools

import jax
import jax.numpy as jnp
from jax import lax
from jax.experimental import pallas as pl
from jax.experimental.pallas import tpu as pltpu
from jax.experimental.pallas import tpu_sc as plsc

f32 = jnp.float32
bf16 = jnp.bfloat16
i32 = jnp.int32

D_MODEL = 1024
DEPTH = 4
CHUNK = 64
A_HEADS = 8
A_HEAD_DIM = 64
A_WIDTH = A_HEADS * A_HEAD_DIM
A_LEFT = 8 * CHUNK
A_MAX_REL = 128
CONV_CH = 512
CONV_WIDTH = 31
C_HEADS = 8
C_NOPE = 64
C_ROPE = 32
C_V = 64
C_QK = C_NOPE + C_ROPE
C_Q_RANK = 384
C_KV_RANK = 256
ROPE_THETA = 10000.0
N_EXPERTS = 32
TOP_K = 4
D_EXPERT = 1024
SWIGLU_ALPHA = 1.702
SWIGLU_LIMIT = 7.0
DN_ALPHA = (2 * DEPTH) ** 0.25
LN_EPS = 1e-5
RMS_EPS = 1e-6
NEG_INF = -1e30
LOG2E = 1.4426950408889634

LANES = 128
A_QBLK = 2 * CHUNK
A_STEP = 2 * A_QBLK
A_BAND = A_LEFT + A_QBLK
C_BLK = 256
C_STEP = 2 * C_BLK
C_SLOT = 256
CONV_BLK = 256
CONV_HALO = 32
MOE_BLK = 512
ROW_TILE = 512
MIX_TILE = 512
SUB_ROWS = 256
ROUTER_TILE = 512
HALF = D_MODEL // 2
QUART = D_MODEL // 4
SC_WINDOW = 128
STREAMS = 2
VMEM_LIMIT = 56 * 1024 * 1024


def _cparams(*sem):
    return pltpu.CompilerParams(dimension_semantics=tuple(sem), vmem_limit_bytes=VMEM_LIMIT)


def _layer_norm(x, g, b):
    mu = jnp.mean(x, axis=-1, keepdims=True)
    xc = x - mu
    var = jnp.mean(xc * xc, axis=-1, keepdims=True)
    return xc * lax.rsqrt(var + LN_EPS) * g + b


def _sigmoid(x):
    return 1.0 / (1.0 + jnp.exp(-x))


def _pack2(a, b):
    ab = lax.bitcast_convert_type(a.astype(bf16).astype(f32), i32)
    bb = lax.bitcast_convert_type(b.astype(bf16).astype(f32), i32)
    return lax.shift_right_logical(ab, 16) | (bb & jnp.int32(-65536))


def _unpack_lo(w):
    return lax.bitcast_convert_type(lax.shift_left(w, 16), f32)


def _unpack_hi(w):
    return lax.bitcast_convert_type(w & jnp.int32(-65536), f32)


def _pack_row(y):
    return (_pack2(y[:, 0:QUART], y[:, 2 * QUART:3 * QUART]),
            _pack2(y[:, QUART:2 * QUART], y[:, 3 * QUART:4 * QUART]))


def _ln_in_kernel(x_ref, g_ref, b_ref, hf_ref, hb_ref):
    y = _layer_norm(x_ref[...], g_ref[...], b_ref[...])
    hf_ref[...] = y
    hb_ref[...] = y.astype(bf16)


def ln_in(x2d, g, b, first_row, n):
    d = x2d.shape[1]
    first_tile = first_row // ROW_TILE
    row = pl.BlockSpec((ROW_TILE, d), lambda i: (i, 0))
    vec = pl.BlockSpec((1, d), lambda i: (0, 0))
    return pl.pallas_call(
        _ln_in_kernel, grid=(n // ROW_TILE,),
        in_specs=[pl.BlockSpec((ROW_TILE, d), lambda i: (first_tile + i, 0)), vec, vec], out_specs=[row, row],
        out_shape=[jax.ShapeDtypeStruct((n, d), f32), jax.ShapeDtypeStruct((n, d), bf16)],
        compiler_params=_cparams("parallel"), name="ln_in",
    )(x2d, g.reshape(1, d), b.reshape(1, d))


def _proj_kernel(x_ref, w_ref, o_ref):
    acc = jnp.dot(x_ref[...], w_ref[...], preferred_element_type=f32)
    o_ref[:, :A_WIDTH] = (acc[:, :A_WIDTH] * (A_HEAD_DIM ** -0.5 * LOG2E)).astype(o_ref.dtype)
    o_ref[:, A_WIDTH:] = acc[:, A_WIDTH:].astype(o_ref.dtype)


def proj(hb, w):
    n, d = hb.shape
    c = w.shape[1]
    return pl.pallas_call(
        _proj_kernel, grid=(n // ROW_TILE,),
        in_specs=[pl.BlockSpec((ROW_TILE, d), lambda i: (i, 0)), pl.BlockSpec((d, c), lambda i: (0, 0))],
        out_specs=pl.BlockSpec((ROW_TILE, c), lambda i: (i, 0)),
        out_shape=jax.ShapeDtypeStruct((n, c), bf16),
        compiler_params=_cparams("parallel"), name="proj",
    )(hb, w)


def _mla_prep_kernel(hb_ref, wcq_ref, wckv_ref, wkr_ref, gq_ref, gkv_ref, wuq_ref, wukv_ref, cos_ref, sin_ref,
                     qc_ref, kc_ref, vc_ref):
    hb = hb_ref[...]
    cosm = cos_ref[...]
    sinm = sin_ref[...]
    lane = lax.broadcasted_iota(i32, cosm.shape, 1)
    half = C_ROPE // 2
    scale = C_QK ** -0.5 * LOG2E

    def rope(x):
        swapped = jnp.where(lane < half, pltpu.roll(x, LANES - half, 1), pltpu.roll(x, half, 1))
        return x * cosm + swapped * sinm

    def rms(x, g):
        return (x * lax.rsqrt(jnp.mean(x * x, axis=-1, keepdims=True) + RMS_EPS) * g).astype(bf16)

    cqn = rms(jnp.dot(hb, wcq_ref[...], preferred_element_type=f32), gq_ref[...])
    q = jnp.dot(cqn, wuq_ref[...], preferred_element_type=f32)
    for h in range(C_HEADS):
        lo = h * C_SLOT
        qc_ref[:, lo:lo + LANES] = (q[:, lo:lo + LANES] * scale).astype(bf16)
        qc_ref[:, lo + LANES:lo + C_SLOT] = (rope(q[:, lo + LANES:lo + C_SLOT]) * scale).astype(bf16)
    ckvn = rms(jnp.dot(hb, wckv_ref[...], preferred_element_type=f32), gkv_ref[...])
    kv = jnp.dot(ckvn, wukv_ref[...], preferred_element_type=f32)
    kr = rope(jnp.dot(hb, wkr_ref[...], preferred_element_type=f32)).astype(bf16)
    for j in range(C_HEADS // 2):
        lo = j * C_SLOT
        kc_ref[:, lo:lo + LANES] = kv[:, j * LANES:(j + 1) * LANES].astype(bf16)
        kc_ref[:, lo + LANES:lo + C_SLOT] = kr
    vc_ref[...] = kv[:, C_HEADS * C_NOPE:].astype(bf16)


def mla_prep(hb, wcq, wckv, wkr, gq, gkv, wuq, wukv, cosm, sinm):
    n, d = hb.shape
    t = ROW_TILE
    full = lambda a: pl.BlockSpec(a.shape, lambda i: (0,) * a.ndim)
    rowb = lambda c: pl.BlockSpec((t, c), lambda i: (i, 0))
    qw, kw, vw = C_HEADS * C_SLOT, (C_HEADS // 2) * C_SLOT, C_HEADS * C_V
    return pl.pallas_call(
        _mla_prep_kernel, grid=(n // t,),
        in_specs=[rowb(d), full(wcq), full(wckv), full(wkr), full(gq), full(gkv), full(wuq), full(wukv),
                  rowb(LANES), rowb(LANES)],
        out_specs=[rowb(qw), rowb(kw), rowb(vw)],
        out_shape=[jax.ShapeDtypeStruct((n, qw), bf16), jax.ShapeDtypeStruct((n, kw), bf16),
                   jax.ShapeDtypeStruct((n, vw), bf16)],
        compiler_params=_cparams("parallel"), name="mla_prep",
    )(hb, wcq, wckv, wkr, gq, gkv, wuq, wukv, cosm, sinm)


def _attn_a_kernel(q_ref, k_ref, v_ref, bias_ref, o_ref, kpad, vpad):
    qi = pl.program_id(1)
    seq = k_ref.shape[0]

    @pl.when(qi == 0)
    def _():
        kpad[0:A_LEFT, :] = jnp.zeros((A_LEFT, A_WIDTH), bf16)
        vpad[0:A_LEFT, :] = jnp.zeros((A_LEFT, A_WIDTH), bf16)
        kpad[A_LEFT:A_LEFT + seq, :] = k_ref[...]
        vpad[A_LEFT:A_LEFT + seq, :] = v_ref[...]

    col = lax.broadcasted_iota(i32, (2 * A_QBLK, A_BAND), 1)
    lane = lax.broadcasted_iota(i32, (A_QBLK, LANES), 1)
    ones = jnp.ones((A_BAND, LANES), bf16)
    for sub in range(A_STEP // A_QBLK):
        start = pl.multiple_of(qi * A_STEP + sub * A_QBLK, A_QBLK)
        before_start = jnp.where(col + start >= A_LEFT, 0.0, NEG_INF)
        rows = slice(sub * A_QBLK, (sub + 1) * A_QBLK)
        for j in range(A_HEADS // 2):
            cs = slice(j * LANES, (j + 1) * LANES)
            qp = q_ref[rows, cs].astype(f32)
            qs = jnp.concatenate([jnp.where(lane < A_HEAD_DIM, qp, 0.0), jnp.where(lane >= A_HEAD_DIM, qp, 0.0)],
                                 axis=0).astype(bf16)
            kb = kpad[pl.ds(start, A_BAND), cs]
            vb = jnp.concatenate([vpad[pl.ds(start, A_BAND), cs], ones], axis=1)
            s = lax.dot_general(qs, kb, (((1,), (1,)), ((), ())), preferred_element_type=f32)
            s = s + bias_ref[j] + before_start
            p = jnp.exp2((s - jnp.max(s, axis=-1, keepdims=True)).astype(bf16))
            o = jnp.dot(p, vb, preferred_element_type=f32)
            o = o[:, :LANES] / o[:, LANES:]
            o_ref[rows, cs] = jnp.where(lane < A_HEAD_DIM, o[:A_QBLK], o[A_QBLK:]).astype(bf16)


def attn_a(qkvg, bias, batch, seq):
    n = batch * seq
    nq = seq // A_STEP
    return pl.pallas_call(
        _attn_a_kernel, grid=(batch, nq),
        in_specs=[pl.BlockSpec((A_STEP, A_WIDTH), lambda b, i: (b * nq + i, 0)),
                  pl.BlockSpec((seq, A_WIDTH), lambda b, i: (b, 1)),
                  pl.BlockSpec((seq, A_WIDTH), lambda b, i: (b, 2)),
                  pl.BlockSpec(bias.shape, lambda b, i: (0, 0, 0))],
        out_specs=pl.BlockSpec((A_STEP, A_WIDTH), lambda b, i: (b * nq + i, 0)),
        out_shape=jax.ShapeDtypeStruct((n, A_WIDTH), bf16),
        scratch_shapes=[pltpu.VMEM((A_LEFT + seq, A_WIDTH), bf16), pltpu.VMEM((A_LEFT + seq, A_WIDTH), bf16)],
        compiler_params=_cparams("parallel", "arbitrary"), name="attn_a",
    )(qkvg, qkvg, qkvg, bias)


def _conv_kernel(a_ref, g_ref, ap_ref, gp_ref, w_ref, cb_ref, lg_ref, lb_ref, o_ref, hs, sh):
    i = pl.program_id(1)
    t = CONV_BLK
    tail = slice(t - CONV_HALO, t)
    prev = ap_ref[tail, :].astype(f32) * _sigmoid(gp_ref[tail, :].astype(f32))
    hs[0:CONV_HALO, :] = jnp.where(i > 0, prev, 0.0)
    hs[CONV_HALO:CONV_HALO + t, :] = a_ref[...].astype(f32) * _sigmoid(g_ref[...].astype(f32))
    first = CONV_HALO - CONV_WIDTH + 1
    a_of = [[a for a in range(CONV_HALO // 8 + 1) if first <= 8 * a + b <= CONV_HALO] for b in range(8)]
    for b in range(1, 8):
        length = 8 * max(a_of[b]) + t
        sh[b - 1, 0:length, :] = hs[b:b + length, :]
    rows = 64
    for r0 in range(0, t, rows):
        acc = jnp.zeros((rows, CONV_CH), f32) + cb_ref[...]
        for b in range(8):
            for a in a_of[b]:
                tap = 8 * a + b - first
                lo = r0 + 8 * a
                src = hs[lo:lo + rows, :] if b == 0 else sh[b - 1, lo:lo + rows, :]
                acc = acc + src * w_ref[tap:tap + 1, :]
        y = _layer_norm(acc, lg_ref[...], lb_ref[...])
        o_ref[r0:r0 + rows, :] = (y * _sigmoid(y)).astype(bf16)


def conv_module(qkvg, w_dw, b_dw, ln_g, ln_b, batch, seq):
    n = batch * seq
    nb = seq // CONV_BLK
    a_col = 3 * A_WIDTH // CONV_CH
    cur = lambda c: pl.BlockSpec((CONV_BLK, CONV_CH), lambda b, i: (b * nb + i, c))
    prv = lambda c: pl.BlockSpec((CONV_BLK, CONV_CH), lambda b, i: (b * nb + jnp.maximum(i - 1, 0), c))
    vec = pl.BlockSpec((1, CONV_CH), lambda b, i: (0, 0))
    return pl.pallas_call(
        _conv_kernel, grid=(batch, nb),
        in_specs=[cur(a_col), cur(a_col + 1), prv(a_col), prv(a_col + 1),
                  pl.BlockSpec((CONV_WIDTH, CONV_CH), lambda b, i: (0, 0)), vec, vec, vec],
        out_specs=pl.BlockSpec((CONV_BLK, CONV_CH), lambda b, i: (b * nb + i, 0)),
        out_shape=jax.ShapeDtypeStruct((n, CONV_CH), bf16),
        scratch_shapes=[pltpu.VMEM((CONV_HALO + CONV_BLK, CONV_CH), f32),
                        pltpu.VMEM((7, CONV_HALO + CONV_BLK, CONV_CH), f32)],
        compiler_params=_cparams("parallel", "parallel"), name="conv_module",
    )(qkvg, qkvg, qkvg, qkvg, w_dw, b_dw.reshape(1, -1), ln_g.reshape(1, -1), ln_b.reshape(1, -1))


def _mla_attn_kernel(q_ref, k_ref, v_ref, o_ref, acc_ref, sa_ref, sb_ref):
    qi = pl.program_id(1)
    t = C_BLK
    pairs = C_HEADS // 2
    row = lax.broadcasted_iota(i32, (2 * t, t), 0)
    col = lax.broadcasted_iota(i32, (2 * t, t), 1)
    diag_ok = (col // CHUNK) <= ((row % t) // CHUNK)
    lane = lax.broadcasted_iota(i32, (t, LANES), 1)
    ones = jnp.ones((t, LANES), bf16)
    subs = C_STEP // t
    both = tuple(range(subs))
    acc_ref[...] = jnp.zeros(acc_ref.shape, f32)

    def scores(kb, s_ref, sub_list):
        k0 = pl.multiple_of(kb * t, t)
        for j in range(pairs):
            kblk = k_ref[pl.ds(k0, t), j * C_SLOT:(j + 1) * C_SLOT]
            for sub in sub_list:
                rows = slice(sub * t, (sub + 1) * t)
                qs = jnp.concatenate([q_ref[rows, (2 * j) * C_SLOT:(2 * j + 1) * C_SLOT],
                                      q_ref[rows, (2 * j + 1) * C_SLOT:(2 * j + 2) * C_SLOT]], axis=0)
                s_ref[sub * pairs + j] = lax.dot_general(qs, kblk, (((1,), (1,)), ((), ())),
                                                         preferred_element_type=f32)

    def absorb(kb, s_ref, ms, plan):
        k0 = pl.multiple_of(kb * t, t)
        ms = list(ms)
        for j in range(pairs):
            vext = jnp.concatenate([v_ref[pl.ds(k0, t), j * LANES:(j + 1) * LANES], ones], axis=1)
            for sub, masked in plan:
                c = sub * pairs + j
                s = s_ref[c]
                if masked:
                    s = jnp.where(diag_ok, s, NEG_INF)
                m_new = jnp.maximum(ms[c], jnp.max(s, axis=-1, keepdims=True))
                a = jnp.exp2(ms[c] - m_new)
                p = jnp.exp2((s - m_new).astype(bf16))
                acc_ref[c] = a * acc_ref[c] + jnp.dot(p, vext, preferred_element_type=f32)
                ms[c] = m_new
        return tuple(ms)

    def two_blocks(i, ms):
        kb = subs * i
        scores(kb + 1, sb_ref, both)
        ms = absorb(kb, sa_ref, ms, [(0, False), (1, False)])
        scores(kb + 2, sa_ref, both)
        return absorb(kb + 1, sb_ref, ms, [(0, False), (1, False)])

    init = tuple(jnp.full((2 * t, 1), NEG_INF, f32) for _ in range(subs * pairs))
    scores(0, sa_ref, both)
    ms = lax.fori_loop(0, qi, two_blocks, init)
    scores(subs * qi + 1, sb_ref, (1,))
    ms = absorb(subs * qi, sa_ref, ms, [(0, True), (1, False)])
    absorb(subs * qi + 1, sb_ref, ms, [(1, True)])
    for sub in range(subs):
        for j in range(pairs):
            acc = acc_ref[sub * pairs + j]
            o = acc[:, :LANES] / acc[:, LANES:]
            o_ref[sub * t:(sub + 1) * t, j * LANES:(j + 1) * LANES] = (
                jnp.where(lane < C_V, o[:t], o[t:]).astype(bf16))


def mla_attn(qc, kc, vc, batch, seq):
    n = batch * seq
    nq = seq // C_STEP
    return pl.pallas_call(
        _mla_attn_kernel, grid=(batch, nq),
        in_specs=[pl.BlockSpec((C_STEP, qc.shape[1]), lambda b, i: (b * nq + i, 0)),
                  pl.BlockSpec((seq, kc.shape[1]), lambda b, i: (b, 0)),
                  pl.BlockSpec((seq, vc.shape[1]), lambda b, i: (b, 0))],
        out_specs=pl.BlockSpec((C_STEP, C_HEADS * C_V), lambda b, i: (b * nq + i, 0)),
        out_shape=jax.ShapeDtypeStruct((n, C_HEADS * C_V), bf16),
        scratch_shapes=[pltpu.VMEM((C_STEP // C_BLK * (C_HEADS // 2), 2 * C_BLK, 2 * LANES), f32),
                        pltpu.VMEM((C_STEP // C_BLK * (C_HEADS // 2), 2 * C_BLK, C_BLK), f32),
                        pltpu.VMEM((C_STEP // C_BLK * (C_HEADS // 2), 2 * C_BLK, C_BLK), f32)],
        compiler_params=_cparams("parallel", "parallel"), name="mla_attn",
    )(qc, kc, vc)


def _mla_pair_kernel(q_ref, k_ref, v_ref, o_ref):
    t = C_BLK
    seq = k_ref.shape[0]
    row = lax.broadcasted_iota(i32, (t, t), 0)
    col = lax.broadcasted_iota(i32, (t, t), 1)
    diag_ok = (col // CHUNK) <= (row // CHUNK)
    lane = lax.broadcasted_iota(i32, (t, LANES), 1)
    nt = lambda a, b: lax.dot_general(a, b, (((1,), (1,)), ((), ())), preferred_element_type=f32)
    for r in range(seq // t):
        lo = r * t
        vext = jnp.concatenate([v_ref[0:lo + t, :], jnp.ones((lo + t, LANES), bf16)], axis=1)
        outs = []
        for hh in range(2):
            q = q_ref[lo:lo + t, hh * C_SLOT:(hh + 1) * C_SLOT]
            s_diag = jnp.where(diag_ok, nt(q, k_ref[lo:lo + t, :]), NEG_INF)
            m = jnp.max(s_diag, axis=-1, keepdims=True)
            if r > 0:
                s_low = nt(q, k_ref[0:lo, :])
                m = jnp.maximum(m, jnp.max(s_low, axis=-1, keepdims=True))
                p = jnp.concatenate([jnp.exp2((s_low - m).astype(bf16)), jnp.exp2((s_diag - m).astype(bf16))],
                                    axis=1)
            else:
                p = jnp.exp2((s_diag - m).astype(bf16))
            o = jnp.dot(p, vext, preferred_element_type=f32)
            outs.append(o[:, :LANES] / o[:, LANES:])
        o_ref[lo:lo + t, :] = jnp.where(lane < C_V, outs[0], outs[1]).astype(bf16)


def mla_attn_pairs(qc, kc, vc, batch, seq):
    n = batch * seq
    pairs = C_HEADS // 2
    return pl.pallas_call(
        _mla_pair_kernel, grid=(batch, pairs),
        in_specs=[pl.BlockSpec((seq, 2 * C_SLOT), lambda b, j: (b, j)),
                  pl.BlockSpec((seq, C_SLOT), lambda b, j: (b, j)),
                  pl.BlockSpec((seq, LANES), lambda b, j: (b, j))],
        out_specs=pl.BlockSpec((seq, LANES), lambda b, j: (b, j)),
        out_shape=jax.ShapeDtypeStruct((n, C_HEADS * C_V), bf16),
        compiler_params=_cparams("parallel", "parallel"), name="mla_attn",
    )(qc, kc, vc)


def _mix_kernel(ya_ref, cb_ref, yc_ref, hb_ref, hf_ref, woa_ref, wpw_ref, woc_ref, wg_ref, bg_ref, wout_ref,
                g_ref, b_ref, of_ref, lo_ref, hi_ref):
    d = D_MODEL
    for r0 in range(0, MIX_TILE, SUB_ROWS):
        rows = slice(r0, r0 + SUB_ROWS)
        gates = _sigmoid(jnp.dot(hb_ref[rows, :], wg_ref[...], preferred_element_type=f32) + bg_ref[...])
        mix = (gates[:, 0:d] * jnp.dot(ya_ref[rows, :], woa_ref[...], preferred_element_type=f32)
               + gates[:, d:2 * d] * jnp.dot(cb_ref[rows, :], wpw_ref[...], preferred_element_type=f32)
               + gates[:, 2 * d:3 * d] * jnp.dot(yc_ref[rows, :], woc_ref[...], preferred_element_type=f32))
        z = DN_ALPHA * hf_ref[rows, :] + jnp.dot(mix.astype(bf16), wout_ref[...], preferred_element_type=f32)
        y = _layer_norm(z, g_ref[...], b_ref[...])
        of_ref[rows, :] = y
        lo, hi = _pack_row(y)
        lo_ref[rows, :] = lo
        hi_ref[rows, :] = hi


def mix_layer(ya, cb, yc, hb, hf, woa, wpw, woc, wg, bg, wout, g, b):
    n, d = hf.shape
    t = MIX_TILE
    full = lambda a: pl.BlockSpec(a.shape, lambda i: (0,) * a.ndim)
    rowb = lambda c: pl.BlockSpec((t, c), lambda i: (i, 0))
    return pl.pallas_call(
        _mix_kernel, grid=(n // t,),
        in_specs=[rowb(ya.shape[1]), rowb(cb.shape[1]), rowb(yc.shape[1]), rowb(d), rowb(d),
                  full(woa), full(wpw), full(woc), full(wg), full(bg), full(wout), full(g), full(b)],
        out_specs=[rowb(d), rowb(QUART), rowb(QUART)],
        out_shape=[jax.ShapeDtypeStruct((n, d), f32), jax.ShapeDtypeStruct((n, QUART), i32),
                   jax.ShapeDtypeStruct((n, QUART), i32)],
        compiler_params=_cparams("parallel"), name="mix_layer",
    )(ya, cb, yc, hb, hf, woa, wpw, woc, wg, bg, wout, g, b)


def _router_kernel(h_ref, wr_ref, br_ref, idx_ref, gate_ref, rank_ref, cnt_ref, base):
    i = pl.program_id(0)
    t = ROUTER_TILE
    e = N_EXPERTS

    @pl.when(i == 0)
    def _():
        base[...] = jnp.zeros_like(base)

    def split(x):
        hi = x.astype(bf16)
        return hi, (x - hi.astype(f32)).astype(bf16)

    nt = lambda a, b: lax.dot_general(a, b, (((1,), (1,)), ((), ())), preferred_element_type=f32)
    w_hi, w_lo = split(wr_ref[...])
    h_hi, h_lo = split(h_ref[...])
    logits = nt(w_hi, h_hi) + nt(w_lo, h_hi) + nt(w_hi, h_lo) + br_ref[...]
    row = lax.broadcasted_iota(i32, (e, t), 0).astype(f32)
    vals, hots = [], []
    cur = logits
    for k in range(TOP_K):
        m = jnp.max(cur, axis=0, keepdims=True)
        first = jnp.min(jnp.where(cur == m, row, float(e)), axis=0, keepdims=True)
        hot = row == first
        cur = jnp.where(hot, -jnp.inf, cur)
        vals.append(m)
        hots.append(hot)
        idx_ref[k:k + 1, :] = first.astype(i32)
    ex = [jnp.exp(v - vals[0]) for v in vals]
    den = ex[0] + ex[1] + ex[2] + ex[3]
    for k in range(TOP_K):
        gate_ref[k:k + 1, :] = ex[k] / den
    onehot = jnp.concatenate([jnp.where(h, 1.0, 0.0) for h in hots], axis=0)
    r = lax.broadcasted_iota(i32, (t, t), 0)
    c = lax.broadcasted_iota(i32, (t, t), 1)
    upper = jnp.where(r <= c, 1.0, 0.0).astype(bf16)
    prefix = jnp.dot(onehot.astype(bf16), upper, preferred_element_type=f32)
    counts = jnp.sum(onehot, axis=1, keepdims=True)
    offset = base[:, 0:1]
    for k in range(TOP_K):
        sel = jnp.where(hots[k], prefix[k * e:(k + 1) * e, :] - 1.0 + offset, 0.0)
        rank_ref[k:k + 1, :] = jnp.sum(sel, axis=0, keepdims=True).astype(i32)
        offset = offset + counts[k * e:(k + 1) * e, :]
    base[...] = jnp.broadcast_to(offset, base.shape)
    cnt_ref[...] = base[...]


def router(hf, w_rt, b_r):
    n, d = hf.shape
    t = ROUTER_TILE
    tok = pl.BlockSpec((TOP_K, t), lambda i: (0, i))
    return pl.pallas_call(
        _router_kernel, grid=(n // t,),
        in_specs=[pl.BlockSpec((t, d), lambda i: (i, 0)), pl.BlockSpec((N_EXPERTS, d), lambda i: (0, 0)),
                  pl.BlockSpec((N_EXPERTS, 1), lambda i: (0, 0))],
        out_specs=[tok, tok, tok, pl.BlockSpec((N_EXPERTS, LANES), lambda i: (0, 0))],
        out_shape=[jax.ShapeDtypeStruct((TOP_K, n), i32), jax.ShapeDtypeStruct((TOP_K, n), f32),
                   jax.ShapeDtypeStruct((TOP_K, n), i32), jax.ShapeDtypeStruct((N_EXPERTS, LANES), f32)],
        scratch_shapes=[pltpu.VMEM((N_EXPERTS, LANES), f32)],
        compiler_params=_cparams("arbitrary"), name="router",
    )(hf, w_rt, b_r)


def _dest_kernel(idx_ref, rank_ref, start_ref, dest_ref):
    t = idx_ref.shape[1]
    row = lax.broadcasted_iota(i32, (N_EXPERTS, t), 0)
    for k in range(TOP_K):
        hot = row == idx_ref[k:k + 1, :]
        off = jnp.sum(jnp.where(hot, start_ref[...], 0.0), axis=0, keepdims=True)
        dest_ref[k:k + 1, :] = rank_ref[k:k + 1, :] + off.astype(i32)


def dest_rows(idx, rank, pad_start):
    n = idx.shape[1]
    t = ROUTER_TILE
    tok = pl.BlockSpec((TOP_K, t), lambda i: (0, i))
    return pl.pallas_call(
        _dest_kernel, grid=(n // t,),
        in_specs=[tok, tok, pl.BlockSpec((N_EXPERTS, 1), lambda i: (0, 0))],
        out_specs=tok, out_shape=jax.ShapeDtypeStruct((TOP_K, n), i32),
        compiler_params=_cparams("parallel"), name="dest_rows",
    )(idx, rank, pad_start)


def _sc_mesh():
    return plsc.VectorSubcoreMesh(core_axis_name="c", subcore_axis_name="s")


def sc_scatter_rows(x, dest, n_rows):
    n, d = x.shape
    kk = dest.shape[0]

    @functools.partial(pl.kernel, out_type=jax.ShapeDtypeStruct((n_rows, d), x.dtype), mesh=_sc_mesh(),
                       scratch_types=[])
    def k(x_hbm, i_hbm, o_hbm):
        def body(x_vmem, i_vmem):
            for j in range(kk):
                pltpu.sync_copy(x_vmem, o_hbm.at[i_vmem.at[j]])

        pltpu.emit_pipeline(
            body, grid=(n // SC_WINDOW,),
            in_specs=[pl.BlockSpec((SC_WINDOW, d), lambda i: (i, 0)),
                      pl.BlockSpec((kk, SC_WINDOW), lambda i: (0, i))],
            out_specs=[], core_axis_name=("c", "s"), dimension_semantics=(pltpu.PARALLEL,),
        )(x_hbm, i_hbm)

    return k(x, dest)


def sc_gather_rows(table, idx):
    m = idx.shape[1]
    d = table.shape[1]

    @functools.partial(pl.kernel, out_type=jax.ShapeDtypeStruct((m, d), table.dtype), mesh=_sc_mesh(),
                       scratch_types=[])
    def k(t_hbm, i_hbm, o_hbm):
        def body(i_vmem, o_vmem):
            pltpu.sync_copy(t_hbm.at[i_vmem.at[0]], o_vmem)

        pltpu.emit_pipeline(
            body, grid=(m // SC_WINDOW,),
            in_specs=[pl.BlockSpec((1, SC_WINDOW), lambda i: (0, i))],
            out_specs=[pl.BlockSpec((SC_WINDOW, d), lambda i: (i, 0))],
            core_axis_name=("c", "s"), dimension_semantics=(pltpu.PARALLEL,),
        )(i_hbm, o_hbm)

    return k(table, idx)


def _expert_kernel(be_ref, nu_ref, xlo_ref, xhi_ref, w1_ref, b1_ref, w2_ref, b2_ref, ylo_ref, yhi_ref, w1b, w2b):
    i = pl.program_id(0)
    new_expert = jnp.logical_or(i == 0, be_ref[i] != be_ref[jnp.maximum(i - 1, 0)])

    @pl.when(jnp.logical_and(i < nu_ref[0], new_expert))
    def _():
        w1b[...] = w1_ref[0, 0].astype(bf16)
        w2b[...] = w2_ref[0, 0].astype(bf16)

    @pl.when(i < nu_ref[0])
    def _():
        for r0 in range(0, MOE_BLK, SUB_ROWS):
            rows = slice(r0, r0 + SUB_ROWS)
            lo = xlo_ref[rows, :]
            hi = xhi_ref[rows, :]
            x = jnp.concatenate([_unpack_lo(lo).astype(bf16), _unpack_lo(hi).astype(bf16),
                                 _unpack_hi(lo).astype(bf16), _unpack_hi(hi).astype(bf16)], axis=1)
            u = jnp.dot(x, w1b[...], preferred_element_type=f32) + b1_ref[0, 0]
            glu = jnp.minimum(u[:, :D_EXPERT], SWIGLU_LIMIT)
            lin = jnp.clip(u[:, D_EXPERT:], -SWIGLU_LIMIT, SWIGLU_LIMIT)
            act = (glu * _sigmoid(SWIGLU_ALPHA * glu) * (lin + 1.0)).astype(bf16)
            y = jnp.dot(act, w2b[...], preferred_element_type=f32) + b2_ref[0, 0]
            ylo, yhi = _pack_row(y)
            ylo_ref[rows, :] = ylo
            yhi_ref[rows, :] = yhi


def expert_ffn(layer, block_expert, n_used, xs_lo, xs_hi, w1, b1, w2, b2):
    n_rows = xs_lo.shape[0]
    nb = n_rows // MOE_BLK
    rows = pl.BlockSpec((MOE_BLK, QUART), lambda i, be, nu: (jnp.minimum(i, nu[0] - 1), 0))
    wsel = lambda shape: pl.BlockSpec((1, 1) + shape, lambda i, be, nu: (layer, be[i], 0, 0))
    out = jax.ShapeDtypeStruct((n_rows, QUART), i32)
    return pl.pallas_call(
        _expert_kernel,
        grid_spec=pltpu.PrefetchScalarGridSpec(
            num_scalar_prefetch=2, grid=(nb,),
            in_specs=[rows, rows, wsel((D_MODEL, 2 * D_EXPERT)), wsel((1, 2 * D_EXPERT)),
                      wsel((D_EXPERT, D_MODEL)), wsel((1, D_MODEL))],
            out_specs=[rows, rows],
            scratch_shapes=[pltpu.VMEM((D_MODEL, 2 * D_EXPERT), bf16), pltpu.VMEM((D_EXPERT, D_MODEL), bf16)]),
        out_shape=[out, out],
        compiler_params=_cparams("arbitrary"), name="expert_ffn",
    )(block_expert, n_used, xs_lo, xs_hi, w1, b1, w2, b2)


def _combine_kernel(ylo_ref, yhi_ref, gt_ref, hf_ref, g_ref, b_ref, of_ref, ob_ref):
    gt = gt_ref[...]
    parts = [None] * 4
    for k in range(TOP_K):
        gk = gt[:, k:k + 1]
        lo = ylo_ref[k]
        hi = yhi_ref[k]
        vals = (_unpack_lo(lo), _unpack_lo(hi), _unpack_hi(lo), _unpack_hi(hi))
        for p in range(4):
            parts[p] = gk * vals[p] if parts[p] is None else parts[p] + gk * vals[p]
    z = DN_ALPHA * hf_ref[...] + jnp.concatenate(parts, axis=1)
    y = _layer_norm(z, g_ref[...], b_ref[...])
    of_ref[...] = y
    ob_ref[...] = y.astype(bf16)


def combine(yk_lo, yk_hi, gates_t, hf, g, b):
    n, d = hf.shape
    t = MIX_TILE
    ysp = pl.BlockSpec((TOP_K, t, QUART), lambda i: (0, i, 0))
    rowb = lambda c: pl.BlockSpec((t, c), lambda i: (i, 0))
    vec = pl.BlockSpec((1, d), lambda i: (0, 0))
    return pl.pallas_call(
        _combine_kernel, grid=(n // t,),
        in_specs=[ysp, ysp, rowb(TOP_K), rowb(d), vec, vec],
        out_specs=[rowb(d), rowb(d)],
        out_shape=[jax.ShapeDtypeStruct((n, d), f32), jax.ShapeDtypeStruct((n, d), bf16)],
        compiler_params=_cparams("parallel"), name="combine",
    )(yk_lo, yk_hi, gates_t, hf, g, b)


def _rope_tables(positions):
    inv = ROPE_THETA ** (-jnp.arange(0, C_ROPE, 2, dtype=f32) / C_ROPE)
    ang = positions.reshape(-1).astype(f32)[:, None] * inv
    cos, sin = jnp.cos(ang), jnp.sin(ang)
    pad = jnp.zeros((ang.shape[0], LANES - C_ROPE), f32)
    return jnp.concatenate([cos, cos, pad], axis=1), jnp.concatenate([-sin, sin, pad], axis=1)


def _attn_a_bias(rel_tables):
    depth = rel_tables.shape[0]
    r = jnp.arange(A_QBLK)[:, None]
    c = jnp.arange(A_BAND)[None, :]
    rel = jnp.clip(c - A_LEFT - r, -A_MAX_REL, A_MAX_REL) + A_MAX_REL
    own = c - CHUNK * (r // CHUNK)
    valid = (own >= 0) & (own < A_LEFT + CHUNK)
    onehot = (rel[:, :, None] == jnp.arange(2 * A_MAX_REL + 1)[None, None, :]).astype(f32)
    bias = jnp.einsum("rck,lhk->lhrc", onehot, rel_tables.astype(f32), precision=lax.Precision.HIGHEST)
    bias = jnp.where(valid[None, None], bias * LOG2E, NEG_INF)
    return bias.reshape(depth, A_HEADS // 2, 2 * A_QBLK, A_BAND)


def _layout_w_uq(w_uq):
    w = w_uq.reshape(C_Q_RANK, C_HEADS, C_QK)
    out = jnp.zeros((C_Q_RANK, C_HEADS, C_SLOT), w_uq.dtype)
    for h in range(C_HEADS):
        off = (h % 2) * C_NOPE
        out = out.at[:, h, off:off + C_NOPE].set(w[:, h, :C_NOPE])
        out = out.at[:, h, LANES:LANES + C_ROPE].set(w[:, h, C_NOPE:])
    return out.reshape(C_Q_RANK, C_HEADS * C_SLOT)


def _layout_w_ukv(w_ukv):
    w = w_ukv.reshape(C_KV_RANK, C_HEADS, C_NOPE + C_V)
    return jnp.concatenate([w[:, :, :C_NOPE].reshape(C_KV_RANK, -1), w[:, :, C_NOPE:].reshape(C_KV_RANK, -1)], axis=1)


def _moe(layer, hf, lo, hi, w_rt, b_r, w1, b1, w2, b2, g2, bb2):
    n = hf.shape[0]
    n_rows = n * TOP_K + N_EXPERTS * MOE_BLK
    nb = n_rows // MOE_BLK
    idx, gates, rank, cnt = router(hf, w_rt, b_r)
    counts = cnt[:, 0].astype(i32)
    padded = (counts + MOE_BLK - 1) // MOE_BLK * MOE_BLK
    pad_end = jnp.cumsum(padded)
    pad_start = (pad_end - padded).astype(f32).reshape(N_EXPERTS, 1)
    block_start = jnp.arange(nb, dtype=i32) * MOE_BLK
    block_expert = jnp.minimum(jnp.sum((pad_end[None, :] <= block_start[:, None]).astype(i32), axis=1),
                               N_EXPERTS - 1)
    n_used = (pad_end[-1:] // MOE_BLK).astype(i32)
    dest = dest_rows(idx, rank, pad_start)
    xs_lo = sc_scatter_rows(lo, dest, n_rows)
    xs_hi = sc_scatter_rows(hi, dest, n_rows)
    ys_lo, ys_hi = expert_ffn(layer, block_expert, n_used, xs_lo, xs_hi, w1, b1, w2, b2)
    flat = dest.reshape(1, TOP_K * n)
    yk_lo = sc_gather_rows(ys_lo, flat)
    yk_hi = sc_gather_rows(ys_hi, flat)
    return combine(yk_lo.reshape(TOP_K, n, QUART), yk_hi.reshape(TOP_K, n, QUART), gates.T, hf, g2, bb2)


def kernel(x, positions, ln_in_g, ln_in_b, w_in, w_gate, b_gate, rel_bias, conv_w, conv_b, conv_ln_g, conv_ln_b, w_pw2, q_norm_g, kv_norm_g, w_uq, w_ukv, w_oa, w_oc, w_out, ln1_g, ln1_b, w_router, b_router, w1, b1, w2, b2, ln2_g, ln2_b):
    batch, seq, d = x.shape
    n = batch * seq
    streams = STREAMS if batch % STREAMS == 0 else 1
    sb = batch // streams
    sn = sb * seq
    cosm, sinm = _rope_tables(positions)
    a_bias = _attn_a_bias(rel_bias)
    x2d = x.reshape(n, d)
    state = [ln_in(x2d, ln_in_g, ln_in_b, p * sn, sn) for p in range(streams)]
    ab = 3 * A_WIDTH + 2 * CONV_CH
    row1 = lambda v: v.reshape(1, -1)
    b1r = b1.reshape(DEPTH, N_EXPERTS, 1, -1)
    b2r = b2.reshape(DEPTH, N_EXPERTS, 1, -1)
    for l in range(DEPTH):
        w_ab = w_in[l, :, :ab].astype(bf16)
        w_cq = w_in[l, :, ab:ab + C_Q_RANK].astype(bf16)
        w_ckv = w_in[l, :, ab + C_Q_RANK:ab + C_Q_RANK + C_KV_RANK].astype(bf16)
        w_kr = jnp.pad(w_in[l, :, ab + C_Q_RANK + C_KV_RANK:], ((0, 0), (0, LANES - C_ROPE))).astype(bf16)
        wuq = _layout_w_uq(w_uq[l]).astype(bf16)
        wukv = _layout_w_ukv(w_ukv[l]).astype(bf16)
        woa, wpw, woc = w_oa[l].astype(bf16), w_pw2[l].astype(bf16), w_oc[l].astype(bf16)
        wg, wout = w_gate[l].astype(bf16), w_out[l].astype(bf16)
        for p in range(streams):
            hf, hb = state[p]
            tok = slice(p * sn, (p + 1) * sn)
            qkvg = proj(hb, w_ab)
            qc, kc, vc = mla_prep(hb, w_cq, w_ckv, w_kr, row1(q_norm_g[l]), row1(kv_norm_g[l]), wuq, wukv,
                                  cosm[tok], sinm[tok])
            ya = attn_a(qkvg, a_bias[l], sb, seq)
            cb = conv_module(qkvg, conv_w[l], conv_b[l], conv_ln_g[l], conv_ln_b[l], sb, seq)
            yc = mla_attn_pairs(qc, kc, vc, sb, seq)
            h1f, lo, hi = mix_layer(ya, cb, yc, hb, hf, woa, wpw, woc, wg, row1(b_gate[l]), wout,
                                    row1(ln1_g[l]), row1(ln1_b[l]))
            state[p] = _moe(l, h1f, lo, hi, w_router[l].T, b_router[l].reshape(N_EXPERTS, 1),
                            w1, b1r, w2, b2r, row1(ln2_g[l]), row1(ln2_b[l]))
    return jnp.concatenate([s[0] for s in state], axis=0).reshape(batch, seq, d)
```

```python
import functools

import jax
import jax.numpy as jnp
from jax import lax
from jax.experimental import pallas as pl
from jax.experimental.pallas import tpu as pltpu
from jax.experimental.pallas import tpu_sc as plsc

f32 = jnp.float32
bf16 = jnp.bfloat16
i32 = jnp.int32

D_MODEL = 1024
DEPTH = 4
CHUNK = 64
A_HEADS = 8
A_HEAD_DIM = 64
A_WIDTH = A_HEADS * A_HEAD_DIM
A_LEFT = 8 * CHUNK
A_MAX_REL = 128
CONV_CH = 512
CONV_WIDTH = 31
C_HEADS = 8
C_NOPE = 64
C_ROPE = 32
C_V = 64
C_QK = C_NOPE + C_ROPE
C_Q_RANK = 384
C_KV_RANK = 256
ROPE_THETA = 10000.0
N_EXPERTS = 32
TOP_K = 4
D_EXPERT = 1024
SWIGLU_ALPHA = 1.702
SWIGLU_LIMIT = 7.0
DN_ALPHA = (2 * DEPTH) ** 0.25
LN_EPS = 1e-5
RMS_EPS = 1e-6
NEG_INF = -1e30
LOG2E = 1.4426950408889634

LANES = 128
A_QBLK = 2 * CHUNK
A_STEP = 2 * A_QBLK
A_BAND = A_LEFT + A_QBLK
C_BLK = 256
C_SLOT = 256
CONV_BLK = 256
CONV_HALO = 32
MOE_BLK = 512
ROW_TILE = 512
MIX_TILE = 512
SUB_ROWS = 256
ROUTER_TILE = 512
HALF = D_MODEL // 2
QUART = D_MODEL // 4
SC_WINDOW = 128
VMEM_LIMIT = 56 * 1024 * 1024


def _cparams(*sem):
    return pltpu.CompilerParams(dimension_semantics=tuple(sem), vmem_limit_bytes=VMEM_LIMIT)


def _layer_norm(x, g, b):
    mu = jnp.mean(x, axis=-1, keepdims=True)
    xc = x - mu
    var = jnp.mean(xc * xc, axis=-1, keepdims=True)
    return xc * lax.rsqrt(var + LN_EPS) * g + b


def _sigmoid(x):
    return 1.0 / (1.0 + jnp.exp(-x))


def _pack2(a, b):
    ab = lax.bitcast_convert_type(a.astype(bf16).astype(f32), i32)
    bb = lax.bitcast_convert_type(b.astype(bf16).astype(f32), i32)
    return lax.shift_right_logical(ab, 16) | (bb & jnp.int32(-65536))


def _unpack_lo(w):
    return lax.bitcast_convert_type(lax.shift_left(w, 16), f32)


def _unpack_hi(w):
    return lax.bitcast_convert_type(w & jnp.int32(-65536), f32)


def _pack_row(y):
    return (_pack2(y[:, 0:QUART], y[:, 2 * QUART:3 * QUART]),
            _pack2(y[:, QUART:2 * QUART], y[:, 3 * QUART:4 * QUART]))


def _moe_sum(ylo_ref, yhi_ref, gt_ref, rows):
    gt = gt_ref[rows, :]
    parts = [None] * 4
    for k in range(TOP_K):
        gk = gt[:, k:k + 1]
        lo = ylo_ref[k, rows, :]
        hi = yhi_ref[k, rows, :]
        vals = (_unpack_lo(lo), _unpack_lo(hi), _unpack_hi(lo), _unpack_hi(hi))
        for p in range(4):
            parts[p] = gk * vals[p] if parts[p] is None else parts[p] + gk * vals[p]
    return jnp.concatenate(parts, axis=1)


def _project(y, rows, w_ref, hf_ref, hb_ref, o_ref):
    hf_ref[rows, :] = y
    yb = y.astype(bf16)
    hb_ref[rows, :] = yb
    acc = jnp.dot(yb, w_ref[...], preferred_element_type=f32)
    o_ref[rows, :A_WIDTH] = (acc[:, :A_WIDTH] * (A_HEAD_DIM ** -0.5 * LOG2E)).astype(bf16)
    o_ref[rows, A_WIDTH:] = acc[:, A_WIDTH:].astype(bf16)


def _entry_first_kernel(x_ref, g_ref, b_ref, w_ref, hf_ref, hb_ref, o_ref):
    for r0 in range(0, ROW_TILE, SUB_ROWS):
        rows = slice(r0, r0 + SUB_ROWS)
        _project(_layer_norm(x_ref[rows, :], g_ref[...], b_ref[...]), rows, w_ref, hf_ref, hb_ref, o_ref)


def _entry_moe_kernel(ylo_ref, yhi_ref, gt_ref, h1_ref, g_ref, b_ref, w_ref, hf_ref, hb_ref, o_ref):
    for r0 in range(0, ROW_TILE, SUB_ROWS):
        rows = slice(r0, r0 + SUB_ROWS)
        z = DN_ALPHA * h1_ref[rows, :] + _moe_sum(ylo_ref, yhi_ref, gt_ref, rows)
        _project(_layer_norm(z, g_ref[...], b_ref[...]), rows, w_ref, hf_ref, hb_ref, o_ref)


def _entry_call(body, n, w, row_inputs, row_specs, g, b, name):
    d = D_MODEL
    c = w.shape[1]
    t = ROW_TILE
    vec = pl.BlockSpec((1, d), lambda i: (0, 0))
    rowb = lambda cc: pl.BlockSpec((t, cc), lambda i: (i, 0))
    return pl.pallas_call(
        body, grid=(n // t,),
        in_specs=row_specs + [vec, vec, pl.BlockSpec((d, c), lambda i: (0, 0))],
        out_specs=[rowb(d), rowb(d), rowb(c)],
        out_shape=[jax.ShapeDtypeStruct((n, d), f32), jax.ShapeDtypeStruct((n, d), bf16),
                   jax.ShapeDtypeStruct((n, c), bf16)],
        compiler_params=_cparams("parallel"), name=name,
    )(*row_inputs, g.reshape(1, d), b.reshape(1, d), w)


def entry_first(x2d, g, b, w):
    n, d = x2d.shape
    return _entry_call(_entry_first_kernel, n, w, [x2d], [pl.BlockSpec((ROW_TILE, d), lambda i: (i, 0))], g, b,
                       "entry_first")


def entry_moe(yk_lo, yk_hi, gates_t, h1f, g, b, w):
    n, d = h1f.shape
    t = ROW_TILE
    ysp = pl.BlockSpec((TOP_K, t, QUART), lambda i: (0, i, 0))
    specs = [ysp, ysp, pl.BlockSpec((t, TOP_K), lambda i: (i, 0)), pl.BlockSpec((t, d), lambda i: (i, 0))]
    return _entry_call(_entry_moe_kernel, n, w, [yk_lo, yk_hi, gates_t, h1f], specs, g, b, "entry_moe")


def _mla_prep_rows(rows, hb_ref, wcq_ref, wckv_ref, wkr_ref, gq_ref, gkv_ref, wuq_ref, wukv_ref, cos_ref, sin_ref,
                   qc_ref, kc_ref, vc_ref):
    hb = hb_ref[rows, :]
    cosm = cos_ref[rows, :]
    sinm = sin_ref[rows, :]
    lane = lax.broadcasted_iota(i32, cosm.shape, 1)
    half = C_ROPE // 2
    scale = C_QK ** -0.5 * LOG2E

    def rope(x):
        swapped = jnp.where(lane < half, pltpu.roll(x, LANES - half, 1), pltpu.roll(x, half, 1))
        return x * cosm + swapped * sinm

    def rms(x, g):
        return (x * lax.rsqrt(jnp.mean(x * x, axis=-1, keepdims=True) + RMS_EPS) * g).astype(bf16)

    cqn = rms(jnp.dot(hb, wcq_ref[...], preferred_element_type=f32), gq_ref[...])
    q = jnp.dot(cqn, wuq_ref[...], preferred_element_type=f32)
    for h in range(C_HEADS):
        lo = h * C_SLOT
        qc_ref[rows, lo:lo + LANES] = (q[:, lo:lo + LANES] * scale).astype(bf16)
        qc_ref[rows, lo + LANES:lo + C_SLOT] = (rope(q[:, lo + LANES:lo + C_SLOT]) * scale).astype(bf16)
    ckvn = rms(jnp.dot(hb, wckv_ref[...], preferred_element_type=f32), gkv_ref[...])
    kv = jnp.dot(ckvn, wukv_ref[...], preferred_element_type=f32)
    kr = rope(jnp.dot(hb, wkr_ref[...], preferred_element_type=f32)).astype(bf16)
    for j in range(C_HEADS // 2):
        lo = j * C_SLOT
        kc_ref[rows, lo:lo + LANES] = kv[:, j * LANES:(j + 1) * LANES].astype(bf16)
        kc_ref[rows, lo + LANES:lo + C_SLOT] = kr
    vc_ref[rows, :] = kv[:, C_HEADS * C_NOPE:].astype(bf16)


def _mla_prep_kernel(*refs):
    _mla_prep_rows(slice(0, ROW_TILE), *refs)


def mla_prep(hb, wcq, wckv, wkr, gq, gkv, wuq, wukv, cosm, sinm):
    n, d = hb.shape
    t = ROW_TILE
    full = lambda a: pl.BlockSpec(a.shape, lambda i: (0,) * a.ndim)
    rowb = lambda c: pl.BlockSpec((t, c), lambda i: (i, 0))
    qw, kw, vw = C_HEADS * C_SLOT, (C_HEADS // 2) * C_SLOT, C_HEADS * C_V
    return pl.pallas_call(
        _mla_prep_kernel, grid=(n // t,),
        in_specs=[rowb(d), full(wcq), full(wckv), full(wkr), full(gq), full(gkv), full(wuq), full(wukv),
                  rowb(LANES), rowb(LANES)],
        out_specs=[rowb(qw), rowb(kw), rowb(vw)],
        out_shape=[jax.ShapeDtypeStruct((n, qw), bf16), jax.ShapeDtypeStruct((n, kw), bf16),
                   jax.ShapeDtypeStruct((n, vw), bf16)],
        compiler_params=_cparams("parallel"), name="mla_prep",
    )(hb, wcq, wckv, wkr, gq, gkv, wuq, wukv, cosm, sinm)


def _conv_rows(base, hs, sh, w_ref, cb_ref, lg_ref, lb_ref, o_ref):
    t = CONV_BLK
    first = CONV_HALO - CONV_WIDTH + 1
    a_of = [[a for a in range(CONV_HALO // 8 + 1) if first <= 8 * a + b <= CONV_HALO] for b in range(8)]
    for b in range(1, 8):
        length = 8 * max(a_of[b]) + t
        sh[b - 1, 0:length, :] = hs[base + b:base + b + length, :]
    rows = 64
    for r0 in range(0, t, rows):
        acc = jnp.zeros((rows, CONV_CH), f32) + cb_ref[...]
        for b in range(8):
            for a in a_of[b]:
                tap = 8 * a + b - first
                lo = r0 + 8 * a
                src = hs[base + lo:base + lo + rows, :] if b == 0 else sh[b - 1, lo:lo + rows, :]
                acc = acc + src * w_ref[tap:tap + 1, :]
        y = _layer_norm(acc, lg_ref[...], lb_ref[...])
        o_ref[base + r0:base + r0 + rows, :] = (y * _sigmoid(y)).astype(bf16)


def _conv_kernel(a_ref, g_ref, ap_ref, gp_ref, w_ref, cb_ref, lg_ref, lb_ref, o_ref, hs, sh):
    i = pl.program_id(1)
    prev = ap_ref[...].astype(f32) * _sigmoid(gp_ref[...].astype(f32))
    hs[0:CONV_HALO, :] = jnp.where(i > 0, prev, 0.0)
    hs[CONV_HALO:CONV_HALO + ROW_TILE, :] = a_ref[...].astype(f32) * _sigmoid(g_ref[...].astype(f32))
    for base in range(0, ROW_TILE, CONV_BLK):
        _conv_rows(base, hs, sh, w_ref, cb_ref, lg_ref, lb_ref, o_ref)


def conv_module(qkvg, w_dw, b_dw, ln_g, ln_b, batch, seq):
    n = batch * seq
    t = ROW_TILE
    nb = seq // t
    halo_per_tile = t // CONV_HALO
    a_col = 3 * A_WIDTH // CONV_CH
    cur = lambda col: pl.BlockSpec((t, CONV_CH), lambda b, i: (b * nb + i, col))
    halo = lambda col: pl.BlockSpec(
        (CONV_HALO, CONV_CH), lambda b, i: (jnp.maximum((b * nb + i) * halo_per_tile - 1, 0), col))
    vec = pl.BlockSpec((1, CONV_CH), lambda b, i: (0, 0))
    return pl.pallas_call(
        _conv_kernel, grid=(batch, nb),
        in_specs=[cur(a_col), cur(a_col + 1), halo(a_col), halo(a_col + 1),
                  pl.BlockSpec((CONV_WIDTH, CONV_CH), lambda b, i: (0, 0)), vec, vec, vec],
        out_specs=pl.BlockSpec((t, CONV_CH), lambda b, i: (b * nb + i, 0)),
        out_shape=jax.ShapeDtypeStruct((n, CONV_CH), bf16),
        scratch_shapes=[pltpu.VMEM((CONV_HALO + t, CONV_CH), f32),
                        pltpu.VMEM((7, CONV_HALO + CONV_BLK, CONV_CH), f32)],
        compiler_params=_cparams("parallel", "parallel"), name="conv_module",
    )(qkvg, qkvg, qkvg, qkvg, w_dw, b_dw.reshape(1, -1), ln_g.reshape(1, -1), ln_b.reshape(1, -1))


def _attn_a_kernel(q_ref, k_ref, v_ref, bias_ref, o_ref, kpad, vpad):
    qi = pl.program_id(1)
    seq = k_ref.shape[0]

    @pl.when(qi == 0)
    def _():
        kpad[0:A_LEFT, :] = jnp.zeros((A_LEFT, A_WIDTH), bf16)
        vpad[0:A_LEFT, :] = jnp.zeros((A_LEFT, A_WIDTH), bf16)
        kpad[A_LEFT:A_LEFT + seq, :] = k_ref[...]
        vpad[A_LEFT:A_LEFT + seq, :] = v_ref[...]

    col = lax.broadcasted_iota(i32, (2 * A_QBLK, A_BAND), 1)
    lane = lax.broadcasted_iota(i32, (A_QBLK, LANES), 1)
    ones = jnp.ones((A_BAND, LANES), bf16)
    for sub in range(A_STEP // A_QBLK):
        start = pl.multiple_of(qi * A_STEP + sub * A_QBLK, A_QBLK)
        before_start = jnp.where(col + start >= A_LEFT, 0.0, NEG_INF)
        rows = slice(sub * A_QBLK, (sub + 1) * A_QBLK)
        for j in range(A_HEADS // 2):
            cs = slice(j * LANES, (j + 1) * LANES)
            qp = q_ref[rows, cs].astype(f32)
            qs = jnp.concatenate([jnp.where(lane < A_HEAD_DIM, qp, 0.0), jnp.where(lane >= A_HEAD_DIM, qp, 0.0)],
                                 axis=0).astype(bf16)
            kb = kpad[pl.ds(start, A_BAND), cs]
            vb = jnp.concatenate([vpad[pl.ds(start, A_BAND), cs], ones], axis=1)
            s = lax.dot_general(qs, kb, (((1,), (1,)), ((), ())), preferred_element_type=f32)
            s = s + bias_ref[j] + before_start
            p = jnp.exp2((s - jnp.max(s, axis=-1, keepdims=True)).astype(bf16))
            o = jnp.dot(p, vb, preferred_element_type=f32)
            o = o[:, :LANES] / o[:, LANES:]
            o_ref[rows, cs] = jnp.where(lane < A_HEAD_DIM, o[:A_QBLK], o[A_QBLK:]).astype(bf16)


def attn_a(qkvg, bias, batch, seq):
    n = batch * seq
    nq = seq // A_STEP
    return pl.pallas_call(
        _attn_a_kernel, grid=(batch, nq),
        in_specs=[pl.BlockSpec((A_STEP, A_WIDTH), lambda b, i: (b * nq + i, 0)),
                  pl.BlockSpec((seq, A_WIDTH), lambda b, i: (b, 1)),
                  pl.BlockSpec((seq, A_WIDTH), lambda b, i: (b, 2)),
                  pl.BlockSpec(bias.shape, lambda b, i: (0, 0, 0))],
        out_specs=pl.BlockSpec((A_STEP, A_WIDTH), lambda b, i: (b * nq + i, 0)),
        out_shape=jax.ShapeDtypeStruct((n, A_WIDTH), bf16),
        scratch_shapes=[pltpu.VMEM((A_LEFT + seq, A_WIDTH), bf16), pltpu.VMEM((A_LEFT + seq, A_WIDTH), bf16)],
        compiler_params=_cparams("parallel", "arbitrary"), name="attn_a",
    )(qkvg, qkvg, qkvg, bias)


def _mla_pair_kernel(q_ref, k_ref, v_ref, o_ref):
    t = C_BLK
    seq = k_ref.shape[0]
    row = lax.broadcasted_iota(i32, (t, t), 0)
    col = lax.broadcasted_iota(i32, (t, t), 1)
    diag_ok = (col // CHUNK) <= (row // CHUNK)
    lane = lax.broadcasted_iota(i32, (t, LANES), 1)
    nt = lambda a, b: lax.dot_general(a, b, (((1,), (1,)), ((), ())), preferred_element_type=f32)
    for r in range(seq // t):
        lo = r * t
        vext = jnp.concatenate([v_ref[0:lo + t, :], jnp.ones((lo + t, LANES), bf16)], axis=1)
        outs = []
        for hh in range(2):
            q = q_ref[lo:lo + t, hh * C_SLOT:(hh + 1) * C_SLOT]
            s_diag = jnp.where(diag_ok, nt(q, k_ref[lo:lo + t, :]), NEG_INF)
            m = jnp.max(s_diag, axis=-1, keepdims=True)
            if r > 0:
                s_low = nt(q, k_ref[0:lo, :])
                m = jnp.maximum(m, jnp.max(s_low, axis=-1, keepdims=True))
                p = jnp.concatenate([jnp.exp2((s_low - m).astype(bf16)), jnp.exp2((s_diag - m).astype(bf16))],
                                    axis=1)
            else:
                p = jnp.exp2((s_diag - m).astype(bf16))
            o = jnp.dot(p, vext, preferred_element_type=f32)
            outs.append(o[:, :LANES] / o[:, LANES:])
        o_ref[lo:lo + t, :] = jnp.where(lane < C_V, outs[0], outs[1]).astype(bf16)


def mla_attn_pairs(qc, kc, vc, batch, seq):
    n = batch * seq
    pairs = C_HEADS // 2
    return pl.pallas_call(
        _mla_pair_kernel, grid=(batch, pairs),
        in_specs=[pl.BlockSpec((seq, 2 * C_SLOT), lambda b, j: (b, j)),
                  pl.BlockSpec((seq, C_SLOT), lambda b, j: (b, j)),
                  pl.BlockSpec((seq, LANES), lambda b, j: (b, j))],
        out_specs=pl.BlockSpec((seq, LANES), lambda b, j: (b, j)),
        out_shape=jax.ShapeDtypeStruct((n, C_HEADS * C_V), bf16),
        compiler_params=_cparams("parallel", "parallel"), name="mla_attn",
    )(qc, kc, vc)


def _mix_kernel(ya_ref, cb_ref, yc_ref, hb_ref, hf_ref, woa_ref, wpw_ref, woc_ref, wg_ref, bg_ref, wout_ref,
                g_ref, b_ref, of_ref, lo_ref, hi_ref):
    d = D_MODEL
    for r0 in range(0, MIX_TILE, SUB_ROWS):
        rows = slice(r0, r0 + SUB_ROWS)
        gates = _sigmoid(jnp.dot(hb_ref[rows, :], wg_ref[...], preferred_element_type=f32) + bg_ref[...])
        mix = (gates[:, 0:d] * jnp.dot(ya_ref[rows, :], woa_ref[...], preferred_element_type=f32)
               + gates[:, d:2 * d] * jnp.dot(cb_ref[rows, :], wpw_ref[...], preferred_element_type=f32)
               + gates[:, 2 * d:3 * d] * jnp.dot(yc_ref[rows, :], woc_ref[...], preferred_element_type=f32))
        z = DN_ALPHA * hf_ref[rows, :] + jnp.dot(mix.astype(bf16), wout_ref[...], preferred_element_type=f32)
        y = _layer_norm(z, g_ref[...], b_ref[...])
        of_ref[rows, :] = y
        lo, hi = _pack_row(y)
        lo_ref[rows, :] = lo
        hi_ref[rows, :] = hi


def mix_layer(ya, cb, yc, hb, hf, woa, wpw, woc, wg, bg, wout, g, b):
    n, d = hf.shape
    t = MIX_TILE
    full = lambda a: pl.BlockSpec(a.shape, lambda i: (0,) * a.ndim)
    rowb = lambda c: pl.BlockSpec((t, c), lambda i: (i, 0))
    return pl.pallas_call(
        _mix_kernel, grid=(n // t,),
        in_specs=[rowb(ya.shape[1]), rowb(cb.shape[1]), rowb(yc.shape[1]), rowb(d), rowb(d),
                  full(woa), full(wpw), full(woc), full(wg), full(bg), full(wout), full(g), full(b)],
        out_specs=[rowb(d), rowb(QUART), rowb(QUART)],
        out_shape=[jax.ShapeDtypeStruct((n, d), f32), jax.ShapeDtypeStruct((n, QUART), i32),
                   jax.ShapeDtypeStruct((n, QUART), i32)],
        compiler_params=_cparams("parallel"), name="mix_layer",
    )(ya, cb, yc, hb, hf, woa, wpw, woc, wg, bg, wout, g, b)


def _router_kernel(h_ref, wr_ref, br_ref, idx_ref, gate_ref, rank_ref, cnt_ref, base):
    i = pl.program_id(0)
    t = ROUTER_TILE
    e = N_EXPERTS

    @pl.when(i == 0)
    def _():
        base[...] = jnp.zeros_like(base)

    def split(x):
        hi = x.astype(bf16)
        return hi, (x - hi.astype(f32)).astype(bf16)

    nt = lambda a, b: lax.dot_general(a, b, (((1,), (1,)), ((), ())), preferred_element_type=f32)
    w_hi, w_lo = split(wr_ref[...])
    h_hi, h_lo = split(h_ref[...])
    logits = nt(w_hi, h_hi) + nt(w_lo, h_hi) + nt(w_hi, h_lo) + br_ref[...]
    row = lax.broadcasted_iota(i32, (e, t), 0).astype(f32)
    vals, hots = [], []
    cur = logits
    for k in range(TOP_K):
        m = jnp.max(cur, axis=0, keepdims=True)
        first = jnp.min(jnp.where(cur == m, row, float(e)), axis=0, keepdims=True)
        hot = row == first
        cur = jnp.where(hot, -jnp.inf, cur)
        vals.append(m)
        hots.append(hot)
        idx_ref[k:k + 1, :] = first.astype(i32)
    ex = [jnp.exp(v - vals[0]) for v in vals]
    den = ex[0] + ex[1] + ex[2] + ex[3]
    for k in range(TOP_K):
        gate_ref[k:k + 1, :] = ex[k] / den
    onehot = jnp.concatenate([jnp.where(h, 1.0, 0.0) for h in hots], axis=0)
    r = lax.broadcasted_iota(i32, (t, t), 0)
    c = lax.broadcasted_iota(i32, (t, t), 1)
    upper = jnp.where(r <= c, 1.0, 0.0).astype(bf16)
    prefix = jnp.dot(onehot.astype(bf16), upper, preferred_element_type=f32)
    counts = jnp.sum(onehot, axis=1, keepdims=True)
    offset = base[:, 0:1]
    for k in range(TOP_K):
        sel = jnp.where(hots[k], prefix[k * e:(k + 1) * e, :] - 1.0 + offset, 0.0)
        rank_ref[k:k + 1, :] = jnp.sum(sel, axis=0, keepdims=True).astype(i32)
        offset = offset + counts[k * e:(k + 1) * e, :]
    base[...] = jnp.broadcast_to(offset, base.shape)
    cnt_ref[...] = base[...]


def router(hf, w_rt, b_r):
    n, d = hf.shape
    t = ROUTER_TILE
    tok = pl.BlockSpec((TOP_K, t), lambda i: (0, i))
    return pl.pallas_call(
        _router_kernel, grid=(n // t,),
        in_specs=[pl.BlockSpec((t, d), lambda i: (i, 0)), pl.BlockSpec((N_EXPERTS, d), lambda i: (0, 0)),
                  pl.BlockSpec((N_EXPERTS, 1), lambda i: (0, 0))],
        out_specs=[tok, tok, tok, pl.BlockSpec((N_EXPERTS, LANES), lambda i: (0, 0))],
        out_shape=[jax.ShapeDtypeStruct((TOP_K, n), i32), jax.ShapeDtypeStruct((TOP_K, n), f32),
                   jax.ShapeDtypeStruct((TOP_K, n), i32), jax.ShapeDtypeStruct((N_EXPERTS, LANES), f32)],
        scratch_shapes=[pltpu.VMEM((N_EXPERTS, LANES), f32)],
        compiler_params=_cparams("arbitrary"), name="router",
    )(hf, w_rt, b_r)


def _dest_kernel(idx_ref, rank_ref, start_ref, dest_ref):
    t = idx_ref.shape[1]
    row = lax.broadcasted_iota(i32, (N_EXPERTS, t), 0)
    for k in range(TOP_K):
        hot = row == idx_ref[k:k + 1, :]
        off = jnp.sum(jnp.where(hot, start_ref[...], 0.0), axis=0, keepdims=True)
        dest_ref[k:k + 1, :] = rank_ref[k:k + 1, :] + off.astype(i32)


def dest_rows(idx, rank, pad_start):
    n = idx.shape[1]
    t = ROUTER_TILE
    tok = pl.BlockSpec((TOP_K, t), lambda i: (0, i))
    return pl.pallas_call(
        _dest_kernel, grid=(n // t,),
        in_specs=[tok, tok, pl.BlockSpec((N_EXPERTS, 1), lambda i: (0, 0))],
        out_specs=tok, out_shape=jax.ShapeDtypeStruct((TOP_K, n), i32),
        compiler_params=_cparams("parallel"), name="dest_rows",
    )(idx, rank, pad_start)


def _sc_mesh():
    return plsc.VectorSubcoreMesh(core_axis_name="c", subcore_axis_name="s")


def sc_scatter_rows(x, dest, n_rows):
    n, d = x.shape
    kk = dest.shape[0]

    @functools.partial(pl.kernel, out_type=jax.ShapeDtypeStruct((n_rows, d), x.dtype), mesh=_sc_mesh(),
                       scratch_types=[])
    def k(x_hbm, i_hbm, o_hbm):
        def body(x_vmem, i_vmem):
            for j in range(kk):
                pltpu.sync_copy(x_vmem, o_hbm.at[i_vmem.at[j]])

        pltpu.emit_pipeline(
            body, grid=(n // SC_WINDOW,),
            in_specs=[pl.BlockSpec((SC_WINDOW, d), lambda i: (i, 0)),
                      pl.BlockSpec((kk, SC_WINDOW), lambda i: (0, i))],
            out_specs=[], core_axis_name=("c", "s"), dimension_semantics=(pltpu.PARALLEL,),
        )(x_hbm, i_hbm)

    return k(x, dest)


def sc_gather_rows(table, idx):
    m = idx.shape[1]
    d = table.shape[1]

    @functools.partial(pl.kernel, out_type=jax.ShapeDtypeStruct((m, d), table.dtype), mesh=_sc_mesh(),
                       scratch_types=[])
    def k(t_hbm, i_hbm, o_hbm):
        def body(i_vmem, o_vmem):
            pltpu.sync_copy(t_hbm.at[i_vmem.at[0]], o_vmem)

        pltpu.emit_pipeline(
            body, grid=(m // SC_WINDOW,),
            in_specs=[pl.BlockSpec((1, SC_WINDOW), lambda i: (0, i))],
            out_specs=[pl.BlockSpec((SC_WINDOW, d), lambda i: (i, 0))],
            core_axis_name=("c", "s"), dimension_semantics=(pltpu.PARALLEL,),
        )(i_hbm, o_hbm)

    return k(table, idx)


def _expert_kernel(be_ref, nu_ref, xlo_ref, xhi_ref, w1_ref, b1_ref, w2_ref, b2_ref, ylo_ref, yhi_ref, w1b, w2b):
    i = pl.program_id(0)
    new_expert = jnp.logical_or(i == 0, be_ref[i] != be_ref[jnp.maximum(i - 1, 0)])

    @pl.when(jnp.logical_and(i < nu_ref[0], new_expert))
    def _():
        w1b[...] = w1_ref[0, 0].astype(bf16)
        w2b[...] = w2_ref[0, 0].astype(bf16)

    @pl.when(i < nu_ref[0])
    def _():
        for r0 in range(0, MOE_BLK, SUB_ROWS):
            rows = slice(r0, r0 + SUB_ROWS)
            lo = xlo_ref[rows, :]
            hi = xhi_ref[rows, :]
            x = jnp.concatenate([_unpack_lo(lo).astype(bf16), _unpack_lo(hi).astype(bf16),
                                 _unpack_hi(lo).astype(bf16), _unpack_hi(hi).astype(bf16)], axis=1)
            u = jnp.dot(x, w1b[...], preferred_element_type=f32) + b1_ref[0, 0]
            glu = jnp.minimum(u[:, :D_EXPERT], SWIGLU_LIMIT)
            lin = jnp.clip(u[:, D_EXPERT:], -SWIGLU_LIMIT, SWIGLU_LIMIT)
            act = (glu * _sigmoid(SWIGLU_ALPHA * glu) * (lin + 1.0)).astype(bf16)
            y = jnp.dot(act, w2b[...], preferred_element_type=f32) + b2_ref[0, 0]
            ylo, yhi = _pack_row(y)
            ylo_ref[rows, :] = ylo
            yhi_ref[rows, :] = yhi


def expert_ffn(layer, block_expert, n_used, xs_lo, xs_hi, w1, b1, w2, b2):
    n_rows = xs_lo.shape[0]
    nb = n_rows // MOE_BLK
    rows = pl.BlockSpec((MOE_BLK, QUART), lambda i, be, nu: (jnp.minimum(i, nu[0] - 1), 0))
    wsel = lambda shape: pl.BlockSpec((1, 1) + shape, lambda i, be, nu: (layer, be[i], 0, 0))
    out = jax.ShapeDtypeStruct((n_rows, QUART), i32)
    return pl.pallas_call(
        _expert_kernel,
        grid_spec=pltpu.PrefetchScalarGridSpec(
            num_scalar_prefetch=2, grid=(nb,),
            in_specs=[rows, rows, wsel((D_MODEL, 2 * D_EXPERT)), wsel((1, 2 * D_EXPERT)),
                      wsel((D_EXPERT, D_MODEL)), wsel((1, D_MODEL))],
            out_specs=[rows, rows],
            scratch_shapes=[pltpu.VMEM((D_MODEL, 2 * D_EXPERT), bf16), pltpu.VMEM((D_EXPERT, D_MODEL), bf16)]),
        out_shape=[out, out],
        compiler_params=_cparams("arbitrary"), name="expert_ffn",
    )(block_expert, n_used, xs_lo, xs_hi, w1, b1, w2, b2)


def _combine_kernel(ylo_ref, yhi_ref, gt_ref, h1_ref, g_ref, b_ref, of_ref):
    z = DN_ALPHA * h1_ref[...] + _moe_sum(ylo_ref, yhi_ref, gt_ref, slice(None))
    of_ref[...] = _layer_norm(z, g_ref[...], b_ref[...])


def combine(yk_lo, yk_hi, gates_t, h1f, g, b):
    n, d = h1f.shape
    t = ROW_TILE
    ysp = pl.BlockSpec((TOP_K, t, QUART), lambda i: (0, i, 0))
    rowb = lambda c: pl.BlockSpec((t, c), lambda i: (i, 0))
    vec = pl.BlockSpec((1, d), lambda i: (0, 0))
    return pl.pallas_call(
        _combine_kernel, grid=(n // t,),
        in_specs=[ysp, ysp, rowb(TOP_K), rowb(d), vec, vec],
        out_specs=rowb(d), out_shape=jax.ShapeDtypeStruct((n, d), f32),
        compiler_params=_cparams("parallel"), name="combine",
    )(yk_lo, yk_hi, gates_t, h1f, g.reshape(1, d), b.reshape(1, d))


def _rope_tables(positions):
    inv = ROPE_THETA ** (-jnp.arange(0, C_ROPE, 2, dtype=f32) / C_ROPE)
    ang = positions.reshape(-1).astype(f32)[:, None] * inv
    cos, sin = jnp.cos(ang), jnp.sin(ang)
    pad = jnp.zeros((ang.shape[0], LANES - C_ROPE), f32)
    return jnp.concatenate([cos, cos, pad], axis=1), jnp.concatenate([-sin, sin, pad], axis=1)


def _attn_a_bias(rel_tables):
    depth = rel_tables.shape[0]
    r = jnp.arange(A_QBLK)[:, None]
    c = jnp.arange(A_BAND)[None, :]
    rel = jnp.clip(c - A_LEFT - r, -A_MAX_REL, A_MAX_REL) + A_MAX_REL
    own = c - CHUNK * (r // CHUNK)
    valid = (own >= 0) & (own < A_LEFT + CHUNK)
    onehot = (rel[:, :, None] == jnp.arange(2 * A_MAX_REL + 1)[None, None, :]).astype(f32)
    bias = jnp.einsum("rck,lhk->lhrc", onehot, rel_tables.astype(f32), precision=lax.Precision.HIGHEST)
    bias = jnp.where(valid[None, None], bias * LOG2E, NEG_INF)
    return bias.reshape(depth, A_HEADS // 2, 2 * A_QBLK, A_BAND)


def _layout_w_uq(w_uq):
    w = w_uq.reshape(C_Q_RANK, C_HEADS, C_QK)
    out = jnp.zeros((C_Q_RANK, C_HEADS, C_SLOT), w_uq.dtype)
    for h in range(C_HEADS):
        off = (h % 2) * C_NOPE
        out = out.at[:, h, off:off + C_NOPE].set(w[:, h, :C_NOPE])
        out = out.at[:, h, LANES:LANES + C_ROPE].set(w[:, h, C_NOPE:])
    return out.reshape(C_Q_RANK, C_HEADS * C_SLOT)


def _layout_w_ukv(w_ukv):
    w = w_ukv.reshape(C_KV_RANK, C_HEADS, C_NOPE + C_V)
    return jnp.concatenate([w[:, :, :C_NOPE].reshape(C_KV_RANK, -1), w[:, :, C_NOPE:].reshape(C_KV_RANK, -1)], axis=1)


def _moe(layer, hf, lo, hi, w_rt, b_r, w1, b1, w2, b2):
    n = hf.shape[0]
    n_rows = n * TOP_K + N_EXPERTS * MOE_BLK
    nb = n_rows // MOE_BLK
    idx, gates, rank, cnt = router(hf, w_rt, b_r)
    counts = cnt[:, 0].astype(i32)
    padded = (counts + MOE_BLK - 1) // MOE_BLK * MOE_BLK
    pad_end = jnp.cumsum(padded)
    pad_start = (pad_end - padded).astype(f32).reshape(N_EXPERTS, 1)
    block_start = jnp.arange(nb, dtype=i32) * MOE_BLK
    block_expert = jnp.minimum(jnp.sum((pad_end[None, :] <= block_start[:, None]).astype(i32), axis=1),
                               N_EXPERTS - 1)
    n_used = (pad_end[-1:] // MOE_BLK).astype(i32)
    dest = dest_rows(idx, rank, pad_start)
    xs_lo = sc_scatter_rows(lo, dest, n_rows)
    xs_hi = sc_scatter_rows(hi, dest, n_rows)
    ys_lo, ys_hi = expert_ffn(layer, block_expert, n_used, xs_lo, xs_hi, w1, b1, w2, b2)
    flat = dest.reshape(1, TOP_K * n)
    yk_lo = sc_gather_rows(ys_lo, flat)
    yk_hi = sc_gather_rows(ys_hi, flat)
    return yk_lo.reshape(TOP_K, n, QUART), yk_hi.reshape(TOP_K, n, QUART), gates.T


def kernel(x, positions, ln_in_g, ln_in_b, w_in, w_gate, b_gate, rel_bias, conv_w, conv_b, conv_ln_g, conv_ln_b, w_pw2, q_norm_g, kv_norm_g, w_uq, w_ukv, w_oa, w_oc, w_out, ln1_g, ln1_b, w_router, b_router, w1, b1, w2, b2, ln2_g, ln2_b):
    batch, seq, d = x.shape
    n = batch * seq
    cosm, sinm = _rope_tables(positions)
    a_bias = _attn_a_bias(rel_bias)
    ab = 3 * A_WIDTH + 2 * CONV_CH
    row1 = lambda v: v.reshape(1, -1)
    b1r = b1.reshape(DEPTH, N_EXPERTS, 1, -1)
    b2r = b2.reshape(DEPTH, N_EXPERTS, 1, -1)
    moe_out = None
    for l in range(DEPTH):
        w_ab = w_in[l, :, :ab].astype(bf16)
        w_cq = w_in[l, :, ab:ab + C_Q_RANK].astype(bf16)
        w_ckv = w_in[l, :, ab + C_Q_RANK:ab + C_Q_RANK + C_KV_RANK].astype(bf16)
        w_kr = jnp.pad(w_in[l, :, ab + C_Q_RANK + C_KV_RANK:], ((0, 0), (0, LANES - C_ROPE))).astype(bf16)
        if l == 0:
            hf, hb, qkvg = entry_first(x.reshape(n, d), ln_in_g, ln_in_b, w_ab)
        else:
            hf, hb, qkvg = entry_moe(*moe_out, h1f, ln2_g[l - 1], ln2_b[l - 1], w_ab)
        qc, kc, vc = mla_prep(hb, w_cq, w_ckv, w_kr, row1(q_norm_g[l]), row1(kv_norm_g[l]),
                              _layout_w_uq(w_uq[l]).astype(bf16), _layout_w_ukv(w_ukv[l]).astype(bf16), cosm, sinm)
        cb = conv_module(qkvg, conv_w[l], conv_b[l], conv_ln_g[l], conv_ln_b[l], batch, seq)
        ya = attn_a(qkvg, a_bias[l], batch, seq)
        yc = mla_attn_pairs(qc, kc, vc, batch, seq)
        h1f, lo, hi = mix_layer(ya, cb, yc, hb, hf, w_oa[l].astype(bf16), w_pw2[l].astype(bf16),
                                w_oc[l].astype(bf16), w_gate[l].astype(bf16), row1(b_gate[l]),
                                w_out[l].astype(bf16), row1(ln1_g[l]), row1(ln1_b[l]))
        moe_out = _moe(l, h1f, lo, hi, w_router[l].T, b_router[l].reshape(N_EXPERTS, 1), w1, b1r, w2, b2r)
    return combine(*moe_out, h1f, ln2_g[DEPTH - 1], ln2_b[DEPTH - 1]).reshape(batch, seq, d)
```

```python
import functools

import jax
import jax.numpy as jnp
from jax import lax
from jax.experimental import pallas as pl
from jax.experimental.pallas import tpu as pltpu
from jax.experimental.pallas import tpu_sc as plsc

f32 = jnp.float32
bf16 = jnp.bfloat16
i32 = jnp.int32

D_MODEL = 1024
DEPTH = 4
CHUNK = 64
A_HEADS = 8
A_HEAD_DIM = 64
A_WIDTH = A_HEADS * A_HEAD_DIM
A_LEFT = 8 * CHUNK
A_MAX_REL = 128
CONV_CH = 512
CONV_WIDTH = 31
C_HEADS = 8
C_NOPE = 64
C_ROPE = 32
C_V = 64
C_QK = C_NOPE + C_ROPE
C_Q_RANK = 384
C_KV_RANK = 256
ROPE_THETA = 10000.0
N_EXPERTS = 32
TOP_K = 4
D_EXPERT = 1024
SWIGLU_ALPHA = 1.702
SWIGLU_LIMIT = 7.0
DN_ALPHA = (2 * DEPTH) ** 0.25
LN_EPS = 1e-5
RMS_EPS = 1e-6
NEG_INF = -1e30
LOG2E = 1.4426950408889634

LANES = 128
A_QBLK = 2 * CHUNK
A_STEP = 8 * A_QBLK
A_BAND = A_LEFT + A_QBLK
C_BLK = 256
C_SLOT = 256
CONV_BLK = 256
CONV_HALO = 32
MOE_BLK = 512
ROW_TILE = 512
MIX_TILE = 512
SUB_ROWS = 256
ROUTER_TILE = 512
HALF = D_MODEL // 2
QUART = D_MODEL // 4
SC_WINDOW = 128
VMEM_LIMIT = 56 * 1024 * 1024


def _cparams(*sem):
    return pltpu.CompilerParams(dimension_semantics=tuple(sem), vmem_limit_bytes=VMEM_LIMIT)


def _layer_norm(x, g, b):
    mu = jnp.mean(x, axis=-1, keepdims=True)
    xc = x - mu
    var = jnp.mean(xc * xc, axis=-1, keepdims=True)
    return xc * lax.rsqrt(var + LN_EPS) * g + b


def _sigmoid(x):
    return 1.0 / (1.0 + jnp.exp(-x))


def _pack2(a, b):
    ab = lax.bitcast_convert_type(a.astype(bf16).astype(f32), i32)
    bb = lax.bitcast_convert_type(b.astype(bf16).astype(f32), i32)
    return lax.shift_right_logical(ab, 16) | (bb & jnp.int32(-65536))


def _unpack_lo(w):
    return lax.bitcast_convert_type(lax.shift_left(w, 16), f32)


def _unpack_hi(w):
    return lax.bitcast_convert_type(w & jnp.int32(-65536), f32)


def _pack_row(y):
    return (_pack2(y[:, 0:QUART], y[:, 2 * QUART:3 * QUART]),
            _pack2(y[:, QUART:2 * QUART], y[:, 3 * QUART:4 * QUART]))


def _moe_sum(ylo_ref, yhi_ref, gt_ref, rows):
    gt = gt_ref[rows, :]
    parts = [None] * 4
    for k in range(TOP_K):
        gk = gt[:, k:k + 1]
        lo = ylo_ref[k, rows, :]
        hi = yhi_ref[k, rows, :]
        vals = (_unpack_lo(lo), _unpack_lo(hi), _unpack_hi(lo), _unpack_hi(hi))
        for p in range(4):
            parts[p] = gk * vals[p] if parts[p] is None else parts[p] + gk * vals[p]
    return jnp.concatenate(parts, axis=1)


def _project(y, rows, w_ref, hf_ref, hb_ref, o_ref):
    hf_ref[rows, :] = y
    yb = y.astype(bf16)
    hb_ref[rows, :] = yb
    acc = jnp.dot(yb, w_ref[...], preferred_element_type=f32)
    o_ref[rows, :A_WIDTH] = (acc[:, :A_WIDTH] * (A_HEAD_DIM ** -0.5 * LOG2E)).astype(bf16)
    o_ref[rows, A_WIDTH:] = acc[:, A_WIDTH:].astype(bf16)


def _entry_first_kernel(x_ref, g_ref, b_ref, w_ref, hf_ref, hb_ref, o_ref):
    for r0 in range(0, ROW_TILE, SUB_ROWS):
        rows = slice(r0, r0 + SUB_ROWS)
        _project(_layer_norm(x_ref[rows, :], g_ref[...], b_ref[...]), rows, w_ref, hf_ref, hb_ref, o_ref)


def _entry_moe_kernel(ylo_ref, yhi_ref, gt_ref, h1_ref, g_ref, b_ref, w_ref, hf_ref, hb_ref, o_ref):
    for r0 in range(0, ROW_TILE, SUB_ROWS):
        rows = slice(r0, r0 + SUB_ROWS)
        z = DN_ALPHA * h1_ref[rows, :] + _moe_sum(ylo_ref, yhi_ref, gt_ref, rows)
        _project(_layer_norm(z, g_ref[...], b_ref[...]), rows, w_ref, hf_ref, hb_ref, o_ref)


def _entry_call(body, n, w, row_inputs, row_specs, g, b, name):
    d = D_MODEL
    c = w.shape[1]
    t = ROW_TILE
    vec = pl.BlockSpec((1, d), lambda i: (0, 0))
    rowb = lambda cc: pl.BlockSpec((t, cc), lambda i: (i, 0))
    return pl.pallas_call(
        body, grid=(n // t,),
        in_specs=row_specs + [vec, vec, pl.BlockSpec((d, c), lambda i: (0, 0))],
        out_specs=[rowb(d), rowb(d), rowb(c)],
        out_shape=[jax.ShapeDtypeStruct((n, d), f32), jax.ShapeDtypeStruct((n, d), bf16),
                   jax.ShapeDtypeStruct((n, c), bf16)],
        compiler_params=_cparams("parallel"), name=name,
    )(*row_inputs, g.reshape(1, d), b.reshape(1, d), w)


def entry_first(x2d, g, b, w):
    n, d = x2d.shape
    return _entry_call(_entry_first_kernel, n, w, [x2d], [pl.BlockSpec((ROW_TILE, d), lambda i: (i, 0))], g, b,
                       "entry_first")


def entry_moe(yk_lo, yk_hi, gates_t, h1f, g, b, w):
    n, d = h1f.shape
    t = ROW_TILE
    ysp = pl.BlockSpec((TOP_K, t, QUART), lambda i: (0, i, 0))
    specs = [ysp, ysp, pl.BlockSpec((t, TOP_K), lambda i: (i, 0)), pl.BlockSpec((t, d), lambda i: (i, 0))]
    return _entry_call(_entry_moe_kernel, n, w, [yk_lo, yk_hi, gates_t, h1f], specs, g, b, "entry_moe")


def _mla_prep_kernel(hb_ref, wdown_ref, gq_ref, gkv_ref, wuq_ref, wukv_ref, cos_ref, sin_ref, qc_ref, kc_ref, vc_ref):
    cosm = cos_ref[...]
    sinm = sin_ref[...]
    lane = lax.broadcasted_iota(i32, cosm.shape, 1)
    half = C_ROPE // 2
    scale = C_QK ** -0.5 * LOG2E

    def rope(x):
        swapped = jnp.where(lane < half, pltpu.roll(x, LANES - half, 1), pltpu.roll(x, half, 1))
        return x * cosm + swapped * sinm

    def rms(x, g):
        return (x * lax.rsqrt(jnp.mean(x * x, axis=-1, keepdims=True) + RMS_EPS) * g).astype(bf16)

    down = jnp.dot(hb_ref[...], wdown_ref[...], preferred_element_type=f32)
    kv_lo = C_Q_RANK
    kr_lo = C_Q_RANK + C_KV_RANK
    q = jnp.dot(rms(down[:, :kv_lo], gq_ref[...]), wuq_ref[...], preferred_element_type=f32)
    rope_lo = C_HEADS * C_NOPE
    heads_per_vreg = LANES // C_ROPE
    for h in range(C_HEADS):
        lo = h * C_SLOT
        pair = q[:, (h // 2) * LANES:(h // 2 + 1) * LANES]
        mine = (lane >= C_NOPE) if h % 2 else (lane < C_NOPE)
        qc_ref[:, lo:lo + LANES] = (jnp.where(mine, pair, 0.0) * scale).astype(bf16)
        group = q[:, rope_lo + (h // heads_per_vreg) * LANES:rope_lo + (h // heads_per_vreg + 1) * LANES]
        shift = (h % heads_per_vreg) * C_ROPE
        mine = pltpu.roll(group, LANES - shift, 1) if shift else group
        qc_ref[:, lo + LANES:lo + C_SLOT] = (rope(mine) * scale).astype(bf16)
    kv = jnp.dot(rms(down[:, kv_lo:kr_lo], gkv_ref[...]), wukv_ref[...], preferred_element_type=f32)
    kr = rope(down[:, kr_lo:]).astype(bf16)
    for j in range(C_HEADS // 2):
        lo = j * C_SLOT
        kc_ref[:, lo:lo + LANES] = kv[:, j * LANES:(j + 1) * LANES].astype(bf16)
        kc_ref[:, lo + LANES:lo + C_SLOT] = kr
    vc_ref[...] = kv[:, C_HEADS * C_NOPE:].astype(bf16)


def mla_prep(hb, wdown, gq, gkv, wuq, wukv, cosm, sinm):
    n, d = hb.shape
    t = ROW_TILE
    full = lambda a: pl.BlockSpec(a.shape, lambda i: (0,) * a.ndim)
    rowb = lambda c: pl.BlockSpec((t, c), lambda i: (i, 0))
    qw, kw, vw = C_HEADS * C_SLOT, (C_HEADS // 2) * C_SLOT, C_HEADS * C_V
    return pl.pallas_call(
        _mla_prep_kernel, grid=(n // t,),
        in_specs=[rowb(d), full(wdown), full(gq), full(gkv), full(wuq), full(wukv), rowb(LANES), rowb(LANES)],
        out_specs=[rowb(qw), rowb(kw), rowb(vw)],
        out_shape=[jax.ShapeDtypeStruct((n, qw), bf16), jax.ShapeDtypeStruct((n, kw), bf16),
                   jax.ShapeDtypeStruct((n, vw), bf16)],
        compiler_params=_cparams("parallel"), name="mla_prep",
    )(hb, wdown, gq, gkv, wuq, wukv, cosm, sinm)


def _conv_rows(base, hs, sh, w_ref, cb_ref, lg_ref, lb_ref, o_ref):
    t = CONV_BLK
    first = CONV_HALO - CONV_WIDTH + 1
    a_of = [[a for a in range(CONV_HALO // 8 + 1) if first <= 8 * a + b <= CONV_HALO] for b in range(8)]
    for b in range(1, 8):
        length = 8 * max(a_of[b]) + t
        sh[b - 1, 0:length, :] = hs[base + b:base + b + length, :]
    rows = 64
    for r0 in range(0, t, rows):
        acc = jnp.zeros((rows, CONV_CH), f32) + cb_ref[...]
        for b in range(8):
            for a in a_of[b]:
                tap = 8 * a + b - first
                lo = r0 + 8 * a
                src = hs[base + lo:base + lo + rows, :] if b == 0 else sh[b - 1, lo:lo + rows, :]
                acc = acc + src * w_ref[tap:tap + 1, :]
        y = _layer_norm(acc, lg_ref[...], lb_ref[...])
        o_ref[base + r0:base + r0 + rows, :] = (y * _sigmoid(y)).astype(bf16)


def _conv_kernel(a_ref, g_ref, ap_ref, gp_ref, w_ref, cb_ref, lg_ref, lb_ref, o_ref, hs, sh):
    i = pl.program_id(1)
    prev = ap_ref[...].astype(f32) * _sigmoid(gp_ref[...].astype(f32))
    hs[0:CONV_HALO, :] = jnp.where(i > 0, prev, 0.0)
    hs[CONV_HALO:CONV_HALO + ROW_TILE, :] = a_ref[...].astype(f32) * _sigmoid(g_ref[...].astype(f32))
    for base in range(0, ROW_TILE, CONV_BLK):
        _conv_rows(base, hs, sh, w_ref, cb_ref, lg_ref, lb_ref, o_ref)


def conv_module(qkvg, w_dw, b_dw, ln_g, ln_b, batch, seq):
    n = batch * seq
    t = ROW_TILE
    nb = seq // t
    halo_per_tile = t // CONV_HALO
    a_col = 3 * A_WIDTH // CONV_CH
    cur = lambda col: pl.BlockSpec((t, CONV_CH), lambda b, i: (b * nb + i, col))
    halo = lambda col: pl.BlockSpec(
        (CONV_HALO, CONV_CH), lambda b, i: (jnp.maximum((b * nb + i) * halo_per_tile - 1, 0), col))
    vec = pl.BlockSpec((1, CONV_CH), lambda b, i: (0, 0))
    return pl.pallas_call(
        _conv_kernel, grid=(batch, nb),
        in_specs=[cur(a_col), cur(a_col + 1), halo(a_col), halo(a_col + 1),
                  pl.BlockSpec((CONV_WIDTH, CONV_CH), lambda b, i: (0, 0)), vec, vec, vec],
        out_specs=pl.BlockSpec((t, CONV_CH), lambda b, i: (b * nb + i, 0)),
        out_shape=jax.ShapeDtypeStruct((n, CONV_CH), bf16),
        scratch_shapes=[pltpu.VMEM((CONV_HALO + t, CONV_CH), f32),
                        pltpu.VMEM((7, CONV_HALO + CONV_BLK, CONV_CH), f32)],
        compiler_params=_cparams("parallel", "parallel"), name="conv_module",
    )(qkvg, qkvg, qkvg, qkvg, w_dw, b_dw.reshape(1, -1), ln_g.reshape(1, -1), ln_b.reshape(1, -1))


def _attn_a_kernel(q_ref, k_ref, v_ref, bias_ref, o_ref, kpad, vpad):
    qi = pl.program_id(1)
    seq = k_ref.shape[0]

    @pl.when(qi == 0)
    def _():
        kpad[0:A_LEFT, :] = jnp.zeros((A_LEFT, A_WIDTH), bf16)
        vpad[0:A_LEFT, :] = jnp.zeros((A_LEFT, A_WIDTH), bf16)
        kpad[A_LEFT:A_LEFT + seq, :] = k_ref[...]
        vpad[A_LEFT:A_LEFT + seq, :] = v_ref[...]

    col = lax.broadcasted_iota(i32, (2 * A_QBLK, A_BAND), 1)
    lane = lax.broadcasted_iota(i32, (A_QBLK, LANES), 1)
    ones = jnp.ones((A_BAND, LANES), bf16)
    for sub in range(A_STEP // A_QBLK):
        start = pl.multiple_of(qi * A_STEP + sub * A_QBLK, A_QBLK)
        before_start = jnp.where(col + start >= A_LEFT, 0.0, NEG_INF)
        rows = slice(sub * A_QBLK, (sub + 1) * A_QBLK)
        for j in range(A_HEADS // 2):
            cs = slice(j * LANES, (j + 1) * LANES)
            qp = q_ref[rows, cs].astype(f32)
            qs = jnp.concatenate([jnp.where(lane < A_HEAD_DIM, qp, 0.0), jnp.where(lane >= A_HEAD_DIM, qp, 0.0)],
                                 axis=0).astype(bf16)
            kb = kpad[pl.ds(start, A_BAND), cs]
            vb = jnp.concatenate([vpad[pl.ds(start, A_BAND), cs], ones], axis=1)
            s = lax.dot_general(qs, kb, (((1,), (1,)), ((), ())), preferred_element_type=f32)
            s = s + bias_ref[j] + before_start
            p = jnp.exp2((s - jnp.max(s, axis=-1, keepdims=True)).astype(bf16))
            o = jnp.dot(p, vb, preferred_element_type=f32)
            o = o[:, :LANES] / o[:, LANES:]
            o_ref[rows, cs] = jnp.where(lane < A_HEAD_DIM, o[:A_QBLK], o[A_QBLK:]).astype(bf16)


def attn_a(qkvg, bias, batch, seq):
    n = batch * seq
    nq = seq // A_STEP
    return pl.pallas_call(
        _attn_a_kernel, grid=(batch, nq),
        in_specs=[pl.BlockSpec((A_STEP, A_WIDTH), lambda b, i: (b * nq + i, 0)),
                  pl.BlockSpec((seq, A_WIDTH), lambda b, i: (b, 1)),
                  pl.BlockSpec((seq, A_WIDTH), lambda b, i: (b, 2)),
                  pl.BlockSpec(bias.shape, lambda b, i: (0, 0, 0))],
        out_specs=pl.BlockSpec((A_STEP, A_WIDTH), lambda b, i: (b * nq + i, 0)),
        out_shape=jax.ShapeDtypeStruct((n, A_WIDTH), bf16),
        scratch_shapes=[pltpu.VMEM((A_LEFT + seq, A_WIDTH), bf16), pltpu.VMEM((A_LEFT + seq, A_WIDTH), bf16)],
        compiler_params=_cparams("parallel", "arbitrary"), name="attn_a",
    )(qkvg, qkvg, qkvg, bias)


def _mla_pair_kernel(q_ref, k_ref, v_ref, o_ref):
    t = C_BLK
    seq = k_ref.shape[0]
    row = lax.broadcasted_iota(i32, (2 * t, t), 0)
    col = lax.broadcasted_iota(i32, (2 * t, t), 1)
    diag_ok = (col // CHUNK) <= ((row % t) // CHUNK)
    lane = lax.broadcasted_iota(i32, (t, LANES), 1)
    nt = lambda a, b: lax.dot_general(a, b, (((1,), (1,)), ((), ())), preferred_element_type=f32)
    for r in range(seq // t):
        lo = r * t
        vext = jnp.concatenate([v_ref[0:lo + t, :], jnp.ones((lo + t, LANES), bf16)], axis=1)
        q = jnp.concatenate([q_ref[lo:lo + t, 0:C_SLOT], q_ref[lo:lo + t, C_SLOT:2 * C_SLOT]], axis=0)
        s_diag = jnp.where(diag_ok, nt(q, k_ref[lo:lo + t, :]), NEG_INF)
        m = jnp.max(s_diag, axis=-1, keepdims=True)
        if r > 0:
            s_low = nt(q, k_ref[0:lo, :])
            m = jnp.maximum(m, jnp.max(s_low, axis=-1, keepdims=True))
            p = jnp.concatenate([jnp.exp2((s_low - m).astype(bf16)), jnp.exp2((s_diag - m).astype(bf16))], axis=1)
        else:
            p = jnp.exp2((s_diag - m).astype(bf16))
        o = jnp.dot(p, vext, preferred_element_type=f32)
        o = o[:, :LANES] / o[:, LANES:]
        o_ref[lo:lo + t, :] = jnp.where(lane < C_V, o[:t], o[t:]).astype(bf16)


def mla_attn_pairs(qc, kc, vc, batch, seq):
    n = batch * seq
    pairs = C_HEADS // 2
    return pl.pallas_call(
        _mla_pair_kernel, grid=(batch, pairs),
        in_specs=[pl.BlockSpec((seq, 2 * C_SLOT), lambda b, j: (b, j)),
                  pl.BlockSpec((seq, C_SLOT), lambda b, j: (b, j)),
                  pl.BlockSpec((seq, LANES), lambda b, j: (b, j))],
        out_specs=pl.BlockSpec((seq, LANES), lambda b, j: (b, j)),
        out_shape=jax.ShapeDtypeStruct((n, C_HEADS * C_V), bf16),
        compiler_params=_cparams("parallel", "parallel"), name="mla_attn",
    )(qc, kc, vc)


def _mix_kernel(ya_ref, cb_ref, yc_ref, hb_ref, hf_ref, woa_ref, wpw_ref, woc_ref, wg_ref, bg_ref, wout_ref,
                g_ref, b_ref, of_ref, lo_ref, hi_ref):
    d = D_MODEL
    for r0 in range(0, MIX_TILE, SUB_ROWS):
        rows = slice(r0, r0 + SUB_ROWS)
        gates = _sigmoid(jnp.dot(hb_ref[rows, :], wg_ref[...], preferred_element_type=f32) + bg_ref[...])
        mix = (gates[:, 0:d] * jnp.dot(ya_ref[rows, :], woa_ref[...], preferred_element_type=f32)
               + gates[:, d:2 * d] * jnp.dot(cb_ref[rows, :], wpw_ref[...], preferred_element_type=f32)
               + gates[:, 2 * d:3 * d] * jnp.dot(yc_ref[rows, :], woc_ref[...], preferred_element_type=f32))
        z = DN_ALPHA * hf_ref[rows, :] + jnp.dot(mix.astype(bf16), wout_ref[...], preferred_element_type=f32)
        y = _layer_norm(z, g_ref[...], b_ref[...])
        of_ref[rows, :] = y
        lo, hi = _pack_row(y)
        lo_ref[rows, :] = lo
        hi_ref[rows, :] = hi


def mix_layer(ya, cb, yc, hb, hf, woa, wpw, woc, wg, bg, wout, g, b):
    n, d = hf.shape
    t = MIX_TILE
    full = lambda a: pl.BlockSpec(a.shape, lambda i: (0,) * a.ndim)
    rowb = lambda c: pl.BlockSpec((t, c), lambda i: (i, 0))
    return pl.pallas_call(
        _mix_kernel, grid=(n // t,),
        in_specs=[rowb(ya.shape[1]), rowb(cb.shape[1]), rowb(yc.shape[1]), rowb(d), rowb(d),
                  full(woa), full(wpw), full(woc), full(wg), full(bg), full(wout), full(g), full(b)],
        out_specs=[rowb(d), rowb(QUART), rowb(QUART)],
        out_shape=[jax.ShapeDtypeStruct((n, d), f32), jax.ShapeDtypeStruct((n, QUART), i32),
                   jax.ShapeDtypeStruct((n, QUART), i32)],
        compiler_params=_cparams("parallel"), name="mix_layer",
    )(ya, cb, yc, hb, hf, woa, wpw, woc, wg, bg, wout, g, b)


def _router_kernel(h_ref, wr_ref, br_ref, idx_ref, gate_ref, rank_ref, cnt_ref, base):
    i = pl.program_id(0)
    t = ROUTER_TILE
    e = N_EXPERTS

    @pl.when(i == 0)
    def _():
        base[...] = jnp.zeros_like(base)

    def split(x):
        hi = x.astype(bf16)
        return hi, (x - hi.astype(f32)).astype(bf16)

    nt = lambda a, b: lax.dot_general(a, b, (((1,), (1,)), ((), ())), preferred_element_type=f32)
    w_hi, w_lo = split(wr_ref[...])
    h_hi, h_lo = split(h_ref[...])
    logits = nt(w_hi, h_hi) + nt(w_lo, h_hi) + nt(w_hi, h_lo) + br_ref[...]
    row = lax.broadcasted_iota(i32, (e, t), 0).astype(f32)
    vals, hots = [], []
    cur = logits
    for k in range(TOP_K):
        m = jnp.max(cur, axis=0, keepdims=True)
        first = jnp.min(jnp.where(cur == m, row, float(e)), axis=0, keepdims=True)
        hot = row == first
        cur = jnp.where(hot, -jnp.inf, cur)
        vals.append(m)
        hots.append(hot)
        idx_ref[k:k + 1, :] = first.astype(i32)
    ex = [jnp.exp(v - vals[0]) for v in vals]
    den = ex[0] + ex[1] + ex[2] + ex[3]
    for k in range(TOP_K):
        gate_ref[k:k + 1, :] = ex[k] / den
    onehot = jnp.concatenate([jnp.where(h, 1.0, 0.0) for h in hots], axis=0)
    r = lax.broadcasted_iota(i32, (t, t), 0)
    c = lax.broadcasted_iota(i32, (t, t), 1)
    upper = jnp.where(r <= c, 1.0, 0.0).astype(bf16)
    prefix = jnp.dot(onehot.astype(bf16), upper, preferred_element_type=f32)
    counts = jnp.sum(onehot, axis=1, keepdims=True)
    offset = base[:, 0:1]
    for k in range(TOP_K):
        sel = jnp.where(hots[k], prefix[k * e:(k + 1) * e, :] - 1.0 + offset, 0.0)
        rank_ref[k:k + 1, :] = jnp.sum(sel, axis=0, keepdims=True).astype(i32)
        offset = offset + counts[k * e:(k + 1) * e, :]
    base[...] = jnp.broadcast_to(offset, base.shape)
    cnt_ref[...] = base[...]


def router(hf, w_rt, b_r):
    n, d = hf.shape
    t = ROUTER_TILE
    tok = pl.BlockSpec((TOP_K, t), lambda i: (0, i))
    return pl.pallas_call(
        _router_kernel, grid=(n // t,),
        in_specs=[pl.BlockSpec((t, d), lambda i: (i, 0)), pl.BlockSpec((N_EXPERTS, d), lambda i: (0, 0)),
                  pl.BlockSpec((N_EXPERTS, 1), lambda i: (0, 0))],
        out_specs=[tok, tok, tok, pl.BlockSpec((N_EXPERTS, LANES), lambda i: (0, 0))],
        out_shape=[jax.ShapeDtypeStruct((TOP_K, n), i32), jax.ShapeDtypeStruct((TOP_K, n), f32),
                   jax.ShapeDtypeStruct((TOP_K, n), i32), jax.ShapeDtypeStruct((N_EXPERTS, LANES), f32)],
        scratch_shapes=[pltpu.VMEM((N_EXPERTS, LANES), f32)],
        compiler_params=_cparams("arbitrary"), name="router",
    )(hf, w_rt, b_r)


def _dest_kernel(idx_ref, rank_ref, start_ref, dest_ref):
    t = idx_ref.shape[1]
    row = lax.broadcasted_iota(i32, (N_EXPERTS, t), 0)
    for k in range(TOP_K):
        hot = row == idx_ref[k:k + 1, :]
        off = jnp.sum(jnp.where(hot, start_ref[...], 0.0), axis=0, keepdims=True)
        dest_ref[k:k + 1, :] = rank_ref[k:k + 1, :] + off.astype(i32)


def dest_rows(idx, rank, pad_start):
    n = idx.shape[1]
    t = ROUTER_TILE
    tok = pl.BlockSpec((TOP_K, t), lambda i: (0, i))
    return pl.pallas_call(
        _dest_kernel, grid=(n // t,),
        in_specs=[tok, tok, pl.BlockSpec((N_EXPERTS, 1), lambda i: (0, 0))],
        out_specs=tok, out_shape=jax.ShapeDtypeStruct((TOP_K, n), i32),
        compiler_params=_cparams("parallel"), name="dest_rows",
    )(idx, rank, pad_start)


def _sc_mesh():
    return plsc.VectorSubcoreMesh(core_axis_name="c", subcore_axis_name="s")


def sc_scatter_rows(x_lo, x_hi, dest, n_rows):
    n, d = x_lo.shape
    kk = dest.shape[0]
    out = jax.ShapeDtypeStruct((n_rows, d), x_lo.dtype)

    @functools.partial(pl.kernel, out_type=(out, out), mesh=_sc_mesh(), scratch_types=[])
    def k(lo_hbm, hi_hbm, i_hbm, olo_hbm, ohi_hbm):
        for x_hbm, o_hbm in ((lo_hbm, olo_hbm), (hi_hbm, ohi_hbm)):
            def body(x_vmem, i_vmem, o_hbm=o_hbm):
                for j in range(kk):
                    pltpu.sync_copy(x_vmem, o_hbm.at[i_vmem.at[j]])

            pltpu.emit_pipeline(
                body, grid=(n // SC_WINDOW,),
                in_specs=[pl.BlockSpec((SC_WINDOW, d), lambda i: (i, 0)),
                          pl.BlockSpec((kk, SC_WINDOW), lambda i: (0, i))],
                out_specs=[], core_axis_name=("c", "s"), dimension_semantics=(pltpu.PARALLEL,),
            )(x_hbm, i_hbm)

    return k(x_lo, x_hi, dest)


def sc_gather_rows(t_lo, t_hi, idx):
    m = idx.shape[1]
    d = t_lo.shape[1]
    out = jax.ShapeDtypeStruct((m, d), t_lo.dtype)

    @functools.partial(pl.kernel, out_type=(out, out), mesh=_sc_mesh(), scratch_types=[])
    def k(lo_hbm, hi_hbm, i_hbm, olo_hbm, ohi_hbm):
        for t_hbm, o_hbm in ((lo_hbm, olo_hbm), (hi_hbm, ohi_hbm)):
            def body(i_vmem, o_vmem, t_hbm=t_hbm):
                pltpu.sync_copy(t_hbm.at[i_vmem.at[0]], o_vmem)

            pltpu.emit_pipeline(
                body, grid=(m // SC_WINDOW,),
                in_specs=[pl.BlockSpec((1, SC_WINDOW), lambda i: (0, i))],
                out_specs=[pl.BlockSpec((SC_WINDOW, d), lambda i: (i, 0))],
                core_axis_name=("c", "s"), dimension_semantics=(pltpu.PARALLEL,),
            )(i_hbm, o_hbm)

    return k(t_lo, t_hi, idx)


def _expert_kernel(be_ref, nu_ref, xlo_ref, xhi_ref, w1_ref, b1_ref, w2_ref, b2_ref, ylo_ref, yhi_ref, w1b, w2b):
    i = pl.program_id(0)
    new_expert = jnp.logical_or(i == 0, be_ref[i] != be_ref[jnp.maximum(i - 1, 0)])

    @pl.when(jnp.logical_and(i < nu_ref[0], new_expert))
    def _():
        w1b[...] = w1_ref[0, 0].astype(bf16)
        w2b[...] = w2_ref[0, 0].astype(bf16)

    @pl.when(i < nu_ref[0])
    def _():
        for r0 in range(0, MOE_BLK, SUB_ROWS):
            rows = slice(r0, r0 + SUB_ROWS)
            lo = xlo_ref[rows, :]
            hi = xhi_ref[rows, :]
            x = jnp.concatenate([_unpack_lo(lo).astype(bf16), _unpack_lo(hi).astype(bf16),
                                 _unpack_hi(lo).astype(bf16), _unpack_hi(hi).astype(bf16)], axis=1)
            u = jnp.dot(x, w1b[...], preferred_element_type=f32) + b1_ref[0, 0]
            glu = jnp.minimum(u[:, :D_EXPERT], SWIGLU_LIMIT)
            lin = jnp.clip(u[:, D_EXPERT:], -SWIGLU_LIMIT, SWIGLU_LIMIT)
            act = (glu * _sigmoid(SWIGLU_ALPHA * glu) * (lin + 1.0)).astype(bf16)
            y = jnp.dot(act, w2b[...], preferred_element_type=f32) + b2_ref[0, 0]
            ylo, yhi = _pack_row(y)
            ylo_ref[rows, :] = ylo
            yhi_ref[rows, :] = yhi


def expert_ffn(layer, block_expert, n_used, xs_lo, xs_hi, w1, b1, w2, b2):
    n_rows = xs_lo.shape[0]
    nb = n_rows // MOE_BLK
    rows = pl.BlockSpec((MOE_BLK, QUART), lambda i, be, nu: (jnp.minimum(i, nu[0] - 1), 0))
    wsel = lambda shape: pl.BlockSpec((1, 1) + shape, lambda i, be, nu: (layer, be[i], 0, 0))
    out = jax.ShapeDtypeStruct((n_rows, QUART), i32)
    return pl.pallas_call(
        _expert_kernel,
        grid_spec=pltpu.PrefetchScalarGridSpec(
            num_scalar_prefetch=2, grid=(nb,),
            in_specs=[rows, rows, wsel((D_MODEL, 2 * D_EXPERT)), wsel((1, 2 * D_EXPERT)),
                      wsel((D_EXPERT, D_MODEL)), wsel((1, D_MODEL))],
            out_specs=[rows, rows],
            scratch_shapes=[pltpu.VMEM((D_MODEL, 2 * D_EXPERT), bf16), pltpu.VMEM((D_EXPERT, D_MODEL), bf16)]),
        out_shape=[out, out],
        compiler_params=_cparams("arbitrary"), name="expert_ffn",
    )(block_expert, n_used, xs_lo, xs_hi, w1, b1, w2, b2)


def _combine_kernel(ylo_ref, yhi_ref, gt_ref, h1_ref, g_ref, b_ref, of_ref):
    z = DN_ALPHA * h1_ref[...] + _moe_sum(ylo_ref, yhi_ref, gt_ref, slice(None))
    of_ref[...] = _layer_norm(z, g_ref[...], b_ref[...])


def combine(yk_lo, yk_hi, gates_t, h1f, g, b):
    n, d = h1f.shape
    t = ROW_TILE
    ysp = pl.BlockSpec((TOP_K, t, QUART), lambda i: (0, i, 0))
    rowb = lambda c: pl.BlockSpec((t, c), lambda i: (i, 0))
    vec = pl.BlockSpec((1, d), lambda i: (0, 0))
    return pl.pallas_call(
        _combine_kernel, grid=(n // t,),
        in_specs=[ysp, ysp, rowb(TOP_K), rowb(d), vec, vec],
        out_specs=rowb(d), out_shape=jax.ShapeDtypeStruct((n, d), f32),
        compiler_params=_cparams("parallel"), name="combine",
    )(yk_lo, yk_hi, gates_t, h1f, g.reshape(1, d), b.reshape(1, d))


def _rope_tables(positions):
    inv = ROPE_THETA ** (-jnp.arange(0, C_ROPE, 2, dtype=f32) / C_ROPE)
    ang = positions.reshape(-1).astype(f32)[:, None] * inv
    cos, sin = jnp.cos(ang), jnp.sin(ang)
    pad = jnp.zeros((ang.shape[0], LANES - C_ROPE), f32)
    return jnp.concatenate([cos, cos, pad], axis=1), jnp.concatenate([-sin, sin, pad], axis=1)


def _attn_a_bias(rel_tables):
    depth = rel_tables.shape[0]
    r = jnp.arange(A_QBLK)[:, None]
    c = jnp.arange(A_BAND)[None, :]
    rel = jnp.clip(c - A_LEFT - r, -A_MAX_REL, A_MAX_REL) + A_MAX_REL
    own = c - CHUNK * (r // CHUNK)
    valid = (own >= 0) & (own < A_LEFT + CHUNK)
    onehot = (rel[:, :, None] == jnp.arange(2 * A_MAX_REL + 1)[None, None, :]).astype(f32)
    bias = jnp.einsum("rck,lhk->lhrc", onehot, rel_tables.astype(f32), precision=lax.Precision.HIGHEST)
    bias = jnp.where(valid[None, None], bias * LOG2E, NEG_INF)
    return bias.reshape(depth, A_HEADS // 2, 2 * A_QBLK, A_BAND)


def _layout_w_uq(w_uq):
    w = w_uq.reshape(C_Q_RANK, C_HEADS, C_QK)
    return jnp.concatenate([w[:, :, :C_NOPE].reshape(C_Q_RANK, -1), w[:, :, C_NOPE:].reshape(C_Q_RANK, -1)], axis=1)


def _layout_w_ukv(w_ukv):
    w = w_ukv.reshape(C_KV_RANK, C_HEADS, C_NOPE + C_V)
    return jnp.concatenate([w[:, :, :C_NOPE].reshape(C_KV_RANK, -1), w[:, :, C_NOPE:].reshape(C_KV_RANK, -1)], axis=1)


def _moe(layer, hf, lo, hi, w_rt, b_r, w1, b1, w2, b2):
    n = hf.shape[0]
    n_rows = n * TOP_K + N_EXPERTS * MOE_BLK
    nb = n_rows // MOE_BLK
    idx, gates, rank, cnt = router(hf, w_rt, b_r)
    counts = cnt[:, 0].astype(i32)
    padded = (counts + MOE_BLK - 1) // MOE_BLK * MOE_BLK
    pad_end = jnp.cumsum(padded)
    pad_start = (pad_end - padded).astype(f32).reshape(N_EXPERTS, 1)
    block_start = jnp.arange(nb, dtype=i32) * MOE_BLK
    block_expert = jnp.minimum(jnp.sum((pad_end[None, :] <= block_start[:, None]).astype(i32), axis=1),
                               N_EXPERTS - 1)
    n_used = (pad_end[-1:] // MOE_BLK).astype(i32)
    dest = dest_rows(idx, rank, pad_start)
    xs_lo, xs_hi = sc_scatter_rows(lo, hi, dest, n_rows)
    ys_lo, ys_hi = expert_ffn(layer, block_expert, n_used, xs_lo, xs_hi, w1, b1, w2, b2)
    yk_lo, yk_hi = sc_gather_rows(ys_lo, ys_hi, dest.reshape(1, TOP_K * n))
    return yk_lo.reshape(TOP_K, n, QUART), yk_hi.reshape(TOP_K, n, QUART), gates.T


def kernel(x, positions, ln_in_g, ln_in_b, w_in, w_gate, b_gate, rel_bias, conv_w, conv_b, conv_ln_g, conv_ln_b, w_pw2, q_norm_g, kv_norm_g, w_uq, w_ukv, w_oa, w_oc, w_out, ln1_g, ln1_b, w_router, b_router, w1, b1, w2, b2, ln2_g, ln2_b):
    batch, seq, d = x.shape
    n = batch * seq
    cosm, sinm = _rope_tables(positions)
    a_bias = _attn_a_bias(rel_bias)
    ab = 3 * A_WIDTH + 2 * CONV_CH
    row1 = lambda v: v.reshape(1, -1)
    b1r = b1.reshape(DEPTH, N_EXPERTS, 1, -1)
    b2r = b2.reshape(DEPTH, N_EXPERTS, 1, -1)
    moe_out = None
    for l in range(DEPTH):
        w_ab = w_in[l, :, :ab].astype(bf16)
        w_down = jnp.pad(w_in[l, :, ab:], ((0, 0), (0, LANES - C_ROPE))).astype(bf16)
        if l == 0:
            hf, hb, qkvg = entry_first(x.reshape(n, d), ln_in_g, ln_in_b, w_ab)
        else:
            hf, hb, qkvg = entry_moe(*moe_out, h1f, ln2_g[l - 1], ln2_b[l - 1], w_ab)
        qc, kc, vc = mla_prep(hb, w_down, row1(q_norm_g[l]), row1(kv_norm_g[l]),
                              _layout_w_uq(w_uq[l]).astype(bf16), _layout_w_ukv(w_ukv[l]).astype(bf16), cosm, sinm)
        cb = conv_module(qkvg, conv_w[l], conv_b[l], conv_ln_g[l], conv_ln_b[l], batch, seq)
        ya = attn_a(qkvg, a_bias[l], batch, seq)
        yc = mla_attn_pairs(qc, kc, vc, batch, seq)
        h1f, lo, hi = mix_layer(ya, cb, yc, hb, hf, w_oa[l].astype(bf16), w_pw2[l].astype(bf16),
                                w_oc[l].astype(bf16), w_gate[l].astype(bf16), row1(b_gate[l]),
                                w_out[l].astype(bf16), row1(ln1_g[l]), row1(ln1_b[l]))
        moe_out = _moe(l, h1f, lo, hi, w_router[l].T, b_router[l].reshape(N_EXPERTS, 1), w1, b1r, w2, b2r)
    return combine(*moe_out, h1f, ln2_g[DEPTH - 1], ln2_b[DEPTH - 1]).reshape(batch, seq, d)
```

```python
import functools

import jax
import jax.numpy as jnp
from jax import lax
from jax.experimental import pallas as pl
from jax.experimental.pallas import tpu as pltpu
from jax.experimental.pallas import tpu_sc as plsc

f32 = jnp.float32
bf16 = jnp.bfloat16
i32 = jnp.int32

D_MODEL = 1024
DEPTH = 4
CHUNK = 64
A_HEADS = 8
A_HEAD_DIM = 64
A_WIDTH = A_HEADS * A_HEAD_DIM
A_LEFT = 8 * CHUNK
A_MAX_REL = 128
CONV_CH = 512
CONV_WIDTH = 31
C_HEADS = 8
C_NOPE = 64
C_ROPE = 32
C_V = 64
C_QK = C_NOPE + C_ROPE
C_Q_RANK = 384
C_KV_RANK = 256
ROPE_THETA = 10000.0
N_EXPERTS = 32
TOP_K = 4
D_EXPERT = 1024
SWIGLU_ALPHA = 1.702
SWIGLU_LIMIT = 7.0
DN_ALPHA = (2 * DEPTH) ** 0.25
LN_EPS = 1e-5
RMS_EPS = 1e-6
NEG_INF = -1e30
LOG2E = 1.4426950408889634

LANES = 128
A_QBLK = 2 * CHUNK
A_STEP = 8 * A_QBLK
A_BAND = A_LEFT + A_QBLK
C_BLK = 256
C_STEP_PAIRS = 4
C_SLOT = 256
CONV_BLK = 256
CONV_HALO = 32
MOE_BLK = 512
ROW_TILE = 512
MIX_TILE = 512
SUB_ROWS = 256
ENTRY_SUB_ROWS = 128
ROUTER_TILE = 512
HALF = D_MODEL // 2
QUART = D_MODEL // 4
SC_WINDOW = 128
VMEM_LIMIT = 56 * 1024 * 1024


def _cparams(*sem):
    return pltpu.CompilerParams(dimension_semantics=tuple(sem), vmem_limit_bytes=VMEM_LIMIT)


def _layer_norm(x, g, b):
    mu = jnp.mean(x, axis=-1, keepdims=True)
    xc = x - mu
    var = jnp.mean(xc * xc, axis=-1, keepdims=True)
    return xc * lax.rsqrt(var + LN_EPS) * g + b


def _sigmoid(x):
    return 1.0 / (1.0 + jnp.exp(-x))


def _pack2(a, b):
    ab = lax.bitcast_convert_type(a.astype(bf16).astype(f32), i32)
    bb = lax.bitcast_convert_type(b.astype(bf16).astype(f32), i32)
    return lax.shift_right_logical(ab, 16) | (bb & jnp.int32(-65536))


def _unpack_lo(w):
    return lax.bitcast_convert_type(lax.shift_left(w, 16), f32)


def _unpack_hi(w):
    return lax.bitcast_convert_type(w & jnp.int32(-65536), f32)


def _pack_row(y):
    return (_pack2(y[:, 0:QUART], y[:, 2 * QUART:3 * QUART]),
            _pack2(y[:, QUART:2 * QUART], y[:, 3 * QUART:4 * QUART]))


def _moe_sum(ylo_ref, yhi_ref, gt_ref, rows):
    gt = gt_ref[rows, :]
    parts = [None] * 4
    for k in range(TOP_K):
        gk = gt[:, k:k + 1]
        lo = ylo_ref[k, rows, :]
        hi = yhi_ref[k, rows, :]
        vals = (_unpack_lo(lo), _unpack_lo(hi), _unpack_hi(lo), _unpack_hi(hi))
        for p in range(4):
            parts[p] = gk * vals[p] if parts[p] is None else parts[p] + gk * vals[p]
    return jnp.concatenate(parts, axis=1)


def _project(y, rows, w_ref, hf_ref, hb_ref, o_ref):
    hf_ref[rows, :] = y
    yb = y.astype(bf16)
    hb_ref[rows, :] = yb
    acc = jnp.dot(yb, w_ref[...], preferred_element_type=f32)
    o_ref[rows, :A_WIDTH] = (acc[:, :A_WIDTH] * (A_HEAD_DIM ** -0.5 * LOG2E)).astype(bf16)
    o_ref[rows, A_WIDTH:] = acc[:, A_WIDTH:].astype(bf16)


def _entry_first_kernel(x_ref, g_ref, b_ref, w_ref, hf_ref, hb_ref, o_ref):
    for r0 in range(0, ROW_TILE, ENTRY_SUB_ROWS):
        rows = slice(r0, r0 + ENTRY_SUB_ROWS)
        _project(_layer_norm(x_ref[rows, :], g_ref[...], b_ref[...]), rows, w_ref, hf_ref, hb_ref, o_ref)


def _entry_moe_kernel(ylo_ref, yhi_ref, gt_ref, h1_ref, g_ref, b_ref, w_ref, hf_ref, hb_ref, o_ref):
    for r0 in range(0, ROW_TILE, ENTRY_SUB_ROWS):
        rows = slice(r0, r0 + ENTRY_SUB_ROWS)
        z = DN_ALPHA * h1_ref[rows, :] + _moe_sum(ylo_ref, yhi_ref, gt_ref, rows)
        _project(_layer_norm(z, g_ref[...], b_ref[...]), rows, w_ref, hf_ref, hb_ref, o_ref)


def _entry_call(body, n, w, row_inputs, row_specs, g, b, name):
    d = D_MODEL
    c = w.shape[1]
    t = ROW_TILE
    vec = pl.BlockSpec((1, d), lambda i: (0, 0))
    rowb = lambda cc: pl.BlockSpec((t, cc), lambda i: (i, 0))
    return pl.pallas_call(
        body, grid=(n // t,),
        in_specs=row_specs + [vec, vec, pl.BlockSpec((d, c), lambda i: (0, 0))],
        out_specs=[rowb(d), rowb(d), rowb(c)],
        out_shape=[jax.ShapeDtypeStruct((n, d), f32), jax.ShapeDtypeStruct((n, d), bf16),
                   jax.ShapeDtypeStruct((n, c), bf16)],
        compiler_params=_cparams("parallel"), name=name,
    )(*row_inputs, g.reshape(1, d), b.reshape(1, d), w)


def entry_first(x2d, g, b, w):
    n, d = x2d.shape
    return _entry_call(_entry_first_kernel, n, w, [x2d], [pl.BlockSpec((ROW_TILE, d), lambda i: (i, 0))], g, b,
                       "entry_first")


def entry_moe(yk_lo, yk_hi, gates_t, h1f, g, b, w):
    n, d = h1f.shape
    t = ROW_TILE
    ysp = pl.BlockSpec((TOP_K, t, QUART), lambda i: (0, i, 0))
    specs = [ysp, ysp, pl.BlockSpec((t, TOP_K), lambda i: (i, 0)), pl.BlockSpec((t, d), lambda i: (i, 0))]
    return _entry_call(_entry_moe_kernel, n, w, [yk_lo, yk_hi, gates_t, h1f], specs, g, b, "entry_moe")


def _mla_prep_kernel(hb_ref, wdown_ref, gq_ref, gkv_ref, wuq_ref, wukv_ref, cos_ref, sin_ref, qc_ref, kc_ref, vc_ref):
    cosm = cos_ref[...]
    sinm = sin_ref[...]
    lane = lax.broadcasted_iota(i32, cosm.shape, 1)
    half = C_ROPE // 2
    scale = C_QK ** -0.5 * LOG2E

    def rope(x):
        swapped = jnp.where(lane < half, pltpu.roll(x, LANES - half, 1), pltpu.roll(x, half, 1))
        return x * cosm + swapped * sinm

    def rms(x, g):
        return (x * lax.rsqrt(jnp.mean(x * x, axis=-1, keepdims=True) + RMS_EPS) * g).astype(bf16)

    down = jnp.dot(hb_ref[...], wdown_ref[...], preferred_element_type=f32)
    kv_lo = C_Q_RANK
    kr_lo = C_Q_RANK + C_KV_RANK
    q = jnp.dot(rms(down[:, :kv_lo], gq_ref[...]), wuq_ref[...], preferred_element_type=f32)
    rope_lo = C_HEADS * C_NOPE
    heads_per_vreg = LANES // C_ROPE
    for h in range(C_HEADS):
        lo = h * C_SLOT
        pair = q[:, (h // 2) * LANES:(h // 2 + 1) * LANES]
        mine = (lane >= C_NOPE) if h % 2 else (lane < C_NOPE)
        qc_ref[:, lo:lo + LANES] = (jnp.where(mine, pair, 0.0) * scale).astype(bf16)
        group = q[:, rope_lo + (h // heads_per_vreg) * LANES:rope_lo + (h // heads_per_vreg + 1) * LANES]
        shift = (h % heads_per_vreg) * C_ROPE
        mine = pltpu.roll(group, LANES - shift, 1) if shift else group
        qc_ref[:, lo + LANES:lo + C_SLOT] = (rope(mine) * scale).astype(bf16)
    kv = jnp.dot(rms(down[:, kv_lo:kr_lo], gkv_ref[...]), wukv_ref[...], preferred_element_type=f32)
    kr = rope(down[:, kr_lo:]).astype(bf16)
    for j in range(C_HEADS // 2):
        lo = j * C_SLOT
        kc_ref[:, lo:lo + LANES] = kv[:, j * LANES:(j + 1) * LANES].astype(bf16)
        kc_ref[:, lo + LANES:lo + C_SLOT] = kr
    vc_ref[...] = kv[:, C_HEADS * C_NOPE:].astype(bf16)


def mla_prep(hb, wdown, gq, gkv, wuq, wukv, cosm, sinm):
    n, d = hb.shape
    t = ROW_TILE
    full = lambda a: pl.BlockSpec(a.shape, lambda i: (0,) * a.ndim)
    rowb = lambda c: pl.BlockSpec((t, c), lambda i: (i, 0))
    qw, kw, vw = C_HEADS * C_SLOT, (C_HEADS // 2) * C_SLOT, C_HEADS * C_V
    return pl.pallas_call(
        _mla_prep_kernel, grid=(n // t,),
        in_specs=[rowb(d), full(wdown), full(gq), full(gkv), full(wuq), full(wukv), rowb(LANES), rowb(LANES)],
        out_specs=[rowb(qw), rowb(kw), rowb(vw)],
        out_shape=[jax.ShapeDtypeStruct((n, qw), bf16), jax.ShapeDtypeStruct((n, kw), bf16),
                   jax.ShapeDtypeStruct((n, vw), bf16)],
        compiler_params=_cparams("parallel"), name="mla_prep",
    )(hb, wdown, gq, gkv, wuq, wukv, cosm, sinm)


def _conv_rows(base, hs, sh, w_ref, cb_ref, lg_ref, lb_ref, o_ref):
    t = CONV_BLK
    first = CONV_HALO - CONV_WIDTH + 1
    a_of = [[a for a in range(CONV_HALO // 8 + 1) if first <= 8 * a + b <= CONV_HALO] for b in range(8)]
    for b in range(1, 8):
        length = 8 * max(a_of[b]) + t
        sh[b - 1, 0:length, :] = hs[base + b:base + b + length, :]
    rows = 64
    for r0 in range(0, t, rows):
        acc = jnp.zeros((rows, CONV_CH), f32) + cb_ref[...]
        for b in range(8):
            for a in a_of[b]:
                tap = 8 * a + b - first
                lo = r0 + 8 * a
                src = hs[base + lo:base + lo + rows, :] if b == 0 else sh[b - 1, lo:lo + rows, :]
                acc = acc + src * w_ref[tap:tap + 1, :]
        y = _layer_norm(acc, lg_ref[...], lb_ref[...])
        o_ref[base + r0:base + r0 + rows, :] = (y * _sigmoid(y)).astype(bf16)


def _conv_kernel(a_ref, g_ref, ap_ref, gp_ref, w_ref, cb_ref, lg_ref, lb_ref, o_ref, hs, sh):
    i = pl.program_id(1)
    prev = ap_ref[...].astype(f32) * _sigmoid(gp_ref[...].astype(f32))
    hs[0:CONV_HALO, :] = jnp.where(i > 0, prev, 0.0)
    hs[CONV_HALO:CONV_HALO + ROW_TILE, :] = a_ref[...].astype(f32) * _sigmoid(g_ref[...].astype(f32))
    for base in range(0, ROW_TILE, CONV_BLK):
        _conv_rows(base, hs, sh, w_ref, cb_ref, lg_ref, lb_ref, o_ref)


def conv_module(qkvg, w_dw, b_dw, ln_g, ln_b, batch, seq):
    n = batch * seq
    t = ROW_TILE
    nb = seq // t
    halo_per_tile = t // CONV_HALO
    a_col = 3 * A_WIDTH // CONV_CH
    cur = lambda col: pl.BlockSpec((t, CONV_CH), lambda b, i: (b * nb + i, col))
    halo = lambda col: pl.BlockSpec(
        (CONV_HALO, CONV_CH), lambda b, i: (jnp.maximum((b * nb + i) * halo_per_tile - 1, 0), col))
    vec = pl.BlockSpec((1, CONV_CH), lambda b, i: (0, 0))
    return pl.pallas_call(
        _conv_kernel, grid=(batch, nb),
        in_specs=[cur(a_col), cur(a_col + 1), halo(a_col), halo(a_col + 1),
                  pl.BlockSpec((CONV_WIDTH, CONV_CH), lambda b, i: (0, 0)), vec, vec, vec],
        out_specs=pl.BlockSpec((t, CONV_CH), lambda b, i: (b * nb + i, 0)),
        out_shape=jax.ShapeDtypeStruct((n, CONV_CH), bf16),
        scratch_shapes=[pltpu.VMEM((CONV_HALO + t, CONV_CH), f32),
                        pltpu.VMEM((7, CONV_HALO + CONV_BLK, CONV_CH), f32)],
        compiler_params=_cparams("parallel", "parallel"), name="conv_module",
    )(qkvg, qkvg, qkvg, qkvg, w_dw, b_dw.reshape(1, -1), ln_g.reshape(1, -1), ln_b.reshape(1, -1))


def _attn_a_kernel(q_ref, k_ref, v_ref, bias_ref, o_ref, kpad, vpad):
    qi = pl.program_id(1)
    seq = k_ref.shape[0]

    @pl.when(qi == 0)
    def _():
        kpad[0:A_LEFT, :] = jnp.zeros((A_LEFT, A_WIDTH), bf16)
        vpad[0:A_LEFT, :] = jnp.zeros((A_LEFT, A_WIDTH), bf16)
        kpad[A_LEFT:A_LEFT + seq, :] = k_ref[...]
        vpad[A_LEFT:A_LEFT + seq, :] = v_ref[...]

    col = lax.broadcasted_iota(i32, (2 * A_QBLK, A_BAND), 1)
    lane = lax.broadcasted_iota(i32, (A_QBLK, LANES), 1)
    ones = jnp.ones((A_BAND, LANES), bf16)
    for sub in range(A_STEP // A_QBLK):
        start = pl.multiple_of(qi * A_STEP + sub * A_QBLK, A_QBLK)
        before_start = jnp.where(col + start >= A_LEFT, 0.0, NEG_INF)
        rows = slice(sub * A_QBLK, (sub + 1) * A_QBLK)
        for j in range(A_HEADS // 2):
            cs = slice(j * LANES, (j + 1) * LANES)
            qp = q_ref[rows, cs].astype(f32)
            qs = jnp.concatenate([jnp.where(lane < A_HEAD_DIM, qp, 0.0), jnp.where(lane >= A_HEAD_DIM, qp, 0.0)],
                                 axis=0).astype(bf16)
            kb = kpad[pl.ds(start, A_BAND), cs]
            vb = jnp.concatenate([vpad[pl.ds(start, A_BAND), cs], ones], axis=1)
            s = lax.dot_general(qs, kb, (((1,), (1,)), ((), ())), preferred_element_type=f32)
            s = s + bias_ref[j] + before_start
            p = jnp.exp2((s - jnp.max(s, axis=-1, keepdims=True)).astype(bf16))
            o = jnp.dot(p, vb, preferred_element_type=f32)
            o = o[:, :LANES] / o[:, LANES:]
            o_ref[rows, cs] = jnp.where(lane < A_HEAD_DIM, o[:A_QBLK], o[A_QBLK:]).astype(bf16)


def attn_a(qkvg, bias, batch, seq):
    n = batch * seq
    nq = seq // A_STEP
    return pl.pallas_call(
        _attn_a_kernel, grid=(batch, nq),
        in_specs=[pl.BlockSpec((A_STEP, A_WIDTH), lambda b, i: (b * nq + i, 0)),
                  pl.BlockSpec((seq, A_WIDTH), lambda b, i: (b, 1)),
                  pl.BlockSpec((seq, A_WIDTH), lambda b, i: (b, 2)),
                  pl.BlockSpec(bias.shape, lambda b, i: (0, 0, 0))],
        out_specs=pl.BlockSpec((A_STEP, A_WIDTH), lambda b, i: (b * nq + i, 0)),
        out_shape=jax.ShapeDtypeStruct((n, A_WIDTH), bf16),
        scratch_shapes=[pltpu.VMEM((A_LEFT + seq, A_WIDTH), bf16), pltpu.VMEM((A_LEFT + seq, A_WIDTH), bf16)],
        compiler_params=_cparams("parallel", "arbitrary"), name="attn_a",
    )(qkvg, qkvg, qkvg, bias)


def _mla_pair_kernel(q_ref, k_ref, v_ref, o_ref):
    t = C_BLK
    seq = k_ref.shape[0]
    row = lax.broadcasted_iota(i32, (2 * t, t), 0)
    col = lax.broadcasted_iota(i32, (2 * t, t), 1)
    diag_ok = (col // CHUNK) <= ((row % t) // CHUNK)
    lane = lax.broadcasted_iota(i32, (t, LANES), 1)
    nt = lambda a, b: lax.dot_general(a, b, (((1,), (1,)), ((), ())), preferred_element_type=f32)
    for jj in range(C_STEP_PAIRS):
        kcols = slice(jj * C_SLOT, (jj + 1) * C_SLOT)
        vcols = slice(jj * LANES, (jj + 1) * LANES)
        for r in range(seq // t):
            lo = r * t
            vext = jnp.concatenate([v_ref[0:lo + t, vcols], jnp.ones((lo + t, LANES), bf16)], axis=1)
            q = jnp.concatenate([q_ref[lo:lo + t, (2 * jj) * C_SLOT:(2 * jj + 1) * C_SLOT],
                                 q_ref[lo:lo + t, (2 * jj + 1) * C_SLOT:(2 * jj + 2) * C_SLOT]], axis=0)
            s_diag = jnp.where(diag_ok, nt(q, k_ref[lo:lo + t, kcols]), NEG_INF)
            m = jnp.max(s_diag, axis=-1, keepdims=True)
            if r > 0:
                s_low = nt(q, k_ref[0:lo, kcols])
                m = jnp.maximum(m, jnp.max(s_low, axis=-1, keepdims=True))
                p = jnp.concatenate([jnp.exp2((s_low - m).astype(bf16)), jnp.exp2((s_diag - m).astype(bf16))],
                                    axis=1)
            else:
                p = jnp.exp2((s_diag - m).astype(bf16))
            o = jnp.dot(p, vext, preferred_element_type=f32)
            o = o[:, :LANES] / o[:, LANES:]
            o_ref[lo:lo + t, vcols] = jnp.where(lane < C_V, o[:t], o[t:]).astype(bf16)


def mla_attn_pairs(qc, kc, vc, batch, seq):
    n = batch * seq
    steps = C_HEADS // 2 // C_STEP_PAIRS
    w = C_STEP_PAIRS
    return pl.pallas_call(
        _mla_pair_kernel, grid=(batch, steps),
        in_specs=[pl.BlockSpec((seq, 2 * w * C_SLOT), lambda b, j: (b, j)),
                  pl.BlockSpec((seq, w * C_SLOT), lambda b, j: (b, j)),
                  pl.BlockSpec((seq, w * LANES), lambda b, j: (b, j))],
        out_specs=pl.BlockSpec((seq, w * LANES), lambda b, j: (b, j)),
        out_shape=jax.ShapeDtypeStruct((n, C_HEADS * C_V), bf16),
        compiler_params=_cparams("parallel", "parallel"), name="mla_attn",
    )(qc, kc, vc)


def _mix_kernel(ya_ref, cb_ref, yc_ref, hb_ref, hf_ref, woa_ref, wpw_ref, woc_ref, wg_ref, bg_ref, wout_ref,
                g_ref, b_ref, of_ref, lo_ref, hi_ref):
    d = D_MODEL
    for r0 in range(0, MIX_TILE, SUB_ROWS):
        rows = slice(r0, r0 + SUB_ROWS)
        gates = _sigmoid(jnp.dot(hb_ref[rows, :], wg_ref[...], preferred_element_type=f32) + bg_ref[...])
        mix = (gates[:, 0:d] * jnp.dot(ya_ref[rows, :], woa_ref[...], preferred_element_type=f32)
               + gates[:, d:2 * d] * jnp.dot(cb_ref[rows, :], wpw_ref[...], preferred_element_type=f32)
               + gates[:, 2 * d:3 * d] * jnp.dot(yc_ref[rows, :], woc_ref[...], preferred_element_type=f32))
        z = DN_ALPHA * hf_ref[rows, :] + jnp.dot(mix.astype(bf16), wout_ref[...], preferred_element_type=f32)
        y = _layer_norm(z, g_ref[...], b_ref[...])
        of_ref[rows, :] = y
        lo, hi = _pack_row(y)
        lo_ref[rows, :] = lo
        hi_ref[rows, :] = hi


def mix_layer(ya, cb, yc, hb, hf, woa, wpw, woc, wg, bg, wout, g, b):
    n, d = hf.shape
    t = MIX_TILE
    full = lambda a: pl.BlockSpec(a.shape, lambda i: (0,) * a.ndim)
    rowb = lambda c: pl.BlockSpec((t, c), lambda i: (i, 0))
    return pl.pallas_call(
        _mix_kernel, grid=(n // t,),
        in_specs=[rowb(ya.shape[1]), rowb(cb.shape[1]), rowb(yc.shape[1]), rowb(d), rowb(d),
                  full(woa), full(wpw), full(woc), full(wg), full(bg), full(wout), full(g), full(b)],
        out_specs=[rowb(d), rowb(QUART), rowb(QUART)],
        out_shape=[jax.ShapeDtypeStruct((n, d), f32), jax.ShapeDtypeStruct((n, QUART), i32),
                   jax.ShapeDtypeStruct((n, QUART), i32)],
        compiler_params=_cparams("parallel"), name="mix_layer",
    )(ya, cb, yc, hb, hf, woa, wpw, woc, wg, bg, wout, g, b)


def _router_kernel(h_ref, wr_ref, br_ref, idx_ref, gate_ref, rank_ref, cnt_ref, base):
    i = pl.program_id(0)
    t = ROUTER_TILE
    e = N_EXPERTS

    @pl.when(i == 0)
    def _():
        base[...] = jnp.zeros_like(base)

    def split(x):
        hi = x.astype(bf16)
        return hi, (x - hi.astype(f32)).astype(bf16)

    nt = lambda a, b: lax.dot_general(a, b, (((1,), (1,)), ((), ())), preferred_element_type=f32)
    w_hi, w_lo = split(wr_ref[...])
    h_hi, h_lo = split(h_ref[...])
    logits = nt(w_hi, h_hi) + nt(w_lo, h_hi) + nt(w_hi, h_lo) + br_ref[...]
    row = lax.broadcasted_iota(i32, (e, t), 0).astype(f32)
    vals, hots = [], []
    cur = logits
    for k in range(TOP_K):
        m = jnp.max(cur, axis=0, keepdims=True)
        first = jnp.min(jnp.where(cur == m, row, float(e)), axis=0, keepdims=True)
        hot = row == first
        cur = jnp.where(hot, -jnp.inf, cur)
        vals.append(m)
        hots.append(hot)
        idx_ref[k:k + 1, :] = first.astype(i32)
    ex = [jnp.exp(v - vals[0]) for v in vals]
    den = ex[0] + ex[1] + ex[2] + ex[3]
    for k in range(TOP_K):
        gate_ref[k:k + 1, :] = ex[k] / den
    onehot = jnp.concatenate([jnp.where(h, 1.0, 0.0) for h in hots], axis=0)
    r = lax.broadcasted_iota(i32, (t, t), 0)
    c = lax.broadcasted_iota(i32, (t, t), 1)
    upper = jnp.where(r <= c, 1.0, 0.0).astype(bf16)
    prefix = jnp.dot(onehot.astype(bf16), upper, preferred_element_type=f32)
    counts = jnp.sum(onehot, axis=1, keepdims=True)
    offset = base[:, 0:1]
    for k in range(TOP_K):
        sel = jnp.where(hots[k], prefix[k * e:(k + 1) * e, :] - 1.0 + offset, 0.0)
        rank_ref[k:k + 1, :] = jnp.sum(sel, axis=0, keepdims=True).astype(i32)
        offset = offset + counts[k * e:(k + 1) * e, :]
    base[...] = jnp.broadcast_to(offset, base.shape)
    cnt_ref[...] = base[...]


def router(hf, w_rt, b_r):
    n, d = hf.shape
    t = ROUTER_TILE
    tok = pl.BlockSpec((TOP_K, t), lambda i: (0, i))
    return pl.pallas_call(
        _router_kernel, grid=(n // t,),
        in_specs=[pl.BlockSpec((t, d), lambda i: (i, 0)), pl.BlockSpec((N_EXPERTS, d), lambda i: (0, 0)),
                  pl.BlockSpec((N_EXPERTS, 1), lambda i: (0, 0))],
        out_specs=[tok, tok, tok, pl.BlockSpec((N_EXPERTS, LANES), lambda i: (0, 0))],
        out_shape=[jax.ShapeDtypeStruct((TOP_K, n), i32), jax.ShapeDtypeStruct((TOP_K, n), f32),
                   jax.ShapeDtypeStruct((TOP_K, n), i32), jax.ShapeDtypeStruct((N_EXPERTS, LANES), f32)],
        scratch_shapes=[pltpu.VMEM((N_EXPERTS, LANES), f32)],
        compiler_params=_cparams("arbitrary"), name="router",
    )(hf, w_rt, b_r)


def _dest_kernel(idx_ref, rank_ref, start_ref, dest_ref):
    t = idx_ref.shape[1]
    row = lax.broadcasted_iota(i32, (N_EXPERTS, t), 0)
    for k in range(TOP_K):
        hot = row == idx_ref[k:k + 1, :]
        off = jnp.sum(jnp.where(hot, start_ref[...], 0.0), axis=0, keepdims=True)
        dest_ref[k:k + 1, :] = rank_ref[k:k + 1, :] + off.astype(i32)


def dest_rows(idx, rank, pad_start):
    n = idx.shape[1]
    t = ROUTER_TILE
    tok = pl.BlockSpec((TOP_K, t), lambda i: (0, i))
    return pl.pallas_call(
        _dest_kernel, grid=(n // t,),
        in_specs=[tok, tok, pl.BlockSpec((N_EXPERTS, 1), lambda i: (0, 0))],
        out_specs=tok, out_shape=jax.ShapeDtypeStruct((TOP_K, n), i32),
        compiler_params=_cparams("parallel"), name="dest_rows",
    )(idx, rank, pad_start)


def _sc_mesh():
    return plsc.VectorSubcoreMesh(core_axis_name="c", subcore_axis_name="s")


def sc_scatter_rows(x_lo, x_hi, dest, n_rows):
    n, d = x_lo.shape
    kk = dest.shape[0]
    out = jax.ShapeDtypeStruct((n_rows, d), x_lo.dtype)

    @functools.partial(pl.kernel, out_type=(out, out), mesh=_sc_mesh(), scratch_types=[])
    def k(lo_hbm, hi_hbm, i_hbm, olo_hbm, ohi_hbm):
        for x_hbm, o_hbm in ((lo_hbm, olo_hbm), (hi_hbm, ohi_hbm)):
            def body(x_vmem, i_vmem, o_hbm=o_hbm):
                for j in range(kk):
                    pltpu.sync_copy(x_vmem, o_hbm.at[i_vmem.at[j]])

            pltpu.emit_pipeline(
                body, grid=(n // SC_WINDOW,),
                in_specs=[pl.BlockSpec((SC_WINDOW, d), lambda i: (i, 0)),
                          pl.BlockSpec((kk, SC_WINDOW), lambda i: (0, i))],
                out_specs=[], core_axis_name=("c", "s"), dimension_semantics=(pltpu.PARALLEL,),
            )(x_hbm, i_hbm)

    return k(x_lo, x_hi, dest)


def sc_gather_rows(t_lo, t_hi, idx):
    m = idx.shape[1]
    d = t_lo.shape[1]
    out = jax.ShapeDtypeStruct((m, d), t_lo.dtype)

    @functools.partial(pl.kernel, out_type=(out, out), mesh=_sc_mesh(), scratch_types=[])
    def k(lo_hbm, hi_hbm, i_hbm, olo_hbm, ohi_hbm):
        for t_hbm, o_hbm in ((lo_hbm, olo_hbm), (hi_hbm, ohi_hbm)):
            def body(i_vmem, o_vmem, t_hbm=t_hbm):
                pltpu.sync_copy(t_hbm.at[i_vmem.at[0]], o_vmem)

            pltpu.emit_pipeline(
                body, grid=(m // SC_WINDOW,),
                in_specs=[pl.BlockSpec((1, SC_WINDOW), lambda i: (0, i))],
                out_specs=[pl.BlockSpec((SC_WINDOW, d), lambda i: (i, 0))],
                core_axis_name=("c", "s"), dimension_semantics=(pltpu.PARALLEL,),
            )(i_hbm, o_hbm)

    return k(t_lo, t_hi, idx)


def _expert_kernel(be_ref, nu_ref, xlo_ref, xhi_ref, w1_ref, b1_ref, w2_ref, b2_ref, ylo_ref, yhi_ref, w1b, w2b):
    i = pl.program_id(0)
    new_expert = jnp.logical_or(i == 0, be_ref[i] != be_ref[jnp.maximum(i - 1, 0)])

    @pl.when(jnp.logical_and(i < nu_ref[0], new_expert))
    def _():
        w1b[...] = w1_ref[0, 0].astype(bf16)
        w2b[...] = w2_ref[0, 0].astype(bf16)

    @pl.when(i < nu_ref[0])
    def _():
        for r0 in range(0, MOE_BLK, SUB_ROWS):
            rows = slice(r0, r0 + SUB_ROWS)
            lo = xlo_ref[rows, :]
            hi = xhi_ref[rows, :]
            x = jnp.concatenate([_unpack_lo(lo).astype(bf16), _unpack_lo(hi).astype(bf16),
                                 _unpack_hi(lo).astype(bf16), _unpack_hi(hi).astype(bf16)], axis=1)
            u = jnp.dot(x, w1b[...], preferred_element_type=f32) + b1_ref[0, 0]
            glu = jnp.minimum(u[:, :D_EXPERT], SWIGLU_LIMIT)
            lin = jnp.clip(u[:, D_EXPERT:], -SWIGLU_LIMIT, SWIGLU_LIMIT)
            act = (glu * _sigmoid(SWIGLU_ALPHA * glu) * (lin + 1.0)).astype(bf16)
            y = jnp.dot(act, w2b[...], preferred_element_type=f32) + b2_ref[0, 0]
            ylo, yhi = _pack_row(y)
            ylo_ref[rows, :] = ylo
            yhi_ref[rows, :] = yhi


def expert_ffn(layer, block_expert, n_used, xs_lo, xs_hi, w1, b1, w2, b2):
    n_rows = xs_lo.shape[0]
    nb = n_rows // MOE_BLK
    rows = pl.BlockSpec((MOE_BLK, QUART), lambda i, be, nu: (jnp.minimum(i, nu[0] - 1), 0))
    wsel = lambda shape: pl.BlockSpec((1, 1) + shape, lambda i, be, nu: (layer, be[i], 0, 0))
    out = jax.ShapeDtypeStruct((n_rows, QUART), i32)
    return pl.pallas_call(
        _expert_kernel,
        grid_spec=pltpu.PrefetchScalarGridSpec(
            num_scalar_prefetch=2, grid=(nb,),
            in_specs=[rows, rows, wsel((D_MODEL, 2 * D_EXPERT)), wsel((1, 2 * D_EXPERT)),
                      wsel((D_EXPERT, D_MODEL)), wsel((1, D_MODEL))],
            out_specs=[rows, rows],
            scratch_shapes=[pltpu.VMEM((D_MODEL, 2 * D_EXPERT), bf16), pltpu.VMEM((D_EXPERT, D_MODEL), bf16)]),
        out_shape=[out, out],
        compiler_params=_cparams("arbitrary"), name="expert_ffn",
    )(block_expert, n_used, xs_lo, xs_hi, w1, b1, w2, b2)


def _combine_kernel(ylo_ref, yhi_ref, gt_ref, h1_ref, g_ref, b_ref, of_ref):
    z = DN_ALPHA * h1_ref[...] + _moe_sum(ylo_ref, yhi_ref, gt_ref, slice(None))
    of_ref[...] = _layer_norm(z, g_ref[...], b_ref[...])


def combine(yk_lo, yk_hi, gates_t, h1f, g, b):
    n, d = h1f.shape
    t = ROW_TILE
    ysp = pl.BlockSpec((TOP_K, t, QUART), lambda i: (0, i, 0))
    rowb = lambda c: pl.BlockSpec((t, c), lambda i: (i, 0))
    vec = pl.BlockSpec((1, d), lambda i: (0, 0))
    return pl.pallas_call(
        _combine_kernel, grid=(n // t,),
        in_specs=[ysp, ysp, rowb(TOP_K), rowb(d), vec, vec],
        out_specs=rowb(d), out_shape=jax.ShapeDtypeStruct((n, d), f32),
        compiler_params=_cparams("parallel"), name="combine",
    )(yk_lo, yk_hi, gates_t, h1f, g.reshape(1, d), b.reshape(1, d))


def _rope_tables(positions):
    inv = ROPE_THETA ** (-jnp.arange(0, C_ROPE, 2, dtype=f32) / C_ROPE)
    ang = positions.reshape(-1).astype(f32)[:, None] * inv
    cos, sin = jnp.cos(ang), jnp.sin(ang)
    pad = jnp.zeros((ang.shape[0], LANES - C_ROPE), f32)
    return jnp.concatenate([cos, cos, pad], axis=1), jnp.concatenate([-sin, sin, pad], axis=1)


def _attn_a_bias(rel_tables):
    depth = rel_tables.shape[0]
    period = A_BAND + A_QBLK - 1
    u = jnp.arange(period)
    diag = jnp.clip(u - (A_QBLK - 1) - A_LEFT, -A_MAX_REL, A_MAX_REL) + A_MAX_REL
    e = rel_tables.astype(f32)[:, :, diag]
    pitch = period + 1
    tiled = jnp.tile(e, (1, 1, A_QBLK + 1))[:, :, :A_QBLK * pitch]
    rows = tiled.reshape(depth, A_HEADS, A_QBLK, pitch)[:, :, :, :A_BAND]
    bias = rows[:, :, ::-1, :]
    r = jnp.arange(A_QBLK)[:, None]
    c = jnp.arange(A_BAND)[None, :]
    own = c - CHUNK * (r // CHUNK)
    valid = (own >= 0) & (own < A_LEFT + CHUNK)
    bias = jnp.where(valid[None, None], bias * LOG2E, NEG_INF)
    return bias.reshape(depth, A_HEADS // 2, 2 * A_QBLK, A_BAND)


def _layout_w_uq(w_uq):
    w = w_uq.reshape(C_Q_RANK, C_HEADS, C_QK)
    return jnp.concatenate([w[:, :, :C_NOPE].reshape(C_Q_RANK, -1), w[:, :, C_NOPE:].reshape(C_Q_RANK, -1)], axis=1)


def _layout_w_ukv(w_ukv):
    w = w_ukv.reshape(C_KV_RANK, C_HEADS, C_NOPE + C_V)
    return jnp.concatenate([w[:, :, :C_NOPE].reshape(C_KV_RANK, -1), w[:, :, C_NOPE:].reshape(C_KV_RANK, -1)], axis=1)


def _moe(layer, hf, lo, hi, w_rt, b_r, w1, b1, w2, b2):
    n = hf.shape[0]
    n_rows = n * TOP_K + N_EXPERTS * MOE_BLK
    nb = n_rows // MOE_BLK
    idx, gates, rank, cnt = router(hf, w_rt, b_r)
    counts = cnt[:, 0].astype(i32)
    padded = (counts + MOE_BLK - 1) // MOE_BLK * MOE_BLK
    pad_end = jnp.cumsum(padded)
    pad_start = (pad_end - padded).astype(f32).reshape(N_EXPERTS, 1)
    block_start = jnp.arange(nb, dtype=i32) * MOE_BLK
    block_expert = jnp.minimum(jnp.sum((pad_end[None, :] <= block_start[:, None]).astype(i32), axis=1),
                               N_EXPERTS - 1)
    n_used = (pad_end[-1:] // MOE_BLK).astype(i32)
    dest = dest_rows(idx, rank, pad_start)
    xs_lo, xs_hi = sc_scatter_rows(lo, hi, dest, n_rows)
    ys_lo, ys_hi = expert_ffn(layer, block_expert, n_used, xs_lo, xs_hi, w1, b1, w2, b2)
    yk_lo, yk_hi = sc_gather_rows(ys_lo, ys_hi, dest.reshape(1, TOP_K * n))
    return yk_lo.reshape(TOP_K, n, QUART), yk_hi.reshape(TOP_K, n, QUART), gates.T


def kernel(x, positions, ln_in_g, ln_in_b, w_in, w_gate, b_gate, rel_bias, conv_w, conv_b, conv_ln_g, conv_ln_b, w_pw2, q_norm_g, kv_norm_g, w_uq, w_ukv, w_oa, w_oc, w_out, ln1_g, ln1_b, w_router, b_router, w1, b1, w2, b2, ln2_g, ln2_b):
    batch, seq, d = x.shape
    n = batch * seq
    cosm, sinm = _rope_tables(positions)
    a_bias = _attn_a_bias(rel_bias)
    ab = 3 * A_WIDTH + 2 * CONV_CH
    row1 = lambda v: v.reshape(1, -1)
    b1r = b1.reshape(DEPTH, N_EXPERTS, 1, -1)
    b2r = b2.reshape(DEPTH, N_EXPERTS, 1, -1)
    moe_out = None
    for l in range(DEPTH):
        w_ab = w_in[l, :, :ab].astype(bf16)
        w_down = jnp.pad(w_in[l, :, ab:], ((0, 0), (0, LANES - C_ROPE))).astype(bf16)
        if l == 0:
            hf, hb, qkvg = entry_first(x.reshape(n, d), ln_in_g, ln_in_b, w_ab)
        else:
            hf, hb, qkvg = entry_moe(*moe_out, h1f, ln2_g[l - 1], ln2_b[l - 1], w_ab)
        qc, kc, vc = mla_prep(hb, w_down, row1(q_norm_g[l]), row1(kv_norm_g[l]),
                              _layout_w_uq(w_uq[l]).astype(bf16), _layout_w_ukv(w_ukv[l]).astype(bf16), cosm, sinm)
        cb = conv_module(qkvg, conv_w[l], conv_b[l], conv_ln_g[l], conv_ln_b[l], batch, seq)
        ya = attn_a(qkvg, a_bias[l], batch, seq)
        yc = mla_attn_pairs(qc, kc, vc, batch, seq)
        h1f, lo, hi = mix_layer(ya, cb, yc, hb, hf, w_oa[l].astype(bf16), w_pw2[l].astype(bf16),
                                w_oc[l].astype(bf16), w_gate[l].astype(bf16), row1(b_gate[l]),
                                w_out[l].astype(bf16), row1(ln1_g[l]), row1(ln1_b[l]))
        moe_out = _moe(l, h1f, lo, hi, w_router[l].T, b_router[l].reshape(N_EXPERTS, 1), w1, b1r, w2, b2r)
    return combine(*moe_out, h1f, ln2_g[DEPTH - 1], ln2_b[DEPTH - 1]).reshape(batch, seq, d)
```

```python
import functools

import jax
import jax.numpy as jnp
from jax import lax
from jax.experimental import pallas as pl
from jax.experimental.pallas import tpu as pltpu
from jax.experimental.pallas import tpu_sc as plsc

f32 = jnp.float32
bf16 = jnp.bfloat16
i32 = jnp.int32

D_MODEL = 1024
DEPTH = 4
CHUNK = 64
A_HEADS = 8
A_HEAD_DIM = 64
A_WIDTH = A_HEADS * A_HEAD_DIM
A_LEFT = 8 * CHUNK
A_MAX_REL = 128
CONV_CH = 512
CONV_WIDTH = 31
C_HEADS = 8
C_NOPE = 64
C_ROPE = 32
C_V = 64
C_QK = C_NOPE + C_ROPE
C_Q_RANK = 384
C_KV_RANK = 256
ROPE_THETA = 10000.0
N_EXPERTS = 32
TOP_K = 4
D_EXPERT = 1024
SWIGLU_ALPHA = 1.702
SWIGLU_LIMIT = 7.0
DN_ALPHA = (2 * DEPTH) ** 0.25
LN_EPS = 1e-5
RMS_EPS = 1e-6
NEG_INF = -1e30
LOG2E = 1.4426950408889634

LANES = 128
A_QBLK = 2 * CHUNK
A_STEP = 8 * A_QBLK
A_BAND = A_LEFT + A_QBLK
C_BLK = 256
C_STEP_PAIRS = 4
C_SLOT = 256
CONV_BLK = 256
CONV_HALO = 32
MOE_BLK = 512
ROW_TILE = 512
MIX_TILE = 512
SUB_ROWS = 256
ENTRY_SUB_ROWS = 128
ROUTER_TILE = 512
HALF = D_MODEL // 2
QUART = D_MODEL // 4
SC_WINDOW = 128
VMEM_LIMIT = 56 * 1024 * 1024


def _cparams(*sem):
    return pltpu.CompilerParams(dimension_semantics=tuple(sem), vmem_limit_bytes=VMEM_LIMIT)


def _layer_norm(x, g, b):
    mu = jnp.mean(x, axis=-1, keepdims=True)
    xc = x - mu
    var = jnp.mean(xc * xc, axis=-1, keepdims=True)
    return xc * lax.rsqrt(var + LN_EPS) * g + b


def _sigmoid(x):
    return 1.0 / (1.0 + jnp.exp(-x))


def _pack2(a, b):
    ab = lax.bitcast_convert_type(a.astype(bf16).astype(f32), i32)
    bb = lax.bitcast_convert_type(b.astype(bf16).astype(f32), i32)
    return lax.shift_right_logical(ab, 16) | (bb & jnp.int32(-65536))


def _unpack_lo(w):
    return lax.bitcast_convert_type(lax.shift_left(w, 16), f32)


def _unpack_hi(w):
    return lax.bitcast_convert_type(w & jnp.int32(-65536), f32)


def _pack_row(y):
    return (_pack2(y[:, 0:QUART], y[:, 2 * QUART:3 * QUART]),
            _pack2(y[:, QUART:2 * QUART], y[:, 3 * QUART:4 * QUART]))


def _moe_sum(ylo_ref, yhi_ref, gt_ref, rows):
    gt = gt_ref[rows, :]
    parts = [None] * 4
    for k in range(TOP_K):
        gk = gt[:, k:k + 1]
        lo = ylo_ref[k, rows, :]
        hi = yhi_ref[k, rows, :]
        vals = (_unpack_lo(lo), _unpack_lo(hi), _unpack_hi(lo), _unpack_hi(hi))
        for p in range(4):
            parts[p] = gk * vals[p] if parts[p] is None else parts[p] + gk * vals[p]
    return jnp.concatenate(parts, axis=1)


def _project(y, rows, w_ref, hf_ref, hb_ref, o_ref):
    hf_ref[rows, :] = y
    yb = y.astype(bf16)
    hb_ref[rows, :] = yb
    acc = jnp.dot(yb, w_ref[...], preferred_element_type=f32)
    o_ref[rows, :A_WIDTH] = (acc[:, :A_WIDTH] * (A_HEAD_DIM ** -0.5 * LOG2E)).astype(bf16)
    o_ref[rows, A_WIDTH:] = acc[:, A_WIDTH:].astype(bf16)


def _entry_first_kernel(x_ref, g_ref, b_ref, w_ref, hf_ref, hb_ref, o_ref):
    for r0 in range(0, ROW_TILE, ENTRY_SUB_ROWS):
        rows = slice(r0, r0 + ENTRY_SUB_ROWS)
        _project(_layer_norm(x_ref[rows, :], g_ref[...], b_ref[...]), rows, w_ref, hf_ref, hb_ref, o_ref)


def _entry_moe_kernel(ylo_ref, yhi_ref, gt_ref, h1_ref, g_ref, b_ref, w_ref, hf_ref, hb_ref, o_ref):
    for r0 in range(0, ROW_TILE, ENTRY_SUB_ROWS):
        rows = slice(r0, r0 + ENTRY_SUB_ROWS)
        z = DN_ALPHA * h1_ref[rows, :] + _moe_sum(ylo_ref, yhi_ref, gt_ref, rows)
        _project(_layer_norm(z, g_ref[...], b_ref[...]), rows, w_ref, hf_ref, hb_ref, o_ref)


def _entry_call(body, n, w, row_inputs, row_specs, g, b, name):
    d = D_MODEL
    c = w.shape[1]
    t = ROW_TILE
    vec = pl.BlockSpec((1, d), lambda i: (0, 0))
    rowb = lambda cc: pl.BlockSpec((t, cc), lambda i: (i, 0))
    return pl.pallas_call(
        body, grid=(n // t,),
        in_specs=row_specs + [vec, vec, pl.BlockSpec((d, c), lambda i: (0, 0))],
        out_specs=[rowb(d), rowb(d), rowb(c)],
        out_shape=[jax.ShapeDtypeStruct((n, d), f32), jax.ShapeDtypeStruct((n, d), bf16),
                   jax.ShapeDtypeStruct((n, c), bf16)],
        compiler_params=_cparams("parallel"), name=name,
    )(*row_inputs, g.reshape(1, d), b.reshape(1, d), w)


def entry_first(x2d, g, b, w):
    n, d = x2d.shape
    return _entry_call(_entry_first_kernel, n, w, [x2d], [pl.BlockSpec((ROW_TILE, d), lambda i: (i, 0))], g, b,
                       "entry_first")


def entry_moe(yk_lo, yk_hi, gates_t, h1f, g, b, w):
    n, d = h1f.shape
    t = ROW_TILE
    ysp = pl.BlockSpec((TOP_K, t, QUART), lambda i: (0, i, 0))
    specs = [ysp, ysp, pl.BlockSpec((t, TOP_K), lambda i: (i, 0)), pl.BlockSpec((t, d), lambda i: (i, 0))]
    return _entry_call(_entry_moe_kernel, n, w, [yk_lo, yk_hi, gates_t, h1f], specs, g, b, "entry_moe")


def _mla_prep_kernel(hb_ref, wdown_ref, gq_ref, gkv_ref, wuq_ref, wukv_ref, cos_ref, sin_ref, qc_ref, kc_ref, vc_ref):
    cosm = cos_ref[...]
    sinm = sin_ref[...]
    lane = lax.broadcasted_iota(i32, cosm.shape, 1)
    half = C_ROPE // 2
    scale = C_QK ** -0.5 * LOG2E

    def rope(x):
        swapped = jnp.where(lane < half, pltpu.roll(x, LANES - half, 1), pltpu.roll(x, half, 1))
        return x * cosm + swapped * sinm

    def rms(x, g):
        return (x * lax.rsqrt(jnp.mean(x * x, axis=-1, keepdims=True) + RMS_EPS) * g).astype(bf16)

    down = jnp.dot(hb_ref[...], wdown_ref[...], preferred_element_type=f32)
    kv_lo = C_Q_RANK
    kr_lo = C_Q_RANK + C_KV_RANK
    q = jnp.dot(rms(down[:, :kv_lo], gq_ref[...]), wuq_ref[...], preferred_element_type=f32)
    rope_lo = C_HEADS * C_NOPE
    heads_per_vreg = LANES // C_ROPE
    for h in range(C_HEADS):
        lo = h * C_SLOT
        pair = q[:, (h // 2) * LANES:(h // 2 + 1) * LANES]
        mine = (lane >= C_NOPE) if h % 2 else (lane < C_NOPE)
        qc_ref[:, lo:lo + LANES] = (jnp.where(mine, pair, 0.0) * scale).astype(bf16)
        group = q[:, rope_lo + (h // heads_per_vreg) * LANES:rope_lo + (h // heads_per_vreg + 1) * LANES]
        shift = (h % heads_per_vreg) * C_ROPE
        mine = pltpu.roll(group, LANES - shift, 1) if shift else group
        qc_ref[:, lo + LANES:lo + C_SLOT] = (rope(mine) * scale).astype(bf16)
    kv = jnp.dot(rms(down[:, kv_lo:kr_lo], gkv_ref[...]), wukv_ref[...], preferred_element_type=f32)
    kr = rope(down[:, kr_lo:]).astype(bf16)
    for j in range(C_HEADS // 2):
        lo = j * C_SLOT
        kc_ref[:, lo:lo + LANES] = kv[:, j * LANES:(j + 1) * LANES].astype(bf16)
        kc_ref[:, lo + LANES:lo + C_SLOT] = kr
    vc_ref[...] = kv[:, C_HEADS * C_NOPE:].astype(bf16)


def mla_prep(hb, wdown, gq, gkv, wuq, wukv, cosm, sinm):
    n, d = hb.shape
    t = ROW_TILE
    full = lambda a: pl.BlockSpec(a.shape, lambda i: (0,) * a.ndim)
    rowb = lambda c: pl.BlockSpec((t, c), lambda i: (i, 0))
    qw, kw, vw = C_HEADS * C_SLOT, (C_HEADS // 2) * C_SLOT, C_HEADS * C_V
    return pl.pallas_call(
        _mla_prep_kernel, grid=(n // t,),
        in_specs=[rowb(d), full(wdown), full(gq), full(gkv), full(wuq), full(wukv), rowb(LANES), rowb(LANES)],
        out_specs=[rowb(qw), rowb(kw), rowb(vw)],
        out_shape=[jax.ShapeDtypeStruct((n, qw), bf16), jax.ShapeDtypeStruct((n, kw), bf16),
                   jax.ShapeDtypeStruct((n, vw), bf16)],
        compiler_params=_cparams("parallel"), name="mla_prep",
    )(hb, wdown, gq, gkv, wuq, wukv, cosm, sinm)


def _conv_rows(base, hs, sh, w_ref, cb_ref, lg_ref, lb_ref, o_ref):
    t = CONV_BLK
    first = CONV_HALO - CONV_WIDTH + 1
    a_of = [[a for a in range(CONV_HALO // 8 + 1) if first <= 8 * a + b <= CONV_HALO] for b in range(8)]
    for b in range(1, 8):
        length = 8 * max(a_of[b]) + t
        sh[b - 1, 0:length, :] = hs[base + b:base + b + length, :]
    rows = 64
    for r0 in range(0, t, rows):
        acc = jnp.zeros((rows, CONV_CH), f32) + cb_ref[...]
        for b in range(8):
            for a in a_of[b]:
                tap = 8 * a + b - first
                lo = r0 + 8 * a
                src = hs[base + lo:base + lo + rows, :] if b == 0 else sh[b - 1, lo:lo + rows, :]
                acc = acc + src * w_ref[tap:tap + 1, :]
        y = _layer_norm(acc, lg_ref[...], lb_ref[...])
        o_ref[base + r0:base + r0 + rows, :] = (y * _sigmoid(y)).astype(bf16)


def _conv_kernel(a_ref, g_ref, ap_ref, gp_ref, w_ref, cb_ref, lg_ref, lb_ref, o_ref, hs, sh):
    i = pl.program_id(1)
    prev = ap_ref[...].astype(f32) * _sigmoid(gp_ref[...].astype(f32))
    hs[0:CONV_HALO, :] = jnp.where(i > 0, prev, 0.0)
    hs[CONV_HALO:CONV_HALO + ROW_TILE, :] = a_ref[...].astype(f32) * _sigmoid(g_ref[...].astype(f32))
    for base in range(0, ROW_TILE, CONV_BLK):
        _conv_rows(base, hs, sh, w_ref, cb_ref, lg_ref, lb_ref, o_ref)


def conv_module(qkvg, w_dw, b_dw, ln_g, ln_b, batch, seq):
    n = batch * seq
    t = ROW_TILE
    nb = seq // t
    halo_per_tile = t // CONV_HALO
    a_col = 3 * A_WIDTH // CONV_CH
    cur = lambda col: pl.BlockSpec((t, CONV_CH), lambda b, i: (b * nb + i, col))
    halo = lambda col: pl.BlockSpec(
        (CONV_HALO, CONV_CH), lambda b, i: (jnp.maximum((b * nb + i) * halo_per_tile - 1, 0), col))
    vec = pl.BlockSpec((1, CONV_CH), lambda b, i: (0, 0))
    return pl.pallas_call(
        _conv_kernel, grid=(batch, nb),
        in_specs=[cur(a_col), cur(a_col + 1), halo(a_col), halo(a_col + 1),
                  pl.BlockSpec((CONV_WIDTH, CONV_CH), lambda b, i: (0, 0)), vec, vec, vec],
        out_specs=pl.BlockSpec((t, CONV_CH), lambda b, i: (b * nb + i, 0)),
        out_shape=jax.ShapeDtypeStruct((n, CONV_CH), bf16),
        scratch_shapes=[pltpu.VMEM((CONV_HALO + t, CONV_CH), f32),
                        pltpu.VMEM((7, CONV_HALO + CONV_BLK, CONV_CH), f32)],
        compiler_params=_cparams("parallel", "parallel"), name="conv_module",
    )(qkvg, qkvg, qkvg, qkvg, w_dw, b_dw.reshape(1, -1), ln_g.reshape(1, -1), ln_b.reshape(1, -1))


def _attn_a_kernel(q_ref, k_ref, v_ref, bias_ref, o_ref, kpad, vpad):
    qi = pl.program_id(1)
    seq = k_ref.shape[0]

    @pl.when(qi == 0)
    def _():
        kpad[0:A_LEFT, :] = jnp.zeros((A_LEFT, A_WIDTH), bf16)
        vpad[0:A_LEFT, :] = jnp.zeros((A_LEFT, A_WIDTH), bf16)
        kpad[A_LEFT:A_LEFT + seq, :] = k_ref[...]
        vpad[A_LEFT:A_LEFT + seq, :] = v_ref[...]

    col = lax.broadcasted_iota(i32, (2 * A_QBLK, A_BAND), 1)
    lane = lax.broadcasted_iota(i32, (A_QBLK, LANES), 1)
    ones = jnp.ones((A_BAND, LANES), bf16)
    for sub in range(A_STEP // A_QBLK):
        start = pl.multiple_of(qi * A_STEP + sub * A_QBLK, A_QBLK)
        before_start = jnp.where(col + start >= A_LEFT, 0.0, NEG_INF)
        rows = slice(sub * A_QBLK, (sub + 1) * A_QBLK)
        for j in range(A_HEADS // 2):
            cs = slice(j * LANES, (j + 1) * LANES)
            qp = q_ref[rows, cs].astype(f32)
            qs = jnp.concatenate([jnp.where(lane < A_HEAD_DIM, qp, 0.0), jnp.where(lane >= A_HEAD_DIM, qp, 0.0)],
                                 axis=0).astype(bf16)
            kb = kpad[pl.ds(start, A_BAND), cs]
            vb = jnp.concatenate([vpad[pl.ds(start, A_BAND), cs], ones], axis=1)
            s = lax.dot_general(qs, kb, (((1,), (1,)), ((), ())), preferred_element_type=f32)
            s = s + bias_ref[j] + before_start
            p = jnp.exp2((s - jnp.max(s, axis=-1, keepdims=True)).astype(bf16))
            o = jnp.dot(p, vb, preferred_element_type=f32)
            o = o[:, :LANES] / o[:, LANES:]
            o_ref[rows, cs] = jnp.where(lane < A_HEAD_DIM, o[:A_QBLK], o[A_QBLK:]).astype(bf16)


def attn_a(qkvg, bias, batch, seq):
    n = batch * seq
    nq = seq // A_STEP
    return pl.pallas_call(
        _attn_a_kernel, grid=(batch, nq),
        in_specs=[pl.BlockSpec((A_STEP, A_WIDTH), lambda b, i: (b * nq + i, 0)),
                  pl.BlockSpec((seq, A_WIDTH), lambda b, i: (b, 1)),
                  pl.BlockSpec((seq, A_WIDTH), lambda b, i: (b, 2)),
                  pl.BlockSpec(bias.shape, lambda b, i: (0, 0, 0))],
        out_specs=pl.BlockSpec((A_STEP, A_WIDTH), lambda b, i: (b * nq + i, 0)),
        out_shape=jax.ShapeDtypeStruct((n, A_WIDTH), bf16),
        scratch_shapes=[pltpu.VMEM((A_LEFT + seq, A_WIDTH), bf16), pltpu.VMEM((A_LEFT + seq, A_WIDTH), bf16)],
        compiler_params=_cparams("parallel", "arbitrary"), name="attn_a",
    )(qkvg, qkvg, qkvg, bias)


def _mla_pair_kernel(q_ref, k_ref, v_ref, o_ref):
    t = C_BLK
    seq = k_ref.shape[0]
    row = lax.broadcasted_iota(i32, (2 * t, t), 0)
    col = lax.broadcasted_iota(i32, (2 * t, t), 1)
    diag_ok = (col // CHUNK) <= ((row % t) // CHUNK)
    lane = lax.broadcasted_iota(i32, (t, LANES), 1)
    nt = lambda a, b: lax.dot_general(a, b, (((1,), (1,)), ((), ())), preferred_element_type=f32)
    for jj in range(C_STEP_PAIRS):
        kcols = slice(jj * C_SLOT, (jj + 1) * C_SLOT)
        vcols = slice(jj * LANES, (jj + 1) * LANES)
        for r in range(seq // t):
            lo = r * t
            vext = jnp.concatenate([v_ref[0:lo + t, vcols], jnp.ones((lo + t, LANES), bf16)], axis=1)
            q = jnp.concatenate([q_ref[lo:lo + t, (2 * jj) * C_SLOT:(2 * jj + 1) * C_SLOT],
                                 q_ref[lo:lo + t, (2 * jj + 1) * C_SLOT:(2 * jj + 2) * C_SLOT]], axis=0)
            s_diag = jnp.where(diag_ok, nt(q, k_ref[lo:lo + t, kcols]), NEG_INF)
            m = jnp.max(s_diag, axis=-1, keepdims=True)
            if r > 0:
                s_low = nt(q, k_ref[0:lo, kcols])
                m = jnp.maximum(m, jnp.max(s_low, axis=-1, keepdims=True))
                p = jnp.concatenate([jnp.exp2((s_low - m).astype(bf16)), jnp.exp2((s_diag - m).astype(bf16))],
                                    axis=1)
            else:
                p = jnp.exp2((s_diag - m).astype(bf16))
            o = jnp.dot(p, vext, preferred_element_type=f32)
            o = o[:, :LANES] / o[:, LANES:]
            o_ref[lo:lo + t, vcols] = jnp.where(lane < C_V, o[:t], o[t:]).astype(bf16)


def mla_attn_pairs(qc, kc, vc, batch, seq):
    n = batch * seq
    steps = C_HEADS // 2 // C_STEP_PAIRS
    w = C_STEP_PAIRS
    return pl.pallas_call(
        _mla_pair_kernel, grid=(batch, steps),
        in_specs=[pl.BlockSpec((seq, 2 * w * C_SLOT), lambda b, j: (b, j)),
                  pl.BlockSpec((seq, w * C_SLOT), lambda b, j: (b, j)),
                  pl.BlockSpec((seq, w * LANES), lambda b, j: (b, j))],
        out_specs=pl.BlockSpec((seq, w * LANES), lambda b, j: (b, j)),
        out_shape=jax.ShapeDtypeStruct((n, C_HEADS * C_V), bf16),
        compiler_params=_cparams("parallel", "parallel"), name="mla_attn",
    )(qc, kc, vc)


def _mix_kernel(ya_ref, cb_ref, yc_ref, hb_ref, hf_ref, woa_ref, wpw_ref, woc_ref, wg_ref, bg_ref, wout_ref,
                g_ref, b_ref, of_ref, lo_ref, hi_ref):
    d = D_MODEL
    for r0 in range(0, MIX_TILE, SUB_ROWS):
        rows = slice(r0, r0 + SUB_ROWS)
        gates = _sigmoid(jnp.dot(hb_ref[rows, :], wg_ref[...], preferred_element_type=f32) + bg_ref[...])
        mix = (gates[:, 0:d] * jnp.dot(ya_ref[rows, :], woa_ref[...], preferred_element_type=f32)
               + gates[:, d:2 * d] * jnp.dot(cb_ref[rows, :], wpw_ref[...], preferred_element_type=f32)
               + gates[:, 2 * d:3 * d] * jnp.dot(yc_ref[rows, :], woc_ref[...], preferred_element_type=f32))
        z = DN_ALPHA * hf_ref[rows, :] + jnp.dot(mix.astype(bf16), wout_ref[...], preferred_element_type=f32)
        y = _layer_norm(z, g_ref[...], b_ref[...])
        of_ref[rows, :] = y
        lo, hi = _pack_row(y)
        lo_ref[rows, :] = lo
        hi_ref[rows, :] = hi


def mix_layer(ya, cb, yc, hb, hf, woa, wpw, woc, wg, bg, wout, g, b):
    n, d = hf.shape
    t = MIX_TILE
    full = lambda a: pl.BlockSpec(a.shape, lambda i: (0,) * a.ndim)
    rowb = lambda c: pl.BlockSpec((t, c), lambda i: (i, 0))
    return pl.pallas_call(
        _mix_kernel, grid=(n // t,),
        in_specs=[rowb(ya.shape[1]), rowb(cb.shape[1]), rowb(yc.shape[1]), rowb(d), rowb(d),
                  full(woa), full(wpw), full(woc), full(wg), full(bg), full(wout), full(g), full(b)],
        out_specs=[rowb(d), rowb(QUART), rowb(QUART)],
        out_shape=[jax.ShapeDtypeStruct((n, d), f32), jax.ShapeDtypeStruct((n, QUART), i32),
                   jax.ShapeDtypeStruct((n, QUART), i32)],
        compiler_params=_cparams("parallel"), name="mix_layer",
    )(ya, cb, yc, hb, hf, woa, wpw, woc, wg, bg, wout, g, b)


def _router_kernel(h_ref, wr_ref, br_ref, idx_ref, gate_ref, rank_ref, cnt_ref, base):
    i = pl.program_id(0)
    t = ROUTER_TILE
    e = N_EXPERTS

    @pl.when(i == 0)
    def _():
        base[...] = jnp.zeros_like(base)

    def split(x):
        hi = x.astype(bf16)
        return hi, (x - hi.astype(f32)).astype(bf16)

    nt = lambda a, b: lax.dot_general(a, b, (((1,), (1,)), ((), ())), preferred_element_type=f32)
    w_hi, w_lo = split(wr_ref[...])
    h_hi, h_lo = split(h_ref[...])
    logits = nt(w_hi, h_hi) + nt(w_lo, h_hi) + nt(w_hi, h_lo) + br_ref[...]
    row = lax.broadcasted_iota(i32, (e, t), 0).astype(f32)
    vals, hots = [], []
    cur = logits
    for k in range(TOP_K):
        m = jnp.max(cur, axis=0, keepdims=True)
        first = jnp.min(jnp.where(cur == m, row, float(e)), axis=0, keepdims=True)
        hot = row == first
        cur = jnp.where(hot, -jnp.inf, cur)
        vals.append(m)
        hots.append(hot)
        idx_ref[k:k + 1, :] = first.astype(i32)
    ex = [jnp.exp(v - vals[0]) for v in vals]
    den = ex[0] + ex[1] + ex[2] + ex[3]
    for k in range(TOP_K):
        gate_ref[k:k + 1, :] = ex[k] / den
    onehot = jnp.concatenate([jnp.where(h, 1.0, 0.0) for h in hots], axis=0)
    r = lax.broadcasted_iota(i32, (t, t), 0)
    c = lax.broadcasted_iota(i32, (t, t), 1)
    upper = jnp.where(r <= c, 1.0, 0.0).astype(bf16)
    prefix = jnp.dot(onehot.astype(bf16), upper, preferred_element_type=f32)
    counts = jnp.sum(onehot, axis=1, keepdims=True)
    offset = base[:, 0:1]
    for k in range(TOP_K):
        sel = jnp.where(hots[k], prefix[k * e:(k + 1) * e, :] - 1.0 + offset, 0.0)
        rank_ref[k:k + 1, :] = jnp.sum(sel, axis=0, keepdims=True).astype(i32)
        offset = offset + counts[k * e:(k + 1) * e, :]
    base[...] = jnp.broadcast_to(offset, base.shape)
    cnt_ref[...] = base[...]


def router(hf, w_rt, b_r):
    n, d = hf.shape
    t = ROUTER_TILE
    tok = pl.BlockSpec((TOP_K, t), lambda i: (0, i))
    return pl.pallas_call(
        _router_kernel, grid=(n // t,),
        in_specs=[pl.BlockSpec((t, d), lambda i: (i, 0)), pl.BlockSpec((N_EXPERTS, d), lambda i: (0, 0)),
                  pl.BlockSpec((N_EXPERTS, 1), lambda i: (0, 0))],
        out_specs=[tok, tok, tok, pl.BlockSpec((N_EXPERTS, LANES), lambda i: (0, 0))],
        out_shape=[jax.ShapeDtypeStruct((TOP_K, n), i32), jax.ShapeDtypeStruct((TOP_K, n), f32),
                   jax.ShapeDtypeStruct((TOP_K, n), i32), jax.ShapeDtypeStruct((N_EXPERTS, LANES), f32)],
        scratch_shapes=[pltpu.VMEM((N_EXPERTS, LANES), f32)],
        compiler_params=_cparams("arbitrary"), name="router",
    )(hf, w_rt, b_r)


def _dest_kernel(idx_ref, rank_ref, start_ref, dest_ref):
    t = idx_ref.shape[1]
    row = lax.broadcasted_iota(i32, (N_EXPERTS, t), 0)
    for k in range(TOP_K):
        hot = row == idx_ref[k:k + 1, :]
        off = jnp.sum(jnp.where(hot, start_ref[...], 0.0), axis=0, keepdims=True)
        dest_ref[k:k + 1, :] = rank_ref[k:k + 1, :] + off.astype(i32)


def dest_rows(idx, rank, pad_start):
    n = idx.shape[1]
    t = ROUTER_TILE
    tok = pl.BlockSpec((TOP_K, t), lambda i: (0, i))
    return pl.pallas_call(
        _dest_kernel, grid=(n // t,),
        in_specs=[tok, tok, pl.BlockSpec((N_EXPERTS, 1), lambda i: (0, 0))],
        out_specs=tok, out_shape=jax.ShapeDtypeStruct((TOP_K, n), i32),
        compiler_params=_cparams("parallel"), name="dest_rows",
    )(idx, rank, pad_start)


def _sc_mesh():
    return plsc.VectorSubcoreMesh(core_axis_name="c", subcore_axis_name="s")


def sc_scatter_rows(x_lo, x_hi, dest, n_rows):
    n, d = x_lo.shape
    kk = dest.shape[0]
    out = jax.ShapeDtypeStruct((n_rows, d), x_lo.dtype)

    @functools.partial(pl.kernel, out_type=(out, out), mesh=_sc_mesh(),
                       scratch_types=[pltpu.SemaphoreType.DMA] * kk)
    def k(lo_hbm, hi_hbm, i_hbm, olo_hbm, ohi_hbm, *sems):
        for x_hbm, o_hbm in ((lo_hbm, olo_hbm), (hi_hbm, ohi_hbm)):
            def body(x_vmem, i_vmem, o_hbm=o_hbm):
                copies = [pltpu.async_copy(x_vmem, o_hbm.at[i_vmem.at[j]], sems[j]) for j in range(kk)]
                for c in copies:
                    c.wait()

            pltpu.emit_pipeline(
                body, grid=(n // SC_WINDOW,),
                in_specs=[pl.BlockSpec((SC_WINDOW, d), lambda i: (i, 0)),
                          pl.BlockSpec((kk, SC_WINDOW), lambda i: (0, i))],
                out_specs=[], core_axis_name=("c", "s"), dimension_semantics=(pltpu.PARALLEL,),
            )(x_hbm, i_hbm)

    return k(x_lo, x_hi, dest)


def sc_gather_rows(t_lo, t_hi, idx):
    m = idx.shape[1]
    d = t_lo.shape[1]
    out = jax.ShapeDtypeStruct((m, d), t_lo.dtype)

    @functools.partial(pl.kernel, out_type=(out, out), mesh=_sc_mesh(), scratch_types=[])
    def k(lo_hbm, hi_hbm, i_hbm, olo_hbm, ohi_hbm):
        for t_hbm, o_hbm in ((lo_hbm, olo_hbm), (hi_hbm, ohi_hbm)):
            def body(i_vmem, o_vmem, t_hbm=t_hbm):
                pltpu.sync_copy(t_hbm.at[i_vmem.at[0]], o_vmem)

            pltpu.emit_pipeline(
                body, grid=(m // SC_WINDOW,),
                in_specs=[pl.BlockSpec((1, SC_WINDOW), lambda i: (0, i))],
                out_specs=[pl.BlockSpec((SC_WINDOW, d), lambda i: (i, 0))],
                core_axis_name=("c", "s"), dimension_semantics=(pltpu.PARALLEL,),
            )(i_hbm, o_hbm)

    return k(t_lo, t_hi, idx)


def _expert_kernel(be_ref, nu_ref, xlo_ref, xhi_ref, w1_ref, b1_ref, w2_ref, b2_ref, ylo_ref, yhi_ref, w1b, w2b):
    i = pl.program_id(0)
    new_expert = jnp.logical_or(i == 0, be_ref[i] != be_ref[jnp.maximum(i - 1, 0)])

    @pl.when(jnp.logical_and(i < nu_ref[0], new_expert))
    def _():
        w1b[...] = w1_ref[0, 0].astype(bf16)
        w2b[...] = w2_ref[0, 0].astype(bf16)

    @pl.when(i < nu_ref[0])
    def _():
        for r0 in range(0, MOE_BLK, SUB_ROWS):
            rows = slice(r0, r0 + SUB_ROWS)
            lo = xlo_ref[rows, :]
            hi = xhi_ref[rows, :]
            x = jnp.concatenate([_unpack_lo(lo).astype(bf16), _unpack_lo(hi).astype(bf16),
                                 _unpack_hi(lo).astype(bf16), _unpack_hi(hi).astype(bf16)], axis=1)
            u = jnp.dot(x, w1b[...], preferred_element_type=f32) + b1_ref[0, 0]
            glu = jnp.minimum(u[:, :D_EXPERT], SWIGLU_LIMIT)
            lin = jnp.clip(u[:, D_EXPERT:], -SWIGLU_LIMIT, SWIGLU_LIMIT)
            act = (glu * _sigmoid(SWIGLU_ALPHA * glu) * (lin + 1.0)).astype(bf16)
            y = jnp.dot(act, w2b[...], preferred_element_type=f32) + b2_ref[0, 0]
            ylo, yhi = _pack_row(y)
            ylo_ref[rows, :] = ylo
            yhi_ref[rows, :] = yhi


def expert_ffn(layer, block_expert, n_used, xs_lo, xs_hi, w1, b1, w2, b2):
    n_rows = xs_lo.shape[0]
    nb = n_rows // MOE_BLK
    rows = pl.BlockSpec((MOE_BLK, QUART), lambda i, be, nu: (jnp.minimum(i, nu[0] - 1), 0))
    wsel = lambda shape: pl.BlockSpec((1, 1) + shape, lambda i, be, nu: (layer, be[i], 0, 0))
    out = jax.ShapeDtypeStruct((n_rows, QUART), i32)
    return pl.pallas_call(
        _expert_kernel,
        grid_spec=pltpu.PrefetchScalarGridSpec(
            num_scalar_prefetch=2, grid=(nb,),
            in_specs=[rows, rows, wsel((D_MODEL, 2 * D_EXPERT)), wsel((1, 2 * D_EXPERT)),
                      wsel((D_EXPERT, D_MODEL)), wsel((1, D_MODEL))],
            out_specs=[rows, rows],
            scratch_shapes=[pltpu.VMEM((D_MODEL, 2 * D_EXPERT), bf16), pltpu.VMEM((D_EXPERT, D_MODEL), bf16)]),
        out_shape=[out, out],
        compiler_params=_cparams("arbitrary"), name="expert_ffn",
    )(block_expert, n_used, xs_lo, xs_hi, w1, b1, w2, b2)


def _combine_kernel(ylo_ref, yhi_ref, gt_ref, h1_ref, g_ref, b_ref, of_ref):
    z = DN_ALPHA * h1_ref[...] + _moe_sum(ylo_ref, yhi_ref, gt_ref, slice(None))
    of_ref[...] = _layer_norm(z, g_ref[...], b_ref[...])


def combine(yk_lo, yk_hi, gates_t, h1f, g, b):
    n, d = h1f.shape
    t = ROW_TILE
    ysp = pl.BlockSpec((TOP_K, t, QUART), lambda i: (0, i, 0))
    rowb = lambda c: pl.BlockSpec((t, c), lambda i: (i, 0))
    vec = pl.BlockSpec((1, d), lambda i: (0, 0))
    return pl.pallas_call(
        _combine_kernel, grid=(n // t,),
        in_specs=[ysp, ysp, rowb(TOP_K), rowb(d), vec, vec],
        out_specs=rowb(d), out_shape=jax.ShapeDtypeStruct((n, d), f32),
        compiler_params=_cparams("parallel"), name="combine",
    )(yk_lo, yk_hi, gates_t, h1f, g.reshape(1, d), b.reshape(1, d))


def _rope_tables(positions):
    inv = ROPE_THETA ** (-jnp.arange(0, C_ROPE, 2, dtype=f32) / C_ROPE)
    ang = positions.reshape(-1).astype(f32)[:, None] * inv
    cos, sin = jnp.cos(ang), jnp.sin(ang)
    pad = jnp.zeros((ang.shape[0], LANES - C_ROPE), f32)
    return jnp.concatenate([cos, cos, pad], axis=1), jnp.concatenate([-sin, sin, pad], axis=1)


def _attn_a_bias(rel_tables):
    depth = rel_tables.shape[0]
    period = A_BAND + A_QBLK - 1
    u = jnp.arange(period)
    diag = jnp.clip(u - (A_QBLK - 1) - A_LEFT, -A_MAX_REL, A_MAX_REL) + A_MAX_REL
    e = rel_tables.astype(f32)[:, :, diag]
    pitch = period + 1
    tiled = jnp.tile(e, (1, 1, A_QBLK + 1))[:, :, :A_QBLK * pitch]
    rows = tiled.reshape(depth, A_HEADS, A_QBLK, pitch)[:, :, :, :A_BAND]
    bias = rows[:, :, ::-1, :]
    r = jnp.arange(A_QBLK)[:, None]
    c = jnp.arange(A_BAND)[None, :]
    own = c - CHUNK * (r // CHUNK)
    valid = (own >= 0) & (own < A_LEFT + CHUNK)
    bias = jnp.where(valid[None, None], bias * LOG2E, NEG_INF)
    return bias.reshape(depth, A_HEADS // 2, 2 * A_QBLK, A_BAND)


def _layout_w_uq(w_uq):
    w = w_uq.reshape(C_Q_RANK, C_HEADS, C_QK)
    return jnp.concatenate([w[:, :, :C_NOPE].reshape(C_Q_RANK, -1), w[:, :, C_NOPE:].reshape(C_Q_RANK, -1)], axis=1)


def _layout_w_ukv(w_ukv):
    w = w_ukv.reshape(C_KV_RANK, C_HEADS, C_NOPE + C_V)
    return jnp.concatenate([w[:, :, :C_NOPE].reshape(C_KV_RANK, -1), w[:, :, C_NOPE:].reshape(C_KV_RANK, -1)], axis=1)


def _moe(layer, hf, lo, hi, w_rt, b_r, w1, b1, w2, b2):
    n = hf.shape[0]
    n_rows = n * TOP_K + N_EXPERTS * MOE_BLK
    nb = n_rows // MOE_BLK
    idx, gates, rank, cnt = router(hf, w_rt, b_r)
    counts = cnt[:, 0].astype(i32)
    padded = (counts + MOE_BLK - 1) // MOE_BLK * MOE_BLK
    pad_end = jnp.cumsum(padded)
    pad_start = (pad_end - padded).astype(f32).reshape(N_EXPERTS, 1)
    block_start = jnp.arange(nb, dtype=i32) * MOE_BLK
    block_expert = jnp.minimum(jnp.sum((pad_end[None, :] <= block_start[:, None]).astype(i32), axis=1),
                               N_EXPERTS - 1)
    n_used = (pad_end[-1:] // MOE_BLK).astype(i32)
    dest = dest_rows(idx, rank, pad_start)
    xs_lo, xs_hi = sc_scatter_rows(lo, hi, dest, n_rows)
    ys_lo, ys_hi = expert_ffn(layer, block_expert, n_used, xs_lo, xs_hi, w1, b1, w2, b2)
    yk_lo, yk_hi = sc_gather_rows(ys_lo, ys_hi, dest.reshape(1, TOP_K * n))
    return yk_lo.reshape(TOP_K, n, QUART), yk_hi.reshape(TOP_K, n, QUART), gates.T


def kernel(x, positions, ln_in_g, ln_in_b, w_in, w_gate, b_gate, rel_bias, conv_w, conv_b, conv_ln_g, conv_ln_b, w_pw2, q_norm_g, kv_norm_g, w_uq, w_ukv, w_oa, w_oc, w_out, ln1_g, ln1_b, w_router, b_router, w1, b1, w2, b2, ln2_g, ln2_b):
    batch, seq, d = x.shape
    n = batch * seq
    cosm, sinm = _rope_tables(positions)
    a_bias = _attn_a_bias(rel_bias)
    ab = 3 * A_WIDTH + 2 * CONV_CH
    row1 = lambda v: v.reshape(1, -1)
    b1r = b1.reshape(DEPTH, N_EXPERTS, 1, -1)
    b2r = b2.reshape(DEPTH, N_EXPERTS, 1, -1)
    moe_out = None
    for l in range(DEPTH):
        w_ab = w_in[l, :, :ab].astype(bf16)
        w_down = jnp.pad(w_in[l, :, ab:], ((0, 0), (0, LANES - C_ROPE))).astype(bf16)
        if l == 0:
            hf, hb, qkvg = entry_first(x.reshape(n, d), ln_in_g, ln_in_b, w_ab)
        else:
            hf, hb, qkvg = entry_moe(*moe_out, h1f, ln2_g[l - 1], ln2_b[l - 1], w_ab)
        qc, kc, vc = mla_prep(hb, w_down, row1(q_norm_g[l]), row1(kv_norm_g[l]),
                              _layout_w_uq(w_uq[l]).astype(bf16), _layout_w_ukv(w_ukv[l]).astype(bf16), cosm, sinm)
        cb = conv_module(qkvg, conv_w[l], conv_b[l], conv_ln_g[l], conv_ln_b[l], batch, seq)
        ya = attn_a(qkvg, a_bias[l], batch, seq)
        yc = mla_attn_pairs(qc, kc, vc, batch, seq)
        h1f, lo, hi = mix_layer(ya, cb, yc, hb, hf, w_oa[l].astype(bf16), w_pw2[l].astype(bf16),
                                w_oc[l].astype(bf16), w_gate[l].astype(bf16), row1(b_gate[l]),
                                w_out[l].astype(bf16), row1(ln1_g[l]), row1(ln1_b[l]))
        moe_out = _moe(l, h1f, lo, hi, w_router[l].T, b_router[l].reshape(N_EXPERTS, 1), w1, b1r, w2, b2r)
    return combine(*moe_out, h1f, ln2_g[DEPTH - 1], ln2_b[DEPTH - 1]).reshape(batch, seq, d)
```

```python
import functools

import jax
import jax.numpy as jnp
from jax import lax
from jax.experimental import pallas as pl
from jax.experimental.pallas import tpu as pltpu
from jax.experimental.pallas import tpu_sc as plsc

f32 = jnp.float32
bf16 = jnp.bfloat16
i32 = jnp.int32

D_MODEL = 1024
DEPTH = 4
CHUNK = 64
A_HEADS = 8
A_HEAD_DIM = 64
A_WIDTH = A_HEADS * A_HEAD_DIM
A_LEFT = 8 * CHUNK
A_MAX_REL = 128
CONV_CH = 512
CONV_WIDTH = 31
C_HEADS = 8
C_NOPE = 64
C_ROPE = 32
C_V = 64
C_QK = C_NOPE + C_ROPE
C_Q_RANK = 384
C_KV_RANK = 256
ROPE_THETA = 10000.0
N_EXPERTS = 32
TOP_K = 4
D_EXPERT = 1024
SWIGLU_ALPHA = 1.702
SWIGLU_LIMIT = 7.0
DN_ALPHA = (2 * DEPTH) ** 0.25
LN_EPS = 1e-5
RMS_EPS = 1e-6
NEG_INF = -1e30
LOG2E = 1.4426950408889634

LANES = 128
SUBLANES = 8
MXU_DIM = 256
V7X_VMEM_BYTES = 64 * 1024 * 1024
BF16_BITS = 16
HI_HALF_MASK = -(1 << BF16_BITS)
A_QBLK = 2 * CHUNK
A_STEP = 8 * A_QBLK
A_BAND = A_LEFT + A_QBLK
C_BLK = MXU_DIM
C_STEP_PAIRS = C_HEADS // 2
C_SLOT = 2 * LANES
CONV_BLK = 256
CONV_ROWS = 64
CONV_HALO = 32
MOE_BLK = 512
ROW_TILE = 512
MIX_TILE = 512
SUB_ROWS = MXU_DIM
ENTRY_SUB_ROWS = 128
ROUTER_TILE = 512
QUART = D_MODEL // 4
SC_WINDOW = 128
VMEM_LIMIT = V7X_VMEM_BYTES * 7 // 8


def _cparams(*sem):
    return pltpu.CompilerParams(dimension_semantics=tuple(sem), vmem_limit_bytes=VMEM_LIMIT)


def _layer_norm(x, g, b):
    mu = jnp.mean(x, axis=-1, keepdims=True)
    xc = x - mu
    var = jnp.mean(xc * xc, axis=-1, keepdims=True)
    return xc * lax.rsqrt(var + LN_EPS) * g + b


def _sigmoid(x):
    return 1.0 / (1.0 + jnp.exp(-x))


def _pack2(a, b):
    ab = lax.bitcast_convert_type(a.astype(bf16).astype(f32), i32)
    bb = lax.bitcast_convert_type(b.astype(bf16).astype(f32), i32)
    return lax.shift_right_logical(ab, BF16_BITS) | (bb & jnp.int32(HI_HALF_MASK))


def _unpack_lo(w):
    return lax.bitcast_convert_type(lax.shift_left(w, BF16_BITS), f32)


def _unpack_hi(w):
    return lax.bitcast_convert_type(w & jnp.int32(HI_HALF_MASK), f32)


def _pack_row(y):
    return (_pack2(y[:, 0:QUART], y[:, 2 * QUART:3 * QUART]),
            _pack2(y[:, QUART:2 * QUART], y[:, 3 * QUART:4 * QUART]))


def _moe_sum(ylo_ref, yhi_ref, gt_ref, rows):
    gt = gt_ref[rows, :]
    parts = [None] * 4
    for k in range(TOP_K):
        gk = gt[:, k:k + 1]
        lo = ylo_ref[k, rows, :]
        hi = yhi_ref[k, rows, :]
        vals = (_unpack_lo(lo), _unpack_lo(hi), _unpack_hi(lo), _unpack_hi(hi))
        for p in range(4):
            parts[p] = gk * vals[p] if parts[p] is None else parts[p] + gk * vals[p]
    return jnp.concatenate(parts, axis=1)


def _project(y, rows, w_ref, hf_ref, hb_ref, o_ref):
    hf_ref[rows, :] = y
    yb = y.astype(bf16)
    hb_ref[rows, :] = yb
    acc = jnp.dot(yb, w_ref[...], preferred_element_type=f32)
    o_ref[rows, :A_WIDTH] = (acc[:, :A_WIDTH] * (A_HEAD_DIM ** -0.5 * LOG2E)).astype(bf16)
    o_ref[rows, A_WIDTH:] = acc[:, A_WIDTH:].astype(bf16)


def _entry_first_kernel(x_ref, g_ref, b_ref, w_ref, hf_ref, hb_ref, o_ref):
    for r0 in range(0, ROW_TILE, ENTRY_SUB_ROWS):
        rows = slice(r0, r0 + ENTRY_SUB_ROWS)
        _project(_layer_norm(x_ref[rows, :], g_ref[...], b_ref[...]), rows, w_ref, hf_ref, hb_ref, o_ref)


def _entry_moe_kernel(ylo_ref, yhi_ref, gt_ref, h1_ref, g_ref, b_ref, w_ref, hf_ref, hb_ref, o_ref):
    for r0 in range(0, ROW_TILE, ENTRY_SUB_ROWS):
        rows = slice(r0, r0 + ENTRY_SUB_ROWS)
        z = DN_ALPHA * h1_ref[rows, :] + _moe_sum(ylo_ref, yhi_ref, gt_ref, rows)
        _project(_layer_norm(z, g_ref[...], b_ref[...]), rows, w_ref, hf_ref, hb_ref, o_ref)


def _entry_call(body, n, w, row_inputs, row_specs, g, b, name):
    d = D_MODEL
    c = w.shape[1]
    t = ROW_TILE
    vec = pl.BlockSpec((1, d), lambda i: (0, 0))
    rowb = lambda cc: pl.BlockSpec((t, cc), lambda i: (i, 0))
    return pl.pallas_call(
        body, grid=(n // t,),
        in_specs=row_specs + [vec, vec, pl.BlockSpec((d, c), lambda i: (0, 0))],
        out_specs=[rowb(d), rowb(d), rowb(c)],
        out_shape=[jax.ShapeDtypeStruct((n, d), f32), jax.ShapeDtypeStruct((n, d), bf16),
                   jax.ShapeDtypeStruct((n, c), bf16)],
        compiler_params=_cparams("parallel"), name=name,
    )(*row_inputs, g.reshape(1, d), b.reshape(1, d), w)


def entry_first(x2d, g, b, w):
    n, d = x2d.shape
    return _entry_call(_entry_first_kernel, n, w, [x2d], [pl.BlockSpec((ROW_TILE, d), lambda i: (i, 0))], g, b,
                       "entry_first")


def entry_moe(yk_lo, yk_hi, gates_t, h1f, g, b, w):
    n, d = h1f.shape
    t = ROW_TILE
    ysp = pl.BlockSpec((TOP_K, t, QUART), lambda i: (0, i, 0))
    specs = [ysp, ysp, pl.BlockSpec((t, TOP_K), lambda i: (i, 0)), pl.BlockSpec((t, d), lambda i: (i, 0))]
    return _entry_call(_entry_moe_kernel, n, w, [yk_lo, yk_hi, gates_t, h1f], specs, g, b, "entry_moe")


def _mla_prep_kernel(hb_ref, wdown_ref, gq_ref, gkv_ref, wuq_ref, wukv_ref, cos_ref, sin_ref, qc_ref, kc_ref, vc_ref):
    cosm = cos_ref[...]
    sinm = sin_ref[...]
    lane = lax.broadcasted_iota(i32, cosm.shape, 1)
    half = C_ROPE // 2
    scale = C_QK ** -0.5 * LOG2E

    def rope(x):
        swapped = jnp.where(lane < half, pltpu.roll(x, LANES - half, 1), pltpu.roll(x, half, 1))
        return x * cosm + swapped * sinm

    def rms(x, g):
        return (x * lax.rsqrt(jnp.mean(x * x, axis=-1, keepdims=True) + RMS_EPS) * g).astype(bf16)

    down = jnp.dot(hb_ref[...], wdown_ref[...], preferred_element_type=f32)
    kv_lo = C_Q_RANK
    kr_lo = C_Q_RANK + C_KV_RANK
    q = jnp.dot(rms(down[:, :kv_lo], gq_ref[...]), wuq_ref[...], preferred_element_type=f32)
    rope_lo = C_HEADS * C_NOPE
    heads_per_vreg = LANES // C_ROPE
    for h in range(C_HEADS):
        lo = h * C_SLOT
        pair = q[:, (h // 2) * LANES:(h // 2 + 1) * LANES]
        mine = (lane >= C_NOPE) if h % 2 else (lane < C_NOPE)
        qc_ref[:, lo:lo + LANES] = (jnp.where(mine, pair, 0.0) * scale).astype(bf16)
        group = q[:, rope_lo + (h // heads_per_vreg) * LANES:rope_lo + (h // heads_per_vreg + 1) * LANES]
        shift = (h % heads_per_vreg) * C_ROPE
        mine = pltpu.roll(group, LANES - shift, 1) if shift else group
        qc_ref[:, lo + LANES:lo + C_SLOT] = (rope(mine) * scale).astype(bf16)
    kv = jnp.dot(rms(down[:, kv_lo:kr_lo], gkv_ref[...]), wukv_ref[...], preferred_element_type=f32)
    kr = rope(down[:, kr_lo:]).astype(bf16)
    for j in range(C_HEADS // 2):
        lo = j * C_SLOT
        kc_ref[:, lo:lo + LANES] = kv[:, j * LANES:(j + 1) * LANES].astype(bf16)
        kc_ref[:, lo + LANES:lo + C_SLOT] = kr
    vc_ref[...] = kv[:, C_HEADS * C_NOPE:].astype(bf16)


def mla_prep(hb, wdown, gq, gkv, wuq, wukv, cosm, sinm):
    n, d = hb.shape
    t = ROW_TILE
    full = lambda a: pl.BlockSpec(a.shape, lambda i: (0,) * a.ndim)
    rowb = lambda c: pl.BlockSpec((t, c), lambda i: (i, 0))
    qw, kw, vw = C_HEADS * C_SLOT, (C_HEADS // 2) * C_SLOT, C_HEADS * C_V
    return pl.pallas_call(
        _mla_prep_kernel, grid=(n // t,),
        in_specs=[rowb(d), full(wdown), full(gq), full(gkv), full(wuq), full(wukv), rowb(LANES), rowb(LANES)],
        out_specs=[rowb(qw), rowb(kw), rowb(vw)],
        out_shape=[jax.ShapeDtypeStruct((n, qw), bf16), jax.ShapeDtypeStruct((n, kw), bf16),
                   jax.ShapeDtypeStruct((n, vw), bf16)],
        compiler_params=_cparams("parallel"), name="mla_prep",
    )(hb, wdown, gq, gkv, wuq, wukv, cosm, sinm)


def _conv_rows(base, hs, sh, w_ref, cb_ref, lg_ref, lb_ref, o_ref):
    t = CONV_BLK
    sl = SUBLANES
    first = CONV_HALO - CONV_WIDTH + 1
    a_of = [[a for a in range(CONV_HALO // sl + 1) if first <= sl * a + b <= CONV_HALO] for b in range(sl)]
    for b in range(1, sl):
        length = sl * max(a_of[b]) + t
        sh[b - 1, 0:length, :] = hs[base + b:base + b + length, :]
    rows = CONV_ROWS
    for r0 in range(0, t, rows):
        acc = jnp.zeros((rows, CONV_CH), f32) + cb_ref[...]
        for b in range(sl):
            for a in a_of[b]:
                tap = sl * a + b - first
                lo = r0 + sl * a
                src = hs[base + lo:base + lo + rows, :] if b == 0 else sh[b - 1, lo:lo + rows, :]
                acc = acc + src * w_ref[tap:tap + 1, :]
        y = _layer_norm(acc, lg_ref[...], lb_ref[...])
        o_ref[base + r0:base + r0 + rows, :] = (y * _sigmoid(y)).astype(bf16)


def _conv_kernel(a_ref, g_ref, ap_ref, gp_ref, w_ref, cb_ref, lg_ref, lb_ref, o_ref, hs, sh):
    i = pl.program_id(1)
    prev = ap_ref[...].astype(f32) * _sigmoid(gp_ref[...].astype(f32))
    hs[0:CONV_HALO, :] = jnp.where(i > 0, prev, 0.0)
    hs[CONV_HALO:CONV_HALO + ROW_TILE, :] = a_ref[...].astype(f32) * _sigmoid(g_ref[...].astype(f32))
    for base in range(0, ROW_TILE, CONV_BLK):
        _conv_rows(base, hs, sh, w_ref, cb_ref, lg_ref, lb_ref, o_ref)


def conv_module(qkvg, w_dw, b_dw, ln_g, ln_b, batch, seq):
    n = batch * seq
    t = ROW_TILE
    nb = seq // t
    halo_per_tile = t // CONV_HALO
    a_col = 3 * A_WIDTH // CONV_CH
    cur = lambda col: pl.BlockSpec((t, CONV_CH), lambda b, i: (b * nb + i, col))
    halo = lambda col: pl.BlockSpec(
        (CONV_HALO, CONV_CH), lambda b, i: (jnp.maximum((b * nb + i) * halo_per_tile - 1, 0), col))
    vec = pl.BlockSpec((1, CONV_CH), lambda b, i: (0, 0))
    return pl.pallas_call(
        _conv_kernel, grid=(batch, nb),
        in_specs=[cur(a_col), cur(a_col + 1), halo(a_col), halo(a_col + 1),
                  pl.BlockSpec((CONV_WIDTH, CONV_CH), lambda b, i: (0, 0)), vec, vec, vec],
        out_specs=pl.BlockSpec((t, CONV_CH), lambda b, i: (b * nb + i, 0)),
        out_shape=jax.ShapeDtypeStruct((n, CONV_CH), bf16),
        scratch_shapes=[pltpu.VMEM((CONV_HALO + t, CONV_CH), f32),
                        pltpu.VMEM((SUBLANES - 1, CONV_HALO + CONV_BLK, CONV_CH), f32)],
        compiler_params=_cparams("parallel", "parallel"), name="conv_module",
    )(qkvg, qkvg, qkvg, qkvg, w_dw, b_dw.reshape(1, -1), ln_g.reshape(1, -1), ln_b.reshape(1, -1))


def _attn_a_kernel(q_ref, k_ref, v_ref, bias_ref, o_ref, kpad, vpad):
    qi = pl.program_id(1)
    seq = k_ref.shape[0]

    @pl.when(qi == 0)
    def _():
        kpad[0:A_LEFT, :] = jnp.zeros((A_LEFT, A_WIDTH), bf16)
        vpad[0:A_LEFT, :] = jnp.zeros((A_LEFT, A_WIDTH), bf16)
        kpad[A_LEFT:A_LEFT + seq, :] = k_ref[...]
        vpad[A_LEFT:A_LEFT + seq, :] = v_ref[...]

    col = lax.broadcasted_iota(i32, (2 * A_QBLK, A_BAND), 1)
    lane = lax.broadcasted_iota(i32, (A_QBLK, LANES), 1)
    ones = jnp.ones((A_BAND, LANES), bf16)
    for sub in range(A_STEP // A_QBLK):
        start = pl.multiple_of(qi * A_STEP + sub * A_QBLK, A_QBLK)
        reaches_padding = sub * A_QBLK < A_LEFT
        before_start = jnp.where(col + start >= A_LEFT, 0.0, NEG_INF) if reaches_padding else None
        rows = slice(sub * A_QBLK, (sub + 1) * A_QBLK)
        for j in range(A_HEADS // 2):
            cs = slice(j * LANES, (j + 1) * LANES)
            qp = q_ref[rows, cs].astype(f32)
            qs = jnp.concatenate([jnp.where(lane < A_HEAD_DIM, qp, 0.0), jnp.where(lane >= A_HEAD_DIM, qp, 0.0)],
                                 axis=0).astype(bf16)
            kb = kpad[pl.ds(start, A_BAND), cs]
            vb = jnp.concatenate([vpad[pl.ds(start, A_BAND), cs], ones], axis=1)
            s = lax.dot_general(qs, kb, (((1,), (1,)), ((), ())), preferred_element_type=f32)
            s = s + bias_ref[j]
            if reaches_padding:
                s = s + before_start
            p = jnp.exp2((s - jnp.max(s, axis=-1, keepdims=True)).astype(bf16))
            o = jnp.dot(p, vb, preferred_element_type=f32)
            o = o[:, :LANES] / o[:, LANES:]
            o_ref[rows, cs] = jnp.where(lane < A_HEAD_DIM, o[:A_QBLK], o[A_QBLK:]).astype(bf16)


def attn_a(qkvg, bias, batch, seq):
    n = batch * seq
    nq = seq // A_STEP
    return pl.pallas_call(
        _attn_a_kernel, grid=(batch, nq),
        in_specs=[pl.BlockSpec((A_STEP, A_WIDTH), lambda b, i: (b * nq + i, 0)),
                  pl.BlockSpec((seq, A_WIDTH), lambda b, i: (b, 1)),
                  pl.BlockSpec((seq, A_WIDTH), lambda b, i: (b, 2)),
                  pl.BlockSpec(bias.shape, lambda b, i: (0, 0, 0))],
        out_specs=pl.BlockSpec((A_STEP, A_WIDTH), lambda b, i: (b * nq + i, 0)),
        out_shape=jax.ShapeDtypeStruct((n, A_WIDTH), bf16),
        scratch_shapes=[pltpu.VMEM((A_LEFT + seq, A_WIDTH), bf16), pltpu.VMEM((A_LEFT + seq, A_WIDTH), bf16)],
        compiler_params=_cparams("parallel", "arbitrary"), name="attn_a",
    )(qkvg, qkvg, qkvg, bias)


def _mla_pair_kernel(q_ref, k_ref, v_ref, o_ref):
    t = C_BLK
    seq = k_ref.shape[0]
    row = lax.broadcasted_iota(i32, (2 * t, t), 0)
    col = lax.broadcasted_iota(i32, (2 * t, t), 1)
    diag_ok = (col // CHUNK) <= ((row % t) // CHUNK)
    lane = lax.broadcasted_iota(i32, (t, LANES), 1)
    nt = lambda a, b: lax.dot_general(a, b, (((1,), (1,)), ((), ())), preferred_element_type=f32)
    for jj in range(C_STEP_PAIRS):
        kcols = slice(jj * C_SLOT, (jj + 1) * C_SLOT)
        vcols = slice(jj * LANES, (jj + 1) * LANES)
        for r in range(seq // t):
            lo = r * t
            vext = jnp.concatenate([v_ref[0:lo + t, vcols], jnp.ones((lo + t, LANES), bf16)], axis=1)
            q = jnp.concatenate([q_ref[lo:lo + t, (2 * jj) * C_SLOT:(2 * jj + 1) * C_SLOT],
                                 q_ref[lo:lo + t, (2 * jj + 1) * C_SLOT:(2 * jj + 2) * C_SLOT]], axis=0)
            s_diag = jnp.where(diag_ok, nt(q, k_ref[lo:lo + t, kcols]), NEG_INF)
            m = jnp.max(s_diag, axis=-1, keepdims=True)
            if r > 0:
                s_low = nt(q, k_ref[0:lo, kcols])
                m = jnp.maximum(m, jnp.max(s_low, axis=-1, keepdims=True))
                p = jnp.concatenate([jnp.exp2((s_low - m).astype(bf16)), jnp.exp2((s_diag - m).astype(bf16))],
                                    axis=1)
            else:
                p = jnp.exp2((s_diag - m).astype(bf16))
            o = jnp.dot(p, vext, preferred_element_type=f32)
            o = o[:, :LANES] / o[:, LANES:]
            o_ref[lo:lo + t, vcols] = jnp.where(lane < C_V, o[:t], o[t:]).astype(bf16)


def mla_attn_pairs(qc, kc, vc, batch, seq):
    n = batch * seq
    steps = C_HEADS // 2 // C_STEP_PAIRS
    w = C_STEP_PAIRS
    return pl.pallas_call(
        _mla_pair_kernel, grid=(batch, steps),
        in_specs=[pl.BlockSpec((seq, 2 * w * C_SLOT), lambda b, j: (b, j)),
                  pl.BlockSpec((seq, w * C_SLOT), lambda b, j: (b, j)),
                  pl.BlockSpec((seq, w * LANES), lambda b, j: (b, j))],
        out_specs=pl.BlockSpec((seq, w * LANES), lambda b, j: (b, j)),
        out_shape=jax.ShapeDtypeStruct((n, C_HEADS * C_V), bf16),
        compiler_params=_cparams("parallel", "parallel"), name="mla_attn",
    )(qc, kc, vc)


def _mix_kernel(ya_ref, cb_ref, yc_ref, hb_ref, hf_ref, woa_ref, wpw_ref, woc_ref, wg_ref, bg_ref, wout_ref,
                g_ref, b_ref, of_ref, lo_ref, hi_ref):
    d = D_MODEL
    for r0 in range(0, MIX_TILE, SUB_ROWS):
        rows = slice(r0, r0 + SUB_ROWS)
        gates = _sigmoid(jnp.dot(hb_ref[rows, :], wg_ref[...], preferred_element_type=f32) + bg_ref[...])
        mix = (gates[:, 0:d] * jnp.dot(ya_ref[rows, :], woa_ref[...], preferred_element_type=f32)
               + gates[:, d:2 * d] * jnp.dot(cb_ref[rows, :], wpw_ref[...], preferred_element_type=f32)
               + gates[:, 2 * d:3 * d] * jnp.dot(yc_ref[rows, :], woc_ref[...], preferred_element_type=f32))
        z = DN_ALPHA * hf_ref[rows, :] + jnp.dot(mix.astype(bf16), wout_ref[...], preferred_element_type=f32)
        y = _layer_norm(z, g_ref[...], b_ref[...])
        of_ref[rows, :] = y
        lo, hi = _pack_row(y)
        lo_ref[rows, :] = lo
        hi_ref[rows, :] = hi


def mix_layer(ya, cb, yc, hb, hf, woa, wpw, woc, wg, bg, wout, g, b):
    n, d = hf.shape
    t = MIX_TILE
    full = lambda a: pl.BlockSpec(a.shape, lambda i: (0,) * a.ndim)
    rowb = lambda c: pl.BlockSpec((t, c), lambda i: (i, 0))
    return pl.pallas_call(
        _mix_kernel, grid=(n // t,),
        in_specs=[rowb(ya.shape[1]), rowb(cb.shape[1]), rowb(yc.shape[1]), rowb(d), rowb(d),
                  full(woa), full(wpw), full(woc), full(wg), full(bg), full(wout), full(g), full(b)],
        out_specs=[rowb(d), rowb(QUART), rowb(QUART)],
        out_shape=[jax.ShapeDtypeStruct((n, d), f32), jax.ShapeDtypeStruct((n, QUART), i32),
                   jax.ShapeDtypeStruct((n, QUART), i32)],
        compiler_params=_cparams("parallel"), name="mix_layer",
    )(ya, cb, yc, hb, hf, woa, wpw, woc, wg, bg, wout, g, b)


def _router_kernel(h_ref, wr_ref, br_ref, idx_ref, gate_ref, rank_ref, cnt_ref, base):
    i = pl.program_id(0)
    t = ROUTER_TILE
    e = N_EXPERTS

    @pl.when(i == 0)
    def _():
        base[...] = jnp.zeros_like(base)

    def split(x):
        hi = x.astype(bf16)
        return hi, (x - hi.astype(f32)).astype(bf16)

    nt = lambda a, b: lax.dot_general(a, b, (((1,), (1,)), ((), ())), preferred_element_type=f32)
    w_hi, w_lo = split(wr_ref[...])
    h_hi, h_lo = split(h_ref[...])
    logits = nt(w_hi, h_hi) + nt(w_lo, h_hi) + nt(w_hi, h_lo) + br_ref[...]
    row = lax.broadcasted_iota(i32, (e, t), 0).astype(f32)
    vals, hots = [], []
    cur = logits
    for k in range(TOP_K):
        m = jnp.max(cur, axis=0, keepdims=True)
        first = jnp.min(jnp.where(cur == m, row, float(e)), axis=0, keepdims=True)
        hot = row == first
        cur = jnp.where(hot, -jnp.inf, cur)
        vals.append(m)
        hots.append(hot)
        idx_ref[k:k + 1, :] = first.astype(i32)
    ex = [jnp.exp(v - vals[0]) for v in vals]
    den = ex[0] + ex[1] + ex[2] + ex[3]
    for k in range(TOP_K):
        gate_ref[k:k + 1, :] = ex[k] / den
    onehot = jnp.concatenate([jnp.where(h, 1.0, 0.0) for h in hots], axis=0)
    r = lax.broadcasted_iota(i32, (t, t), 0)
    c = lax.broadcasted_iota(i32, (t, t), 1)
    upper = jnp.where(r <= c, 1.0, 0.0).astype(bf16)
    prefix = jnp.dot(onehot.astype(bf16), upper, preferred_element_type=f32)
    counts = jnp.sum(onehot, axis=1, keepdims=True)
    offset = base[:, 0:1]
    for k in range(TOP_K):
        sel = jnp.where(hots[k], prefix[k * e:(k + 1) * e, :] - 1.0 + offset, 0.0)
        rank_ref[k:k + 1, :] = jnp.sum(sel, axis=0, keepdims=True).astype(i32)
        offset = offset + counts[k * e:(k + 1) * e, :]
    base[...] = jnp.broadcast_to(offset, base.shape)
    cnt_ref[...] = base[...]


def router(hf, w_rt, b_r):
    n, d = hf.shape
    t = ROUTER_TILE
    tok = pl.BlockSpec((TOP_K, t), lambda i: (0, i))
    return pl.pallas_call(
        _router_kernel, grid=(n // t,),
        in_specs=[pl.BlockSpec((t, d), lambda i: (i, 0)), pl.BlockSpec((N_EXPERTS, d), lambda i: (0, 0)),
                  pl.BlockSpec((N_EXPERTS, 1), lambda i: (0, 0))],
        out_specs=[tok, tok, tok, pl.BlockSpec((N_EXPERTS, LANES), lambda i: (0, 0))],
        out_shape=[jax.ShapeDtypeStruct((TOP_K, n), i32), jax.ShapeDtypeStruct((TOP_K, n), f32),
                   jax.ShapeDtypeStruct((TOP_K, n), i32), jax.ShapeDtypeStruct((N_EXPERTS, LANES), f32)],
        scratch_shapes=[pltpu.VMEM((N_EXPERTS, LANES), f32)],
        compiler_params=_cparams("arbitrary"), name="router",
    )(hf, w_rt, b_r)


def _dest_kernel(idx_ref, rank_ref, start_ref, dest_ref):
    t = idx_ref.shape[1]
    row = lax.broadcasted_iota(i32, (N_EXPERTS, t), 0)
    for k in range(TOP_K):
        hot = row == idx_ref[k:k + 1, :]
        off = jnp.sum(jnp.where(hot, start_ref[...], 0.0), axis=0, keepdims=True)
        dest_ref[k:k + 1, :] = rank_ref[k:k + 1, :] + off.astype(i32)


def dest_rows(idx, rank, pad_start):
    n = idx.shape[1]
    t = ROUTER_TILE
    tok = pl.BlockSpec((TOP_K, t), lambda i: (0, i))
    return pl.pallas_call(
        _dest_kernel, grid=(n // t,),
        in_specs=[tok, tok, pl.BlockSpec((N_EXPERTS, 1), lambda i: (0, 0))],
        out_specs=tok, out_shape=jax.ShapeDtypeStruct((TOP_K, n), i32),
        compiler_params=_cparams("parallel"), name="dest_rows",
    )(idx, rank, pad_start)


def _sc_mesh():
    return plsc.VectorSubcoreMesh(core_axis_name="c", subcore_axis_name="s")


def sc_scatter_rows(x_lo, x_hi, dest, n_rows):
    n, d = x_lo.shape
    kk = dest.shape[0]
    out = jax.ShapeDtypeStruct((n_rows, d), x_lo.dtype)

    @functools.partial(pl.kernel, out_type=(out, out), mesh=_sc_mesh(), scratch_types=[])
    def k(lo_hbm, hi_hbm, i_hbm, olo_hbm, ohi_hbm):
        for x_hbm, o_hbm in ((lo_hbm, olo_hbm), (hi_hbm, ohi_hbm)):
            def body(x_vmem, i_vmem, o_hbm=o_hbm):
                for j in range(kk):
                    pltpu.sync_copy(x_vmem, o_hbm.at[i_vmem.at[j]])

            pltpu.emit_pipeline(
                body, grid=(n // SC_WINDOW,),
                in_specs=[pl.BlockSpec((SC_WINDOW, d), lambda i: (i, 0)),
                          pl.BlockSpec((kk, SC_WINDOW), lambda i: (0, i))],
                out_specs=[], core_axis_name=("c", "s"), dimension_semantics=(pltpu.PARALLEL,),
            )(x_hbm, i_hbm)

    return k(x_lo, x_hi, dest)


def sc_gather_rows(t_lo, t_hi, idx):
    m = idx.shape[1]
    d = t_lo.shape[1]
    out = jax.ShapeDtypeStruct((m, d), t_lo.dtype)

    @functools.partial(pl.kernel, out_type=(out, out), mesh=_sc_mesh(), scratch_types=[])
    def k(lo_hbm, hi_hbm, i_hbm, olo_hbm, ohi_hbm):
        for t_hbm, o_hbm in ((lo_hbm, olo_hbm), (hi_hbm, ohi_hbm)):
            def body(i_vmem, o_vmem, t_hbm=t_hbm):
                pltpu.sync_copy(t_hbm.at[i_vmem.at[0]], o_vmem)

            pltpu.emit_pipeline(
                body, grid=(m // SC_WINDOW,),
                in_specs=[pl.BlockSpec((1, SC_WINDOW), lambda i: (0, i))],
                out_specs=[pl.BlockSpec((SC_WINDOW, d), lambda i: (i, 0))],
                core_axis_name=("c", "s"), dimension_semantics=(pltpu.PARALLEL,),
            )(i_hbm, o_hbm)

    return k(t_lo, t_hi, idx)


def _expert_kernel(be_ref, nu_ref, xlo_ref, xhi_ref, w1_ref, b1_ref, w2_ref, b2_ref, ylo_ref, yhi_ref, w1b, w2b):
    i = pl.program_id(0)
    new_expert = jnp.logical_or(i == 0, be_ref[i] != be_ref[jnp.maximum(i - 1, 0)])

    @pl.when(jnp.logical_and(i < nu_ref[0], new_expert))
    def _():
        w1b[...] = w1_ref[0, 0].astype(bf16)
        w2b[...] = w2_ref[0, 0].astype(bf16)

    @pl.when(i < nu_ref[0])
    def _():
        for r0 in range(0, MOE_BLK, SUB_ROWS):
            rows = slice(r0, r0 + SUB_ROWS)
            lo = xlo_ref[rows, :]
            hi = xhi_ref[rows, :]
            x = jnp.concatenate([_unpack_lo(lo).astype(bf16), _unpack_lo(hi).astype(bf16),
                                 _unpack_hi(lo).astype(bf16), _unpack_hi(hi).astype(bf16)], axis=1)
            u = jnp.dot(x, w1b[...], preferred_element_type=f32) + b1_ref[0, 0]
            glu = jnp.minimum(u[:, :D_EXPERT], SWIGLU_LIMIT)
            lin = jnp.clip(u[:, D_EXPERT:], -SWIGLU_LIMIT, SWIGLU_LIMIT)
            act = (glu * _sigmoid(SWIGLU_ALPHA * glu) * (lin + 1.0)).astype(bf16)
            y = jnp.dot(act, w2b[...], preferred_element_type=f32) + b2_ref[0, 0]
            ylo, yhi = _pack_row(y)
            ylo_ref[rows, :] = ylo
            yhi_ref[rows, :] = yhi


def expert_ffn(layer, block_expert, n_used, xs_lo, xs_hi, w1, b1, w2, b2):
    n_rows = xs_lo.shape[0]
    nb = n_rows // MOE_BLK
    rows = pl.BlockSpec((MOE_BLK, QUART), lambda i, be, nu: (jnp.minimum(i, nu[0] - 1), 0))
    wsel = lambda shape: pl.BlockSpec((1, 1) + shape, lambda i, be, nu: (layer, be[i], 0, 0))
    out = jax.ShapeDtypeStruct((n_rows, QUART), i32)
    return pl.pallas_call(
        _expert_kernel,
        grid_spec=pltpu.PrefetchScalarGridSpec(
            num_scalar_prefetch=2, grid=(nb,),
            in_specs=[rows, rows, wsel((D_MODEL, 2 * D_EXPERT)), wsel((1, 2 * D_EXPERT)),
                      wsel((D_EXPERT, D_MODEL)), wsel((1, D_MODEL))],
            out_specs=[rows, rows],
            scratch_shapes=[pltpu.VMEM((D_MODEL, 2 * D_EXPERT), bf16), pltpu.VMEM((D_EXPERT, D_MODEL), bf16)]),
        out_shape=[out, out],
        compiler_params=_cparams("arbitrary"), name="expert_ffn",
    )(block_expert, n_used, xs_lo, xs_hi, w1, b1, w2, b2)


def _combine_kernel(ylo_ref, yhi_ref, gt_ref, h1_ref, g_ref, b_ref, of_ref):
    z = DN_ALPHA * h1_ref[...] + _moe_sum(ylo_ref, yhi_ref, gt_ref, slice(None))
    of_ref[...] = _layer_norm(z, g_ref[...], b_ref[...])


def combine(yk_lo, yk_hi, gates_t, h1f, g, b):
    n, d = h1f.shape
    t = ROW_TILE
    ysp = pl.BlockSpec((TOP_K, t, QUART), lambda i: (0, i, 0))
    rowb = lambda c: pl.BlockSpec((t, c), lambda i: (i, 0))
    vec = pl.BlockSpec((1, d), lambda i: (0, 0))
    return pl.pallas_call(
        _combine_kernel, grid=(n // t,),
        in_specs=[ysp, ysp, rowb(TOP_K), rowb(d), vec, vec],
        out_specs=rowb(d), out_shape=jax.ShapeDtypeStruct((n, d), f32),
        compiler_params=_cparams("parallel"), name="combine",
    )(yk_lo, yk_hi, gates_t, h1f, g.reshape(1, d), b.reshape(1, d))


def _rope_tables(positions):
    inv = ROPE_THETA ** (-jnp.arange(0, C_ROPE, 2, dtype=f32) / C_ROPE)
    ang = positions.reshape(-1).astype(f32)[:, None] * inv
    cos, sin = jnp.cos(ang), jnp.sin(ang)
    pad = jnp.zeros((ang.shape[0], LANES - C_ROPE), f32)
    return jnp.concatenate([cos, cos, pad], axis=1), jnp.concatenate([-sin, sin, pad], axis=1)


def _attn_a_bias(rel_tables):
    depth = rel_tables.shape[0]
    period = A_BAND + A_QBLK - 1
    u = jnp.arange(period)
    diag = jnp.clip(u - (A_QBLK - 1) - A_LEFT, -A_MAX_REL, A_MAX_REL) + A_MAX_REL
    e = rel_tables.astype(f32)[:, :, diag]
    pitch = period + 1
    tiled = jnp.tile(e, (1, 1, A_QBLK + 1))[:, :, :A_QBLK * pitch]
    rows = tiled.reshape(depth, A_HEADS, A_QBLK, pitch)[:, :, :, :A_BAND]
    bias = rows[:, :, ::-1, :]
    r = jnp.arange(A_QBLK)[:, None]
    c = jnp.arange(A_BAND)[None, :]
    own = c - CHUNK * (r // CHUNK)
    valid = (own >= 0) & (own < A_LEFT + CHUNK)
    bias = jnp.where(valid[None, None], bias * LOG2E, NEG_INF)
    return bias.reshape(depth, A_HEADS // 2, 2 * A_QBLK, A_BAND)


def _layout_w_uq(w_uq):
    w = w_uq.reshape(C_Q_RANK, C_HEADS, C_QK)
    return jnp.concatenate([w[:, :, :C_NOPE].reshape(C_Q_RANK, -1), w[:, :, C_NOPE:].reshape(C_Q_RANK, -1)], axis=1)


def _layout_w_ukv(w_ukv):
    w = w_ukv.reshape(C_KV_RANK, C_HEADS, C_NOPE + C_V)
    return jnp.concatenate([w[:, :, :C_NOPE].reshape(C_KV_RANK, -1), w[:, :, C_NOPE:].reshape(C_KV_RANK, -1)], axis=1)


def _moe(layer, hf, lo, hi, w_rt, b_r, w1, b1, w2, b2):
    n = hf.shape[0]
    n_rows = n * TOP_K + N_EXPERTS * MOE_BLK
    nb = n_rows // MOE_BLK
    idx, gates, rank, cnt = router(hf, w_rt, b_r)
    counts = cnt[:, 0].astype(i32)
    padded = (counts + MOE_BLK - 1) // MOE_BLK * MOE_BLK
    pad_end = jnp.cumsum(padded)
    pad_start = (pad_end - padded).astype(f32).reshape(N_EXPERTS, 1)
    block_start = jnp.arange(nb, dtype=i32) * MOE_BLK
    block_expert = jnp.minimum(jnp.sum((pad_end[None, :] <= block_start[:, None]).astype(i32), axis=1),
                               N_EXPERTS - 1)
    n_used = (pad_end[-1:] // MOE_BLK).astype(i32)
    dest = dest_rows(idx, rank, pad_start)
    xs_lo, xs_hi = sc_scatter_rows(lo, hi, dest, n_rows)
    ys_lo, ys_hi = expert_ffn(layer, block_expert, n_used, xs_lo, xs_hi, w1, b1, w2, b2)
    yk_lo, yk_hi = sc_gather_rows(ys_lo, ys_hi, dest.reshape(1, TOP_K * n))
    return yk_lo.reshape(TOP_K, n, QUART), yk_hi.reshape(TOP_K, n, QUART), gates.T


def kernel(x, positions, ln_in_g, ln_in_b, w_in, w_gate, b_gate, rel_bias, conv_w, conv_b, conv_ln_g, conv_ln_b, w_pw2, q_norm_g, kv_norm_g, w_uq, w_ukv, w_oa, w_oc, w_out, ln1_g, ln1_b, w_router, b_router, w1, b1, w2, b2, ln2_g, ln2_b):
    batch, seq, d = x.shape
    n = batch * seq
    cosm, sinm = _rope_tables(positions)
    a_bias = _attn_a_bias(rel_bias)
    ab = 3 * A_WIDTH + 2 * CONV_CH
    row1 = lambda v: v.reshape(1, -1)
    b1r = b1.reshape(DEPTH, N_EXPERTS, 1, -1)
    b2r = b2.reshape(DEPTH, N_EXPERTS, 1, -1)
    moe_out = None
    for l in range(DEPTH):
        w_ab = w_in[l, :, :ab].astype(bf16)
        w_down = jnp.pad(w_in[l, :, ab:], ((0, 0), (0, LANES - C_ROPE))).astype(bf16)
        if l == 0:
            hf, hb, qkvg = entry_first(x.reshape(n, d), ln_in_g, ln_in_b, w_ab)
        else:
            hf, hb, qkvg = entry_moe(*moe_out, h1f, ln2_g[l - 1], ln2_b[l - 1], w_ab)
        qc, kc, vc = mla_prep(hb, w_down, row1(q_norm_g[l]), row1(kv_norm_g[l]),
                              _layout_w_uq(w_uq[l]).astype(bf16), _layout_w_ukv(w_ukv[l]).astype(bf16), cosm, sinm)
        cb = conv_module(qkvg, conv_w[l], conv_b[l], conv_ln_g[l], conv_ln_b[l], batch, seq)
        ya = attn_a(qkvg, a_bias[l], batch, seq)
        yc = mla_attn_pairs(qc, kc, vc, batch, seq)
        h1f, lo, hi = mix_layer(ya, cb, yc, hb, hf, w_oa[l].astype(bf16), w_pw2[l].astype(bf16),
                                w_oc[l].astype(bf16), w_gate[l].astype(bf16), row1(b_gate[l]),
                                w_out[l].astype(bf16), row1(ln1_g[l]), row1(ln1_b[l]))
        moe_out = _moe(l, h1f, lo, hi, w_router[l].T, b_router[l].reshape(N_EXPERTS, 1), w1, b1r, w2, b2r)
    return combine(*moe_out, h1f, ln2_g[DEPTH - 1], ln2_b[DEPTH - 1]).reshape(batch, seq, d)
```

```python
import functools

import jax
import jax.numpy as jnp
from jax import lax
from jax.experimental import pallas as pl
from jax.experimental.pallas import tpu as pltpu
from jax.experimental.pallas import tpu_sc as plsc

f32 = jnp.float32
bf16 = jnp.bfloat16
i32 = jnp.int32

D_MODEL = 1024
DEPTH = 4
CHUNK = 64
A_HEADS = 8
A_HEAD_DIM = 64
A_WIDTH = A_HEADS * A_HEAD_DIM
A_LEFT = 8 * CHUNK
A_MAX_REL = 128
CONV_CH = 512
CONV_WIDTH = 31
C_HEADS = 8
C_NOPE = 64
C_ROPE = 32
C_V = 64
C_QK = C_NOPE + C_ROPE
C_Q_RANK = 384
C_KV_RANK = 256
ROPE_THETA = 10000.0
N_EXPERTS = 32
TOP_K = 4
D_EXPERT = 1024
SWIGLU_ALPHA = 1.702
SWIGLU_LIMIT = 7.0
DN_ALPHA = (2 * DEPTH) ** 0.25
LN_EPS = 1e-5
RMS_EPS = 1e-6
NEG_INF = -1e30
LOG2E = 1.4426950408889634

LANES = 128
SUBLANES = 8
MXU_DIM = 256
V7X_VMEM_BYTES = 64 * 1024 * 1024
BF16_BITS = 16
HI_HALF_MASK = -(1 << BF16_BITS)
A_QBLK = 2 * CHUNK
A_STEP = 8 * A_QBLK
A_BAND = A_LEFT + A_QBLK
C_BLK = MXU_DIM
C_STEP_PAIRS = C_HEADS // 2
C_SLOT = 2 * LANES
CONV_BLK = 256
CONV_ROWS = 64
CONV_HALO = 32
MOE_BLK = 512
ROW_TILE = 512
MIX_TILE = 512
SUB_ROWS = MXU_DIM
ENTRY_SUB_ROWS = 128
ROUTER_TILE = 1024
DEST_TILE = 4096
QUART = D_MODEL // 4
SC_WINDOW = 128
VMEM_LIMIT = V7X_VMEM_BYTES * 7 // 8


def _cparams(*sem):
    return pltpu.CompilerParams(dimension_semantics=tuple(sem), vmem_limit_bytes=VMEM_LIMIT)


def _layer_norm(x, g, b):
    mu = jnp.mean(x, axis=-1, keepdims=True)
    xc = x - mu
    var = jnp.mean(xc * xc, axis=-1, keepdims=True)
    return xc * lax.rsqrt(var + LN_EPS) * g + b


def _sigmoid(x):
    return 1.0 / (1.0 + jnp.exp(-x))


def _pack2(a, b):
    ab = lax.bitcast_convert_type(a.astype(bf16).astype(f32), i32)
    bb = lax.bitcast_convert_type(b.astype(bf16).astype(f32), i32)
    return lax.shift_right_logical(ab, BF16_BITS) | (bb & jnp.int32(HI_HALF_MASK))


def _unpack_lo(w):
    return lax.bitcast_convert_type(lax.shift_left(w, BF16_BITS), f32)


def _unpack_hi(w):
    return lax.bitcast_convert_type(w & jnp.int32(HI_HALF_MASK), f32)


def _pack_row(y):
    return (_pack2(y[:, 0:QUART], y[:, 2 * QUART:3 * QUART]),
            _pack2(y[:, QUART:2 * QUART], y[:, 3 * QUART:4 * QUART]))


def _moe_sum(ylo_ref, yhi_ref, gt_ref, rows):
    gt = gt_ref[rows, :]
    parts = [None] * 4
    for k in range(TOP_K):
        gk = gt[:, k:k + 1]
        lo = ylo_ref[k, rows, :]
        hi = yhi_ref[k, rows, :]
        vals = (_unpack_lo(lo), _unpack_lo(hi), _unpack_hi(lo), _unpack_hi(hi))
        for p in range(4):
            parts[p] = gk * vals[p] if parts[p] is None else parts[p] + gk * vals[p]
    return jnp.concatenate(parts, axis=1)


def _project(y, rows, w_ref, hf_ref, hb_ref, o_ref):
    hf_ref[rows, :] = y
    yb = y.astype(bf16)
    hb_ref[rows, :] = yb
    acc = jnp.dot(yb, w_ref[...], preferred_element_type=f32)
    o_ref[rows, :A_WIDTH] = (acc[:, :A_WIDTH] * (A_HEAD_DIM ** -0.5 * LOG2E)).astype(bf16)
    o_ref[rows, A_WIDTH:] = acc[:, A_WIDTH:].astype(bf16)


def _entry_first_kernel(x_ref, g_ref, b_ref, w_ref, hf_ref, hb_ref, o_ref):
    for r0 in range(0, ROW_TILE, ENTRY_SUB_ROWS):
        rows = slice(r0, r0 + ENTRY_SUB_ROWS)
        _project(_layer_norm(x_ref[rows, :], g_ref[...], b_ref[...]), rows, w_ref, hf_ref, hb_ref, o_ref)


def _entry_moe_kernel(ylo_ref, yhi_ref, gt_ref, h1_ref, g_ref, b_ref, w_ref, hf_ref, hb_ref, o_ref):
    for r0 in range(0, ROW_TILE, ENTRY_SUB_ROWS):
        rows = slice(r0, r0 + ENTRY_SUB_ROWS)
        z = DN_ALPHA * h1_ref[rows, :] + _moe_sum(ylo_ref, yhi_ref, gt_ref, rows)
        _project(_layer_norm(z, g_ref[...], b_ref[...]), rows, w_ref, hf_ref, hb_ref, o_ref)


def _entry_call(body, n, w, row_inputs, row_specs, g, b, name):
    d = D_MODEL
    c = w.shape[1]
    t = ROW_TILE
    vec = pl.BlockSpec((1, d), lambda i: (0, 0))
    rowb = lambda cc: pl.BlockSpec((t, cc), lambda i: (i, 0))
    return pl.pallas_call(
        body, grid=(n // t,),
        in_specs=row_specs + [vec, vec, pl.BlockSpec((d, c), lambda i: (0, 0))],
        out_specs=[rowb(d), rowb(d), rowb(c)],
        out_shape=[jax.ShapeDtypeStruct((n, d), f32), jax.ShapeDtypeStruct((n, d), bf16),
                   jax.ShapeDtypeStruct((n, c), bf16)],
        compiler_params=_cparams("parallel"), name=name,
    )(*row_inputs, g.reshape(1, d), b.reshape(1, d), w)


def entry_first(x2d, g, b, w):
    n, d = x2d.shape
    return _entry_call(_entry_first_kernel, n, w, [x2d], [pl.BlockSpec((ROW_TILE, d), lambda i: (i, 0))], g, b,
                       "entry_first")


def entry_moe(yk_lo, yk_hi, gates_t, h1f, g, b, w):
    n, d = h1f.shape
    t = ROW_TILE
    ysp = pl.BlockSpec((TOP_K, t, QUART), lambda i: (0, i, 0))
    specs = [ysp, ysp, pl.BlockSpec((t, TOP_K), lambda i: (i, 0)), pl.BlockSpec((t, d), lambda i: (i, 0))]
    return _entry_call(_entry_moe_kernel, n, w, [yk_lo, yk_hi, gates_t, h1f], specs, g, b, "entry_moe")


def _mla_prep_kernel(hb_ref, wdown_ref, gq_ref, gkv_ref, wuq_ref, wukv_ref, cos_ref, sin_ref, qc_ref, kc_ref, vc_ref):
    cosm = cos_ref[...]
    sinm = sin_ref[...]
    lane = lax.broadcasted_iota(i32, cosm.shape, 1)
    half = C_ROPE // 2
    scale = C_QK ** -0.5 * LOG2E

    def rope(x):
        swapped = jnp.where(lane < half, pltpu.roll(x, LANES - half, 1), pltpu.roll(x, half, 1))
        return x * cosm + swapped * sinm

    def rms(x, g):
        return (x * lax.rsqrt(jnp.mean(x * x, axis=-1, keepdims=True) + RMS_EPS) * g).astype(bf16)

    down = jnp.dot(hb_ref[...], wdown_ref[...], preferred_element_type=f32)
    kv_lo = C_Q_RANK
    kr_lo = C_Q_RANK + C_KV_RANK
    q = jnp.dot(rms(down[:, :kv_lo], gq_ref[...]), wuq_ref[...], preferred_element_type=f32)
    rope_lo = C_HEADS * C_NOPE
    heads_per_vreg = LANES // C_ROPE
    for h in range(C_HEADS):
        lo = h * C_SLOT
        pair = q[:, (h // 2) * LANES:(h // 2 + 1) * LANES]
        mine = (lane >= C_NOPE) if h % 2 else (lane < C_NOPE)
        qc_ref[:, lo:lo + LANES] = (jnp.where(mine, pair, 0.0) * scale).astype(bf16)
        group = q[:, rope_lo + (h // heads_per_vreg) * LANES:rope_lo + (h // heads_per_vreg + 1) * LANES]
        shift = (h % heads_per_vreg) * C_ROPE
        mine = pltpu.roll(group, LANES - shift, 1) if shift else group
        qc_ref[:, lo + LANES:lo + C_SLOT] = (rope(mine) * scale).astype(bf16)
    kv = jnp.dot(rms(down[:, kv_lo:kr_lo], gkv_ref[...]), wukv_ref[...], preferred_element_type=f32)
    kr = rope(down[:, kr_lo:]).astype(bf16)
    for j in range(C_HEADS // 2):
        lo = j * C_SLOT
        kc_ref[:, lo:lo + LANES] = kv[:, j * LANES:(j + 1) * LANES].astype(bf16)
        kc_ref[:, lo + LANES:lo + C_SLOT] = kr
    vc_ref[...] = kv[:, C_HEADS * C_NOPE:].astype(bf16)


def mla_prep(hb, wdown, gq, gkv, wuq, wukv, cosm, sinm):
    n, d = hb.shape
    t = ROW_TILE
    full = lambda a: pl.BlockSpec(a.shape, lambda i: (0,) * a.ndim)
    rowb = lambda c: pl.BlockSpec((t, c), lambda i: (i, 0))
    qw, kw, vw = C_HEADS * C_SLOT, (C_HEADS // 2) * C_SLOT, C_HEADS * C_V
    return pl.pallas_call(
        _mla_prep_kernel, grid=(n // t,),
        in_specs=[rowb(d), full(wdown), full(gq), full(gkv), full(wuq), full(wukv), rowb(LANES), rowb(LANES)],
        out_specs=[rowb(qw), rowb(kw), rowb(vw)],
        out_shape=[jax.ShapeDtypeStruct((n, qw), bf16), jax.ShapeDtypeStruct((n, kw), bf16),
                   jax.ShapeDtypeStruct((n, vw), bf16)],
        compiler_params=_cparams("parallel"), name="mla_prep",
    )(hb, wdown, gq, gkv, wuq, wukv, cosm, sinm)


def _conv_rows(base, hs, sh, w_ref, cb_ref, lg_ref, lb_ref, o_ref):
    t = CONV_BLK
    sl = SUBLANES
    first = CONV_HALO - CONV_WIDTH + 1
    a_of = [[a for a in range(CONV_HALO // sl + 1) if first <= sl * a + b <= CONV_HALO] for b in range(sl)]
    for b in range(1, sl):
        length = sl * max(a_of[b]) + t
        sh[b - 1, 0:length, :] = hs[base + b:base + b + length, :]
    rows = CONV_ROWS
    for r0 in range(0, t, rows):
        acc = jnp.zeros((rows, CONV_CH), f32) + cb_ref[...]
        for b in range(sl):
            for a in a_of[b]:
                tap = sl * a + b - first
                lo = r0 + sl * a
                src = hs[base + lo:base + lo + rows, :] if b == 0 else sh[b - 1, lo:lo + rows, :]
                acc = acc + src * w_ref[tap:tap + 1, :]
        y = _layer_norm(acc, lg_ref[...], lb_ref[...])
        o_ref[base + r0:base + r0 + rows, :] = (y * _sigmoid(y)).astype(bf16)


def _conv_kernel(a_ref, g_ref, ap_ref, gp_ref, w_ref, cb_ref, lg_ref, lb_ref, o_ref, hs, sh):
    i = pl.program_id(1)
    prev = ap_ref[...].astype(f32) * _sigmoid(gp_ref[...].astype(f32))
    hs[0:CONV_HALO, :] = jnp.where(i > 0, prev, 0.0)
    hs[CONV_HALO:CONV_HALO + ROW_TILE, :] = a_ref[...].astype(f32) * _sigmoid(g_ref[...].astype(f32))
    for base in range(0, ROW_TILE, CONV_BLK):
        _conv_rows(base, hs, sh, w_ref, cb_ref, lg_ref, lb_ref, o_ref)


def conv_module(qkvg, w_dw, b_dw, ln_g, ln_b, batch, seq):
    n = batch * seq
    t = ROW_TILE
    nb = seq // t
    halo_per_tile = t // CONV_HALO
    a_col = 3 * A_WIDTH // CONV_CH
    cur = lambda col: pl.BlockSpec((t, CONV_CH), lambda b, i: (b * nb + i, col))
    halo = lambda col: pl.BlockSpec(
        (CONV_HALO, CONV_CH), lambda b, i: (jnp.maximum((b * nb + i) * halo_per_tile - 1, 0), col))
    vec = pl.BlockSpec((1, CONV_CH), lambda b, i: (0, 0))
    return pl.pallas_call(
        _conv_kernel, grid=(batch, nb),
        in_specs=[cur(a_col), cur(a_col + 1), halo(a_col), halo(a_col + 1),
                  pl.BlockSpec((CONV_WIDTH, CONV_CH), lambda b, i: (0, 0)), vec, vec, vec],
        out_specs=pl.BlockSpec((t, CONV_CH), lambda b, i: (b * nb + i, 0)),
        out_shape=jax.ShapeDtypeStruct((n, CONV_CH), bf16),
        scratch_shapes=[pltpu.VMEM((CONV_HALO + t, CONV_CH), f32),
                        pltpu.VMEM((SUBLANES - 1, CONV_HALO + CONV_BLK, CONV_CH), f32)],
        compiler_params=_cparams("parallel", "parallel"), name="conv_module",
    )(qkvg, qkvg, qkvg, qkvg, w_dw, b_dw.reshape(1, -1), ln_g.reshape(1, -1), ln_b.reshape(1, -1))


def _attn_a_kernel(q_ref, k_ref, v_ref, bias_ref, o_ref, kpad, vpad):
    qi = pl.program_id(1)
    seq = k_ref.shape[0]

    @pl.when(qi == 0)
    def _():
        kpad[0:A_LEFT, :] = jnp.zeros((A_LEFT, A_WIDTH), bf16)
        vpad[0:A_LEFT, :] = jnp.zeros((A_LEFT, A_WIDTH), bf16)
        kpad[A_LEFT:A_LEFT + seq, :] = k_ref[...]
        vpad[A_LEFT:A_LEFT + seq, :] = v_ref[...]

    col = lax.broadcasted_iota(i32, (2 * A_QBLK, A_BAND), 1)
    lane = lax.broadcasted_iota(i32, (A_QBLK, LANES), 1)
    ones = jnp.ones((A_BAND, LANES), bf16)
    for sub in range(A_STEP // A_QBLK):
        start = pl.multiple_of(qi * A_STEP + sub * A_QBLK, A_QBLK)
        reaches_padding = sub * A_QBLK < A_LEFT
        before_start = jnp.where(col + start >= A_LEFT, 0.0, NEG_INF) if reaches_padding else None
        rows = slice(sub * A_QBLK, (sub + 1) * A_QBLK)
        for j in range(A_HEADS // 2):
            cs = slice(j * LANES, (j + 1) * LANES)
            qp = q_ref[rows, cs].astype(f32)
            qs = jnp.concatenate([jnp.where(lane < A_HEAD_DIM, qp, 0.0), jnp.where(lane >= A_HEAD_DIM, qp, 0.0)],
                                 axis=0).astype(bf16)
            kb = kpad[pl.ds(start, A_BAND), cs]
            vb = jnp.concatenate([vpad[pl.ds(start, A_BAND), cs], ones], axis=1)
            s = lax.dot_general(qs, kb, (((1,), (1,)), ((), ())), preferred_element_type=f32)
            s = s + bias_ref[j]
            if reaches_padding:
                s = s + before_start
            p = jnp.exp2((s - jnp.max(s, axis=-1, keepdims=True)).astype(bf16))
            o = jnp.dot(p, vb, preferred_element_type=f32)
            o = o[:, :LANES] / o[:, LANES:]
            o_ref[rows, cs] = jnp.where(lane < A_HEAD_DIM, o[:A_QBLK], o[A_QBLK:]).astype(bf16)


def attn_a(qkvg, bias, batch, seq):
    n = batch * seq
    nq = seq // A_STEP
    return pl.pallas_call(
        _attn_a_kernel, grid=(batch, nq),
        in_specs=[pl.BlockSpec((A_STEP, A_WIDTH), lambda b, i: (b * nq + i, 0)),
                  pl.BlockSpec((seq, A_WIDTH), lambda b, i: (b, 1)),
                  pl.BlockSpec((seq, A_WIDTH), lambda b, i: (b, 2)),
                  pl.BlockSpec(bias.shape, lambda b, i: (0, 0, 0))],
        out_specs=pl.BlockSpec((A_STEP, A_WIDTH), lambda b, i: (b * nq + i, 0)),
        out_shape=jax.ShapeDtypeStruct((n, A_WIDTH), bf16),
        scratch_shapes=[pltpu.VMEM((A_LEFT + seq, A_WIDTH), bf16), pltpu.VMEM((A_LEFT + seq, A_WIDTH), bf16)],
        compiler_params=_cparams("parallel", "arbitrary"), name="attn_a",
    )(qkvg, qkvg, qkvg, bias)


def _mla_pair_kernel(q_ref, k_ref, v_ref, o_ref):
    t = C_BLK
    seq = k_ref.shape[0]
    row = lax.broadcasted_iota(i32, (2 * t, t), 0)
    col = lax.broadcasted_iota(i32, (2 * t, t), 1)
    diag_ok = (col // CHUNK) <= ((row % t) // CHUNK)
    lane = lax.broadcasted_iota(i32, (t, LANES), 1)
    nt = lambda a, b: lax.dot_general(a, b, (((1,), (1,)), ((), ())), preferred_element_type=f32)
    for jj in range(C_STEP_PAIRS):
        kcols = slice(jj * C_SLOT, (jj + 1) * C_SLOT)
        vcols = slice(jj * LANES, (jj + 1) * LANES)
        for r in range(seq // t):
            lo = r * t
            vext = jnp.concatenate([v_ref[0:lo + t, vcols], jnp.ones((lo + t, LANES), bf16)], axis=1)
            q = jnp.concatenate([q_ref[lo:lo + t, (2 * jj) * C_SLOT:(2 * jj + 1) * C_SLOT],
                                 q_ref[lo:lo + t, (2 * jj + 1) * C_SLOT:(2 * jj + 2) * C_SLOT]], axis=0)
            s_diag = jnp.where(diag_ok, nt(q, k_ref[lo:lo + t, kcols]), NEG_INF)
            m = jnp.max(s_diag, axis=-1, keepdims=True)
            if r > 0:
                s_low = nt(q, k_ref[0:lo, kcols])
                m = jnp.maximum(m, jnp.max(s_low, axis=-1, keepdims=True))
                p = jnp.concatenate([jnp.exp2((s_low - m).astype(bf16)), jnp.exp2((s_diag - m).astype(bf16))],
                                    axis=1)
            else:
                p = jnp.exp2((s_diag - m).astype(bf16))
            o = jnp.dot(p, vext, preferred_element_type=f32)
            o = o[:, :LANES] / o[:, LANES:]
            o_ref[lo:lo + t, vcols] = jnp.where(lane < C_V, o[:t], o[t:]).astype(bf16)


def mla_attn_pairs(qc, kc, vc, batch, seq):
    n = batch * seq
    steps = C_HEADS // 2 // C_STEP_PAIRS
    w = C_STEP_PAIRS
    return pl.pallas_call(
        _mla_pair_kernel, grid=(batch, steps),
        in_specs=[pl.BlockSpec((seq, 2 * w * C_SLOT), lambda b, j: (b, j)),
                  pl.BlockSpec((seq, w * C_SLOT), lambda b, j: (b, j)),
                  pl.BlockSpec((seq, w * LANES), lambda b, j: (b, j))],
        out_specs=pl.BlockSpec((seq, w * LANES), lambda b, j: (b, j)),
        out_shape=jax.ShapeDtypeStruct((n, C_HEADS * C_V), bf16),
        compiler_params=_cparams("parallel", "parallel"), name="mla_attn",
    )(qc, kc, vc)


def _mix_kernel(ya_ref, cb_ref, yc_ref, hb_ref, hf_ref, woa_ref, wpw_ref, woc_ref, wg_ref, bg_ref, wout_ref,
                g_ref, b_ref, of_ref, lo_ref, hi_ref):
    d = D_MODEL
    for r0 in range(0, MIX_TILE, SUB_ROWS):
        rows = slice(r0, r0 + SUB_ROWS)
        gates = _sigmoid(jnp.dot(hb_ref[rows, :], wg_ref[...], preferred_element_type=f32) + bg_ref[...])
        mix = (gates[:, 0:d] * jnp.dot(ya_ref[rows, :], woa_ref[...], preferred_element_type=f32)
               + gates[:, d:2 * d] * jnp.dot(cb_ref[rows, :], wpw_ref[...], preferred_element_type=f32)
               + gates[:, 2 * d:3 * d] * jnp.dot(yc_ref[rows, :], woc_ref[...], preferred_element_type=f32))
        z = DN_ALPHA * hf_ref[rows, :] + jnp.dot(mix.astype(bf16), wout_ref[...], preferred_element_type=f32)
        y = _layer_norm(z, g_ref[...], b_ref[...])
        of_ref[rows, :] = y
        lo, hi = _pack_row(y)
        lo_ref[rows, :] = lo
        hi_ref[rows, :] = hi


def mix_layer(ya, cb, yc, hb, hf, woa, wpw, woc, wg, bg, wout, g, b):
    n, d = hf.shape
    t = MIX_TILE
    full = lambda a: pl.BlockSpec(a.shape, lambda i: (0,) * a.ndim)
    rowb = lambda c: pl.BlockSpec((t, c), lambda i: (i, 0))
    return pl.pallas_call(
        _mix_kernel, grid=(n // t,),
        in_specs=[rowb(ya.shape[1]), rowb(cb.shape[1]), rowb(yc.shape[1]), rowb(d), rowb(d),
                  full(woa), full(wpw), full(woc), full(wg), full(bg), full(wout), full(g), full(b)],
        out_specs=[rowb(d), rowb(QUART), rowb(QUART)],
        out_shape=[jax.ShapeDtypeStruct((n, d), f32), jax.ShapeDtypeStruct((n, QUART), i32),
                   jax.ShapeDtypeStruct((n, QUART), i32)],
        compiler_params=_cparams("parallel"), name="mix_layer",
    )(ya, cb, yc, hb, hf, woa, wpw, woc, wg, bg, wout, g, b)


def _router_kernel(h_ref, wr_ref, br_ref, idx_ref, gate_ref, rank_ref, cnt_ref, base):
    i = pl.program_id(0)
    t = ROUTER_TILE
    e = N_EXPERTS

    @pl.when(i == 0)
    def _():
        base[...] = jnp.zeros_like(base)

    def split(x):
        hi = x.astype(bf16)
        return hi, (x - hi.astype(f32)).astype(bf16)

    nt = lambda a, b: lax.dot_general(a, b, (((1,), (1,)), ((), ())), preferred_element_type=f32)
    w_hi, w_lo = split(wr_ref[...])
    h_hi, h_lo = split(h_ref[...])
    logits = nt(w_hi, h_hi) + nt(w_lo, h_hi) + nt(w_hi, h_lo) + br_ref[...]
    row = lax.broadcasted_iota(i32, (e, t), 0).astype(f32)
    vals, hots = [], []
    cur = logits
    for k in range(TOP_K):
        m = jnp.max(cur, axis=0, keepdims=True)
        first = jnp.min(jnp.where(cur == m, row, float(e)), axis=0, keepdims=True)
        hot = row == first
        cur = jnp.where(hot, -jnp.inf, cur)
        vals.append(m)
        hots.append(hot)
        idx_ref[k:k + 1, :] = first.astype(i32)
    ex = [jnp.exp(v - vals[0]) for v in vals]
    den = ex[0] + ex[1] + ex[2] + ex[3]
    for k in range(TOP_K):
        gate_ref[k:k + 1, :] = ex[k] / den
    onehot = jnp.concatenate([jnp.where(h, 1.0, 0.0) for h in hots], axis=0)
    r = lax.broadcasted_iota(i32, (t, t), 0)
    c = lax.broadcasted_iota(i32, (t, t), 1)
    upper = jnp.where(r <= c, 1.0, 0.0).astype(bf16)
    prefix = jnp.dot(onehot.astype(bf16), upper, preferred_element_type=f32)
    counts = jnp.sum(onehot, axis=1, keepdims=True)
    offset = base[:, 0:1]
    for k in range(TOP_K):
        sel = jnp.where(hots[k], prefix[k * e:(k + 1) * e, :] - 1.0 + offset, 0.0)
        rank_ref[k:k + 1, :] = jnp.sum(sel, axis=0, keepdims=True).astype(i32)
        offset = offset + counts[k * e:(k + 1) * e, :]
    base[...] = jnp.broadcast_to(offset, base.shape)
    cnt_ref[...] = base[...]


def router(hf, w_rt, b_r):
    n, d = hf.shape
    t = ROUTER_TILE
    tok = pl.BlockSpec((TOP_K, t), lambda i: (0, i))
    return pl.pallas_call(
        _router_kernel, grid=(n // t,),
        in_specs=[pl.BlockSpec((t, d), lambda i: (i, 0)), pl.BlockSpec((N_EXPERTS, d), lambda i: (0, 0)),
                  pl.BlockSpec((N_EXPERTS, 1), lambda i: (0, 0))],
        out_specs=[tok, tok, tok, pl.BlockSpec((N_EXPERTS, LANES), lambda i: (0, 0))],
        out_shape=[jax.ShapeDtypeStruct((TOP_K, n), i32), jax.ShapeDtypeStruct((TOP_K, n), f32),
                   jax.ShapeDtypeStruct((TOP_K, n), i32), jax.ShapeDtypeStruct((N_EXPERTS, LANES), f32)],
        scratch_shapes=[pltpu.VMEM((N_EXPERTS, LANES), f32)],
        compiler_params=_cparams("arbitrary"), name="router",
    )(hf, w_rt, b_r)


def _dest_kernel(idx_ref, rank_ref, start_ref, dest_ref):
    t = idx_ref.shape[1]
    row = lax.broadcasted_iota(i32, (N_EXPERTS, t), 0)
    for k in range(TOP_K):
        hot = row == idx_ref[k:k + 1, :]
        off = jnp.sum(jnp.where(hot, start_ref[...], 0.0), axis=0, keepdims=True)
        dest_ref[k:k + 1, :] = rank_ref[k:k + 1, :] + off.astype(i32)


def dest_rows(idx, rank, pad_start):
    n = idx.shape[1]
    t = min(DEST_TILE, n)
    tok = pl.BlockSpec((TOP_K, t), lambda i: (0, i))
    return pl.pallas_call(
        _dest_kernel, grid=(n // t,),
        in_specs=[tok, tok, pl.BlockSpec((N_EXPERTS, 1), lambda i: (0, 0))],
        out_specs=tok, out_shape=jax.ShapeDtypeStruct((TOP_K, n), i32),
        compiler_params=_cparams("parallel"), name="dest_rows",
    )(idx, rank, pad_start)


def _sc_mesh():
    return plsc.VectorSubcoreMesh(core_axis_name="c", subcore_axis_name="s")


def sc_scatter_rows(x_lo, x_hi, dest, n_rows):
    n, d = x_lo.shape
    kk = dest.shape[0]
    out = jax.ShapeDtypeStruct((n_rows, d), x_lo.dtype)

    @functools.partial(pl.kernel, out_type=(out, out), mesh=_sc_mesh(), scratch_types=[])
    def k(lo_hbm, hi_hbm, i_hbm, olo_hbm, ohi_hbm):
        for x_hbm, o_hbm in ((lo_hbm, olo_hbm), (hi_hbm, ohi_hbm)):
            def body(x_vmem, i_vmem, o_hbm=o_hbm):
                for j in range(kk):
                    pltpu.sync_copy(x_vmem, o_hbm.at[i_vmem.at[j]])

            pltpu.emit_pipeline(
                body, grid=(n // SC_WINDOW,),
                in_specs=[pl.BlockSpec((SC_WINDOW, d), lambda i: (i, 0)),
                          pl.BlockSpec((kk, SC_WINDOW), lambda i: (0, i))],
                out_specs=[], core_axis_name=("c", "s"), dimension_semantics=(pltpu.PARALLEL,),
            )(x_hbm, i_hbm)

    return k(x_lo, x_hi, dest)


def sc_gather_rows(t_lo, t_hi, idx):
    m = idx.shape[1]
    d = t_lo.shape[1]
    out = jax.ShapeDtypeStruct((m, d), t_lo.dtype)

    @functools.partial(pl.kernel, out_type=(out, out), mesh=_sc_mesh(), scratch_types=[])
    def k(lo_hbm, hi_hbm, i_hbm, olo_hbm, ohi_hbm):
        for t_hbm, o_hbm in ((lo_hbm, olo_hbm), (hi_hbm, ohi_hbm)):
            def body(i_vmem, o_vmem, t_hbm=t_hbm):
                pltpu.sync_copy(t_hbm.at[i_vmem.at[0]], o_vmem)

            pltpu.emit_pipeline(
                body, grid=(m // SC_WINDOW,),
                in_specs=[pl.BlockSpec((1, SC_WINDOW), lambda i: (0, i))],
                out_specs=[pl.BlockSpec((SC_WINDOW, d), lambda i: (i, 0))],
                core_axis_name=("c", "s"), dimension_semantics=(pltpu.PARALLEL,),
            )(i_hbm, o_hbm)

    return k(t_lo, t_hi, idx)


def _expert_kernel(be_ref, nu_ref, xlo_ref, xhi_ref, w1_ref, b1_ref, w2_ref, b2_ref, ylo_ref, yhi_ref, w1b, w2b):
    i = pl.program_id(0)
    new_expert = jnp.logical_or(i == 0, be_ref[i] != be_ref[jnp.maximum(i - 1, 0)])

    @pl.when(jnp.logical_and(i < nu_ref[0], new_expert))
    def _():
        w1b[...] = w1_ref[0, 0].astype(bf16)
        w2b[...] = w2_ref[0, 0].astype(bf16)

    @pl.when(i < nu_ref[0])
    def _():
        for r0 in range(0, MOE_BLK, SUB_ROWS):
            rows = slice(r0, r0 + SUB_ROWS)
            lo = xlo_ref[rows, :]
            hi = xhi_ref[rows, :]
            x = jnp.concatenate([_unpack_lo(lo).astype(bf16), _unpack_lo(hi).astype(bf16),
                                 _unpack_hi(lo).astype(bf16), _unpack_hi(hi).astype(bf16)], axis=1)
            u = jnp.dot(x, w1b[...], preferred_element_type=f32) + b1_ref[0, 0]
            glu = jnp.minimum(u[:, :D_EXPERT], SWIGLU_LIMIT)
            lin = jnp.clip(u[:, D_EXPERT:], -SWIGLU_LIMIT, SWIGLU_LIMIT)
            act = (glu * _sigmoid(SWIGLU_ALPHA * glu) * (lin + 1.0)).astype(bf16)
            y = jnp.dot(act, w2b[...], preferred_element_type=f32) + b2_ref[0, 0]
            ylo, yhi = _pack_row(y)
            ylo_ref[rows, :] = ylo
            yhi_ref[rows, :] = yhi


def expert_ffn(layer, block_expert, n_used, xs_lo, xs_hi, w1, b1, w2, b2):
    n_rows = xs_lo.shape[0]
    nb = n_rows // MOE_BLK
    rows = pl.BlockSpec((MOE_BLK, QUART), lambda i, be, nu: (jnp.minimum(i, nu[0] - 1), 0))
    wsel = lambda shape: pl.BlockSpec((1, 1) + shape, lambda i, be, nu: (layer, be[i], 0, 0))
    out = jax.ShapeDtypeStruct((n_rows, QUART), i32)
    return pl.pallas_call(
        _expert_kernel,
        grid_spec=pltpu.PrefetchScalarGridSpec(
            num_scalar_prefetch=2, grid=(nb,),
            in_specs=[rows, rows, wsel((D_MODEL, 2 * D_EXPERT)), wsel((1, 2 * D_EXPERT)),
                      wsel((D_EXPERT, D_MODEL)), wsel((1, D_MODEL))],
            out_specs=[rows, rows],
            scratch_shapes=[pltpu.VMEM((D_MODEL, 2 * D_EXPERT), bf16), pltpu.VMEM((D_EXPERT, D_MODEL), bf16)]),
        out_shape=[out, out],
        compiler_params=_cparams("arbitrary"), name="expert_ffn",
    )(block_expert, n_used, xs_lo, xs_hi, w1, b1, w2, b2)


def _combine_kernel(ylo_ref, yhi_ref, gt_ref, h1_ref, g_ref, b_ref, of_ref):
    z = DN_ALPHA * h1_ref[...] + _moe_sum(ylo_ref, yhi_ref, gt_ref, slice(None))
    of_ref[...] = _layer_norm(z, g_ref[...], b_ref[...])


def combine(yk_lo, yk_hi, gates_t, h1f, g, b):
    n, d = h1f.shape
    t = ROW_TILE
    ysp = pl.BlockSpec((TOP_K, t, QUART), lambda i: (0, i, 0))
    rowb = lambda c: pl.BlockSpec((t, c), lambda i: (i, 0))
    vec = pl.BlockSpec((1, d), lambda i: (0, 0))
    return pl.pallas_call(
        _combine_kernel, grid=(n // t,),
        in_specs=[ysp, ysp, rowb(TOP_K), rowb(d), vec, vec],
        out_specs=rowb(d), out_shape=jax.ShapeDtypeStruct((n, d), f32),
        compiler_params=_cparams("parallel"), name="combine",
    )(yk_lo, yk_hi, gates_t, h1f, g.reshape(1, d), b.reshape(1, d))


def _rope_tables(positions):
    inv = ROPE_THETA ** (-jnp.arange(0, C_ROPE, 2, dtype=f32) / C_ROPE)
    ang = positions.reshape(-1).astype(f32)[:, None] * inv
    cos, sin = jnp.cos(ang), jnp.sin(ang)
    pad = jnp.zeros((ang.shape[0], LANES - C_ROPE), f32)
    return jnp.concatenate([cos, cos, pad], axis=1), jnp.concatenate([-sin, sin, pad], axis=1)


def _attn_a_bias(rel_tables):
    depth = rel_tables.shape[0]
    period = A_BAND + A_QBLK - 1
    u = jnp.arange(period)
    diag = jnp.clip(u - (A_QBLK - 1) - A_LEFT, -A_MAX_REL, A_MAX_REL) + A_MAX_REL
    e = rel_tables.astype(f32)[:, :, diag]
    pitch = period + 1
    tiled = jnp.tile(e, (1, 1, A_QBLK + 1))[:, :, :A_QBLK * pitch]
    rows = tiled.reshape(depth, A_HEADS, A_QBLK, pitch)[:, :, :, :A_BAND]
    bias = rows[:, :, ::-1, :]
    r = jnp.arange(A_QBLK)[:, None]
    c = jnp.arange(A_BAND)[None, :]
    own = c - CHUNK * (r // CHUNK)
    valid = (own >= 0) & (own < A_LEFT + CHUNK)
    bias = jnp.where(valid[None, None], bias * LOG2E, NEG_INF)
    return bias.reshape(depth, A_HEADS // 2, 2 * A_QBLK, A_BAND)


def _layout_w_uq(w_uq):
    w = w_uq.reshape(C_Q_RANK, C_HEADS, C_QK)
    return jnp.concatenate([w[:, :, :C_NOPE].reshape(C_Q_RANK, -1), w[:, :, C_NOPE:].reshape(C_Q_RANK, -1)], axis=1)


def _layout_w_ukv(w_ukv):
    w = w_ukv.reshape(C_KV_RANK, C_HEADS, C_NOPE + C_V)
    return jnp.concatenate([w[:, :, :C_NOPE].reshape(C_KV_RANK, -1), w[:, :, C_NOPE:].reshape(C_KV_RANK, -1)], axis=1)


def _moe(layer, hf, lo, hi, w_rt, b_r, w1, b1, w2, b2):
    n = hf.shape[0]
    n_rows = n * TOP_K + N_EXPERTS * MOE_BLK
    nb = n_rows // MOE_BLK
    idx, gates, rank, cnt = router(hf, w_rt, b_r)
    counts = cnt[:, 0].astype(i32)
    padded = (counts + MOE_BLK - 1) // MOE_BLK * MOE_BLK
    pad_end = jnp.cumsum(padded)
    pad_start = (pad_end - padded).astype(f32).reshape(N_EXPERTS, 1)
    block_start = jnp.arange(nb, dtype=i32) * MOE_BLK
    block_expert = jnp.minimum(jnp.sum((pad_end[None, :] <= block_start[:, None]).astype(i32), axis=1),
                               N_EXPERTS - 1)
    n_used = (pad_end[-1:] // MOE_BLK).astype(i32)
    dest = dest_rows(idx, rank, pad_start)
    xs_lo, xs_hi = sc_scatter_rows(lo, hi, dest, n_rows)
    ys_lo, ys_hi = expert_ffn(layer, block_expert, n_used, xs_lo, xs_hi, w1, b1, w2, b2)
    yk_lo, yk_hi = sc_gather_rows(ys_lo, ys_hi, dest.reshape(1, TOP_K * n))
    return yk_lo.reshape(TOP_K, n, QUART), yk_hi.reshape(TOP_K, n, QUART), gates.T


def kernel(x, positions, ln_in_g, ln_in_b, w_in, w_gate, b_gate, rel_bias, conv_w, conv_b, conv_ln_g, conv_ln_b, w_pw2, q_norm_g, kv_norm_g, w_uq, w_ukv, w_oa, w_oc, w_out, ln1_g, ln1_b, w_router, b_router, w1, b1, w2, b2, ln2_g, ln2_b):
    batch, seq, d = x.shape
    n = batch * seq
    cosm, sinm = _rope_tables(positions)
    a_bias = _attn_a_bias(rel_bias)
    ab = 3 * A_WIDTH + 2 * CONV_CH
    row1 = lambda v: v.reshape(1, -1)
    b1r = b1.reshape(DEPTH, N_EXPERTS, 1, -1)
    b2r = b2.reshape(DEPTH, N_EXPERTS, 1, -1)
    moe_out = None
    for l in range(DEPTH):
        w_ab = w_in[l, :, :ab].astype(bf16)
        w_down = jnp.pad(w_in[l, :, ab:], ((0, 0), (0, LANES - C_ROPE))).astype(bf16)
        if l == 0:
            hf, hb, qkvg = entry_first(x.reshape(n, d), ln_in_g, ln_in_b, w_ab)
        else:
            hf, hb, qkvg = entry_moe(*moe_out, h1f, ln2_g[l - 1], ln2_b[l - 1], w_ab)
        qc, kc, vc = mla_prep(hb, w_down, row1(q_norm_g[l]), row1(kv_norm_g[l]),
                              _layout_w_uq(w_uq[l]).astype(bf16), _layout_w_ukv(w_ukv[l]).astype(bf16), cosm, sinm)
        cb = conv_module(qkvg, conv_w[l], conv_b[l], conv_ln_g[l], conv_ln_b[l], batch, seq)
        ya = attn_a(qkvg, a_bias[l], batch, seq)
        yc = mla_attn_pairs(qc, kc, vc, batch, seq)
        h1f, lo, hi = mix_layer(ya, cb, yc, hb, hf, w_oa[l].astype(bf16), w_pw2[l].astype(bf16),
                                w_oc[l].astype(bf16), w_gate[l].astype(bf16), row1(b_gate[l]),
                                w_out[l].astype(bf16), row1(ln1_g[l]), row1(ln1_b[l]))
        moe_out = _moe(l, h1f, lo, hi, w_router[l].T, b_router[l].reshape(N_EXPERTS, 1), w1, b1r, w2, b2r)
    return combine(*moe_out, h1f, ln2_g[DEPTH - 1], ln2_b[DEPTH - 1]).reshape(batch, seq, d)
```

```python
import functools

import jax
import jax.numpy as jnp
from jax import lax
from jax.experimental import pallas as pl
from jax.experimental.pallas import tpu as pltpu
from jax.experimental.pallas import tpu_sc as plsc

f32 = jnp.float32
bf16 = jnp.bfloat16
i32 = jnp.int32

D_MODEL = 1024
DEPTH = 4
CHUNK = 64
A_HEADS = 8
A_HEAD_DIM = 64
A_WIDTH = A_HEADS * A_HEAD_DIM
A_LEFT = 8 * CHUNK
A_MAX_REL = 128
CONV_CH = 512
CONV_WIDTH = 31
C_HEADS = 8
C_NOPE = 64
C_ROPE = 32
C_V = 64
C_QK = C_NOPE + C_ROPE
C_Q_RANK = 384
C_KV_RANK = 256
ROPE_THETA = 10000.0
N_EXPERTS = 32
TOP_K = 4
D_EXPERT = 1024
SWIGLU_ALPHA = 1.702
SWIGLU_LIMIT = 7.0
DN_ALPHA = (2 * DEPTH) ** 0.25
LN_EPS = 1e-5
RMS_EPS = 1e-6
NEG_INF = -1e30
LOG2E = 1.4426950408889634

LANES = 128
SUBLANES = 8
MXU_DIM = 256
V7X_VMEM_BYTES = 64 * 1024 * 1024
BF16_BITS = 16
HI_HALF_MASK = -(1 << BF16_BITS)
A_QBLK = 2 * CHUNK
A_STEP = 8 * A_QBLK
A_BAND = A_LEFT + A_QBLK
C_BLK = MXU_DIM
C_STEP_PAIRS = C_HEADS // 2
C_SLOT = 2 * LANES
CONV_BLK = 256
CONV_ROWS = 64
CONV_HALO = 32
MOE_BLK = 512
ROW_TILE = 512
MIX_TILE = 1024
SUB_ROWS = MXU_DIM
ENTRY_SUB_ROWS = 128
ROUTER_TILE = 1024
DEST_TILE = 4096
QUART = D_MODEL // 4
SC_WINDOW = 128
VMEM_LIMIT = V7X_VMEM_BYTES * 7 // 8


def _cparams(*sem):
    return pltpu.CompilerParams(dimension_semantics=tuple(sem), vmem_limit_bytes=VMEM_LIMIT)


def _layer_norm(x, g, b):
    mu = jnp.mean(x, axis=-1, keepdims=True)
    xc = x - mu
    var = jnp.mean(xc * xc, axis=-1, keepdims=True)
    return xc * lax.rsqrt(var + LN_EPS) * g + b


def _sigmoid(x):
    return 1.0 / (1.0 + jnp.exp(-x))


def _pack2(a, b):
    ab = lax.bitcast_convert_type(a.astype(bf16).astype(f32), i32)
    bb = lax.bitcast_convert_type(b.astype(bf16).astype(f32), i32)
    return lax.shift_right_logical(ab, BF16_BITS) | (bb & jnp.int32(HI_HALF_MASK))


def _unpack_lo(w):
    return lax.bitcast_convert_type(lax.shift_left(w, BF16_BITS), f32)


def _unpack_hi(w):
    return lax.bitcast_convert_type(w & jnp.int32(HI_HALF_MASK), f32)


def _pack_row(y):
    return (_pack2(y[:, 0:QUART], y[:, 2 * QUART:3 * QUART]),
            _pack2(y[:, QUART:2 * QUART], y[:, 3 * QUART:4 * QUART]))


def _moe_sum(ylo_ref, yhi_ref, gt_ref, rows):
    gt = gt_ref[rows, :]
    parts = [None] * 4
    for k in range(TOP_K):
        gk = gt[:, k:k + 1]
        lo = ylo_ref[k, rows, :]
        hi = yhi_ref[k, rows, :]
        vals = (_unpack_lo(lo), _unpack_lo(hi), _unpack_hi(lo), _unpack_hi(hi))
        for p in range(4):
            parts[p] = gk * vals[p] if parts[p] is None else parts[p] + gk * vals[p]
    return jnp.concatenate(parts, axis=1)


def _project(y, rows, w_ref, hf_ref, hb_ref, o_ref):
    hf_ref[rows, :] = y
    yb = y.astype(bf16)
    hb_ref[rows, :] = yb
    acc = jnp.dot(yb, w_ref[...], preferred_element_type=f32)
    o_ref[rows, :A_WIDTH] = (acc[:, :A_WIDTH] * (A_HEAD_DIM ** -0.5 * LOG2E)).astype(bf16)
    o_ref[rows, A_WIDTH:] = acc[:, A_WIDTH:].astype(bf16)


def _entry_first_kernel(x_ref, g_ref, b_ref, w_ref, hf_ref, hb_ref, o_ref):
    for r0 in range(0, ROW_TILE, ENTRY_SUB_ROWS):
        rows = slice(r0, r0 + ENTRY_SUB_ROWS)
        _project(_layer_norm(x_ref[rows, :], g_ref[...], b_ref[...]), rows, w_ref, hf_ref, hb_ref, o_ref)


def _entry_moe_kernel(ylo_ref, yhi_ref, gt_ref, h1_ref, g_ref, b_ref, w_ref, hf_ref, hb_ref, o_ref):
    for r0 in range(0, ROW_TILE, ENTRY_SUB_ROWS):
        rows = slice(r0, r0 + ENTRY_SUB_ROWS)
        z = DN_ALPHA * h1_ref[rows, :] + _moe_sum(ylo_ref, yhi_ref, gt_ref, rows)
        _project(_layer_norm(z, g_ref[...], b_ref[...]), rows, w_ref, hf_ref, hb_ref, o_ref)


def _entry_call(body, n, w, row_inputs, row_specs, g, b, name):
    d = D_MODEL
    c = w.shape[1]
    t = ROW_TILE
    vec = pl.BlockSpec((1, d), lambda i: (0, 0))
    rowb = lambda cc: pl.BlockSpec((t, cc), lambda i: (i, 0))
    return pl.pallas_call(
        body, grid=(n // t,),
        in_specs=row_specs + [vec, vec, pl.BlockSpec((d, c), lambda i: (0, 0))],
        out_specs=[rowb(d), rowb(d), rowb(c)],
        out_shape=[jax.ShapeDtypeStruct((n, d), f32), jax.ShapeDtypeStruct((n, d), bf16),
                   jax.ShapeDtypeStruct((n, c), bf16)],
        compiler_params=_cparams("parallel"), name=name,
    )(*row_inputs, g.reshape(1, d), b.reshape(1, d), w)


def entry_first(x2d, g, b, w):
    n, d = x2d.shape
    return _entry_call(_entry_first_kernel, n, w, [x2d], [pl.BlockSpec((ROW_TILE, d), lambda i: (i, 0))], g, b,
                       "entry_first")


def entry_moe(yk_lo, yk_hi, gates_t, h1f, g, b, w):
    n, d = h1f.shape
    t = ROW_TILE
    ysp = pl.BlockSpec((TOP_K, t, QUART), lambda i: (0, i, 0))
    specs = [ysp, ysp, pl.BlockSpec((t, TOP_K), lambda i: (i, 0)), pl.BlockSpec((t, d), lambda i: (i, 0))]
    return _entry_call(_entry_moe_kernel, n, w, [yk_lo, yk_hi, gates_t, h1f], specs, g, b, "entry_moe")


def _mla_prep_kernel(hb_ref, wdown_ref, gq_ref, gkv_ref, wuq_ref, wukv_ref, cos_ref, sin_ref, qc_ref, kc_ref, vc_ref):
    cosm = cos_ref[...]
    sinm = sin_ref[...]
    lane = lax.broadcasted_iota(i32, cosm.shape, 1)
    half = C_ROPE // 2
    scale = C_QK ** -0.5 * LOG2E

    def rope(x):
        swapped = jnp.where(lane < half, pltpu.roll(x, LANES - half, 1), pltpu.roll(x, half, 1))
        return x * cosm + swapped * sinm

    def rms(x, g):
        return (x * lax.rsqrt(jnp.mean(x * x, axis=-1, keepdims=True) + RMS_EPS) * g).astype(bf16)

    down = jnp.dot(hb_ref[...], wdown_ref[...], preferred_element_type=f32)
    kv_lo = C_Q_RANK
    kr_lo = C_Q_RANK + C_KV_RANK
    q = jnp.dot(rms(down[:, :kv_lo], gq_ref[...]), wuq_ref[...], preferred_element_type=f32)
    rope_lo = C_HEADS * C_NOPE
    heads_per_vreg = LANES // C_ROPE
    for h in range(C_HEADS):
        lo = h * C_SLOT
        pair = q[:, (h // 2) * LANES:(h // 2 + 1) * LANES]
        mine = (lane >= C_NOPE) if h % 2 else (lane < C_NOPE)
        qc_ref[:, lo:lo + LANES] = (jnp.where(mine, pair, 0.0) * scale).astype(bf16)
        group = q[:, rope_lo + (h // heads_per_vreg) * LANES:rope_lo + (h // heads_per_vreg + 1) * LANES]
        shift = (h % heads_per_vreg) * C_ROPE
        mine = pltpu.roll(group, LANES - shift, 1) if shift else group
        qc_ref[:, lo + LANES:lo + C_SLOT] = (rope(mine) * scale).astype(bf16)
    kv = jnp.dot(rms(down[:, kv_lo:kr_lo], gkv_ref[...]), wukv_ref[...], preferred_element_type=f32)
    kr = rope(down[:, kr_lo:]).astype(bf16)
    for j in range(C_HEADS // 2):
        lo = j * C_SLOT
        kc_ref[:, lo:lo + LANES] = kv[:, j * LANES:(j + 1) * LANES].astype(bf16)
        kc_ref[:, lo + LANES:lo + C_SLOT] = kr
    vc_ref[...] = kv[:, C_HEADS * C_NOPE:].astype(bf16)


def mla_prep(hb, wdown, gq, gkv, wuq, wukv, cosm, sinm):
    n, d = hb.shape
    t = ROW_TILE
    full = lambda a: pl.BlockSpec(a.shape, lambda i: (0,) * a.ndim)
    rowb = lambda c: pl.BlockSpec((t, c), lambda i: (i, 0))
    qw, kw, vw = C_HEADS * C_SLOT, (C_HEADS // 2) * C_SLOT, C_HEADS * C_V
    return pl.pallas_call(
        _mla_prep_kernel, grid=(n // t,),
        in_specs=[rowb(d), full(wdown), full(gq), full(gkv), full(wuq), full(wukv), rowb(LANES), rowb(LANES)],
        out_specs=[rowb(qw), rowb(kw), rowb(vw)],
        out_shape=[jax.ShapeDtypeStruct((n, qw), bf16), jax.ShapeDtypeStruct((n, kw), bf16),
                   jax.ShapeDtypeStruct((n, vw), bf16)],
        compiler_params=_cparams("parallel"), name="mla_prep",
    )(hb, wdown, gq, gkv, wuq, wukv, cosm, sinm)


def _conv_rows(base, hs, sh, w_ref, cb_ref, lg_ref, lb_ref, o_ref):
    t = CONV_BLK
    sl = SUBLANES
    first = CONV_HALO - CONV_WIDTH + 1
    a_of = [[a for a in range(CONV_HALO // sl + 1) if first <= sl * a + b <= CONV_HALO] for b in range(sl)]
    for b in range(1, sl):
        length = sl * max(a_of[b]) + t
        sh[b - 1, 0:length, :] = hs[base + b:base + b + length, :]
    rows = CONV_ROWS
    for r0 in range(0, t, rows):
        acc = jnp.zeros((rows, CONV_CH), f32) + cb_ref[...]
        for b in range(sl):
            for a in a_of[b]:
                tap = sl * a + b - first
                lo = r0 + sl * a
                src = hs[base + lo:base + lo + rows, :] if b == 0 else sh[b - 1, lo:lo + rows, :]
                acc = acc + src * w_ref[tap:tap + 1, :]
        y = _layer_norm(acc, lg_ref[...], lb_ref[...])
        o_ref[base + r0:base + r0 + rows, :] = (y * _sigmoid(y)).astype(bf16)


def _conv_kernel(a_ref, g_ref, ap_ref, gp_ref, w_ref, cb_ref, lg_ref, lb_ref, o_ref, hs, sh):
    i = pl.program_id(1)
    prev = ap_ref[...].astype(f32) * _sigmoid(gp_ref[...].astype(f32))
    hs[0:CONV_HALO, :] = jnp.where(i > 0, prev, 0.0)
    hs[CONV_HALO:CONV_HALO + ROW_TILE, :] = a_ref[...].astype(f32) * _sigmoid(g_ref[...].astype(f32))
    for base in range(0, ROW_TILE, CONV_BLK):
        _conv_rows(base, hs, sh, w_ref, cb_ref, lg_ref, lb_ref, o_ref)


def conv_module(qkvg, w_dw, b_dw, ln_g, ln_b, batch, seq):
    n = batch * seq
    t = ROW_TILE
    nb = seq // t
    halo_per_tile = t // CONV_HALO
    a_col = 3 * A_WIDTH // CONV_CH
    cur = lambda col: pl.BlockSpec((t, CONV_CH), lambda b, i: (b * nb + i, col))
    halo = lambda col: pl.BlockSpec(
        (CONV_HALO, CONV_CH), lambda b, i: (jnp.maximum((b * nb + i) * halo_per_tile - 1, 0), col))
    vec = pl.BlockSpec((1, CONV_CH), lambda b, i: (0, 0))
    return pl.pallas_call(
        _conv_kernel, grid=(batch, nb),
        in_specs=[cur(a_col), cur(a_col + 1), halo(a_col), halo(a_col + 1),
                  pl.BlockSpec((CONV_WIDTH, CONV_CH), lambda b, i: (0, 0)), vec, vec, vec],
        out_specs=pl.BlockSpec((t, CONV_CH), lambda b, i: (b * nb + i, 0)),
        out_shape=jax.ShapeDtypeStruct((n, CONV_CH), bf16),
        scratch_shapes=[pltpu.VMEM((CONV_HALO + t, CONV_CH), f32),
                        pltpu.VMEM((SUBLANES - 1, CONV_HALO + CONV_BLK, CONV_CH), f32)],
        compiler_params=_cparams("parallel", "parallel"), name="conv_module",
    )(qkvg, qkvg, qkvg, qkvg, w_dw, b_dw.reshape(1, -1), ln_g.reshape(1, -1), ln_b.reshape(1, -1))


def _attn_a_kernel(q_ref, k_ref, v_ref, bias_ref, o_ref, kpad, vpad):
    qi = pl.program_id(1)
    seq = k_ref.shape[0]

    @pl.when(qi == 0)
    def _():
        kpad[0:A_LEFT, :] = jnp.zeros((A_LEFT, A_WIDTH), bf16)
        vpad[0:A_LEFT, :] = jnp.zeros((A_LEFT, A_WIDTH), bf16)
        kpad[A_LEFT:A_LEFT + seq, :] = k_ref[...]
        vpad[A_LEFT:A_LEFT + seq, :] = v_ref[...]

    col = lax.broadcasted_iota(i32, (2 * A_QBLK, A_BAND), 1)
    lane = lax.broadcasted_iota(i32, (A_QBLK, LANES), 1)
    ones = jnp.ones((A_BAND, LANES), bf16)
    for sub in range(A_STEP // A_QBLK):
        start = pl.multiple_of(qi * A_STEP + sub * A_QBLK, A_QBLK)
        reaches_padding = sub * A_QBLK < A_LEFT
        before_start = jnp.where(col + start >= A_LEFT, 0.0, NEG_INF) if reaches_padding else None
        rows = slice(sub * A_QBLK, (sub + 1) * A_QBLK)
        for j in range(A_HEADS // 2):
            cs = slice(j * LANES, (j + 1) * LANES)
            qp = q_ref[rows, cs].astype(f32)
            qs = jnp.concatenate([jnp.where(lane < A_HEAD_DIM, qp, 0.0), jnp.where(lane >= A_HEAD_DIM, qp, 0.0)],
                                 axis=0).astype(bf16)
            kb = kpad[pl.ds(start, A_BAND), cs]
            vb = jnp.concatenate([vpad[pl.ds(start, A_BAND), cs], ones], axis=1)
            s = lax.dot_general(qs, kb, (((1,), (1,)), ((), ())), preferred_element_type=f32)
            s = s + bias_ref[j]
            if reaches_padding:
                s = s + before_start
            p = jnp.exp2((s - jnp.max(s, axis=-1, keepdims=True)).astype(bf16))
            o = jnp.dot(p, vb, preferred_element_type=f32)
            o = o[:, :LANES] / o[:, LANES:]
            o_ref[rows, cs] = jnp.where(lane < A_HEAD_DIM, o[:A_QBLK], o[A_QBLK:]).astype(bf16)


def attn_a(qkvg, bias, batch, seq):
    n = batch * seq
    nq = seq // A_STEP
    return pl.pallas_call(
        _attn_a_kernel, grid=(batch, nq),
        in_specs=[pl.BlockSpec((A_STEP, A_WIDTH), lambda b, i: (b * nq + i, 0)),
                  pl.BlockSpec((seq, A_WIDTH), lambda b, i: (b, 1)),
                  pl.BlockSpec((seq, A_WIDTH), lambda b, i: (b, 2)),
                  pl.BlockSpec(bias.shape, lambda b, i: (0, 0, 0))],
        out_specs=pl.BlockSpec((A_STEP, A_WIDTH), lambda b, i: (b * nq + i, 0)),
        out_shape=jax.ShapeDtypeStruct((n, A_WIDTH), bf16),
        scratch_shapes=[pltpu.VMEM((A_LEFT + seq, A_WIDTH), bf16), pltpu.VMEM((A_LEFT + seq, A_WIDTH), bf16)],
        compiler_params=_cparams("parallel", "arbitrary"), name="attn_a",
    )(qkvg, qkvg, qkvg, bias)


def _mla_pair_kernel(q_ref, k_ref, v_ref, o_ref):
    t = C_BLK
    seq = k_ref.shape[0]
    row = lax.broadcasted_iota(i32, (2 * t, t), 0)
    col = lax.broadcasted_iota(i32, (2 * t, t), 1)
    diag_ok = (col // CHUNK) <= ((row % t) // CHUNK)
    lane = lax.broadcasted_iota(i32, (t, LANES), 1)
    nt = lambda a, b: lax.dot_general(a, b, (((1,), (1,)), ((), ())), preferred_element_type=f32)
    for jj in range(C_STEP_PAIRS):
        kcols = slice(jj * C_SLOT, (jj + 1) * C_SLOT)
        vcols = slice(jj * LANES, (jj + 1) * LANES)
        for r in range(seq // t):
            lo = r * t
            vext = jnp.concatenate([v_ref[0:lo + t, vcols], jnp.ones((lo + t, LANES), bf16)], axis=1)
            q = jnp.concatenate([q_ref[lo:lo + t, (2 * jj) * C_SLOT:(2 * jj + 1) * C_SLOT],
                                 q_ref[lo:lo + t, (2 * jj + 1) * C_SLOT:(2 * jj + 2) * C_SLOT]], axis=0)
            s_diag = jnp.where(diag_ok, nt(q, k_ref[lo:lo + t, kcols]), NEG_INF)
            m = jnp.max(s_diag, axis=-1, keepdims=True)
            if r > 0:
                s_low = nt(q, k_ref[0:lo, kcols])
                m = jnp.maximum(m, jnp.max(s_low, axis=-1, keepdims=True))
                p = jnp.concatenate([jnp.exp2((s_low - m).astype(bf16)), jnp.exp2((s_diag - m).astype(bf16))],
                                    axis=1)
            else:
                p = jnp.exp2((s_diag - m).astype(bf16))
            o = jnp.dot(p, vext, preferred_element_type=f32)
            o = o[:, :LANES] / o[:, LANES:]
            o_ref[lo:lo + t, vcols] = jnp.where(lane < C_V, o[:t], o[t:]).astype(bf16)


def mla_attn_pairs(qc, kc, vc, batch, seq):
    n = batch * seq
    steps = C_HEADS // 2 // C_STEP_PAIRS
    w = C_STEP_PAIRS
    return pl.pallas_call(
        _mla_pair_kernel, grid=(batch, steps),
        in_specs=[pl.BlockSpec((seq, 2 * w * C_SLOT), lambda b, j: (b, j)),
                  pl.BlockSpec((seq, w * C_SLOT), lambda b, j: (b, j)),
                  pl.BlockSpec((seq, w * LANES), lambda b, j: (b, j))],
        out_specs=pl.BlockSpec((seq, w * LANES), lambda b, j: (b, j)),
        out_shape=jax.ShapeDtypeStruct((n, C_HEADS * C_V), bf16),
        compiler_params=_cparams("parallel", "parallel"), name="mla_attn",
    )(qc, kc, vc)


def _mix_kernel(ya_ref, cb_ref, yc_ref, hb_ref, hf_ref, woa_ref, wpw_ref, woc_ref, wg_ref, bg_ref, wout_ref,
                g_ref, b_ref, of_ref, lo_ref, hi_ref):
    d = D_MODEL
    for r0 in range(0, MIX_TILE, SUB_ROWS):
        rows = slice(r0, r0 + SUB_ROWS)
        gates = _sigmoid(jnp.dot(hb_ref[rows, :], wg_ref[...], preferred_element_type=f32) + bg_ref[...])
        mix = (gates[:, 0:d] * jnp.dot(ya_ref[rows, :], woa_ref[...], preferred_element_type=f32)
               + gates[:, d:2 * d] * jnp.dot(cb_ref[rows, :], wpw_ref[...], preferred_element_type=f32)
               + gates[:, 2 * d:3 * d] * jnp.dot(yc_ref[rows, :], woc_ref[...], preferred_element_type=f32))
        z = DN_ALPHA * hf_ref[rows, :] + jnp.dot(mix.astype(bf16), wout_ref[...], preferred_element_type=f32)
        y = _layer_norm(z, g_ref[...], b_ref[...])
        of_ref[rows, :] = y
        lo, hi = _pack_row(y)
        lo_ref[rows, :] = lo
        hi_ref[rows, :] = hi


def mix_layer(ya, cb, yc, hb, hf, woa, wpw, woc, wg, bg, wout, g, b):
    n, d = hf.shape
    t = MIX_TILE
    full = lambda a: pl.BlockSpec(a.shape, lambda i: (0,) * a.ndim, pipeline_mode=pl.Buffered(1))
    rowb = lambda c: pl.BlockSpec((t, c), lambda i: (i, 0))
    return pl.pallas_call(
        _mix_kernel, grid=(n // t,),
        in_specs=[rowb(ya.shape[1]), rowb(cb.shape[1]), rowb(yc.shape[1]), rowb(d), rowb(d),
                  full(woa), full(wpw), full(woc), full(wg), full(bg), full(wout), full(g), full(b)],
        out_specs=[rowb(d), rowb(QUART), rowb(QUART)],
        out_shape=[jax.ShapeDtypeStruct((n, d), f32), jax.ShapeDtypeStruct((n, QUART), i32),
                   jax.ShapeDtypeStruct((n, QUART), i32)],
        compiler_params=_cparams("parallel"), name="mix_layer",
    )(ya, cb, yc, hb, hf, woa, wpw, woc, wg, bg, wout, g, b)


def _router_kernel(h_ref, wr_ref, br_ref, idx_ref, gate_ref, rank_ref, cnt_ref, base):
    i = pl.program_id(0)
    t = ROUTER_TILE
    e = N_EXPERTS

    @pl.when(i == 0)
    def _():
        base[...] = jnp.zeros_like(base)

    def split(x):
        hi = x.astype(bf16)
        return hi, (x - hi.astype(f32)).astype(bf16)

    nt = lambda a, b: lax.dot_general(a, b, (((1,), (1,)), ((), ())), preferred_element_type=f32)
    w_hi, w_lo = split(wr_ref[...])
    h_hi, h_lo = split(h_ref[...])
    logits = nt(w_hi, h_hi) + nt(w_lo, h_hi) + nt(w_hi, h_lo) + br_ref[...]
    row = lax.broadcasted_iota(i32, (e, t), 0).astype(f32)
    vals, hots = [], []
    cur = logits
    for k in range(TOP_K):
        m = jnp.max(cur, axis=0, keepdims=True)
        first = jnp.min(jnp.where(cur == m, row, float(e)), axis=0, keepdims=True)
        hot = row == first
        cur = jnp.where(hot, -jnp.inf, cur)
        vals.append(m)
        hots.append(hot)
        idx_ref[k:k + 1, :] = first.astype(i32)
    ex = [jnp.exp(v - vals[0]) for v in vals]
    den = ex[0] + ex[1] + ex[2] + ex[3]
    for k in range(TOP_K):
        gate_ref[k:k + 1, :] = ex[k] / den
    onehot = jnp.concatenate([jnp.where(h, 1.0, 0.0) for h in hots], axis=0)
    r = lax.broadcasted_iota(i32, (t, t), 0)
    c = lax.broadcasted_iota(i32, (t, t), 1)
    upper = jnp.where(r <= c, 1.0, 0.0).astype(bf16)
    prefix = jnp.dot(onehot.astype(bf16), upper, preferred_element_type=f32)
    counts = jnp.sum(onehot, axis=1, keepdims=True)
    offset = base[:, 0:1]
    for k in range(TOP_K):
        sel = jnp.where(hots[k], prefix[k * e:(k + 1) * e, :] - 1.0 + offset, 0.0)
        rank_ref[k:k + 1, :] = jnp.sum(sel, axis=0, keepdims=True).astype(i32)
        offset = offset + counts[k * e:(k + 1) * e, :]
    base[...] = jnp.broadcast_to(offset, base.shape)
    cnt_ref[...] = base[...]


def router(hf, w_rt, b_r):
    n, d = hf.shape
    t = ROUTER_TILE
    tok = pl.BlockSpec((TOP_K, t), lambda i: (0, i))
    return pl.pallas_call(
        _router_kernel, grid=(n // t,),
        in_specs=[pl.BlockSpec((t, d), lambda i: (i, 0)), pl.BlockSpec((N_EXPERTS, d), lambda i: (0, 0)),
                  pl.BlockSpec((N_EXPERTS, 1), lambda i: (0, 0))],
        out_specs=[tok, tok, tok, pl.BlockSpec((N_EXPERTS, LANES), lambda i: (0, 0))],
        out_shape=[jax.ShapeDtypeStruct((TOP_K, n), i32), jax.ShapeDtypeStruct((TOP_K, n), f32),
                   jax.ShapeDtypeStruct((TOP_K, n), i32), jax.ShapeDtypeStruct((N_EXPERTS, LANES), f32)],
        scratch_shapes=[pltpu.VMEM((N_EXPERTS, LANES), f32)],
        compiler_params=_cparams("arbitrary"), name="router",
    )(hf, w_rt, b_r)


def _dest_kernel(idx_ref, rank_ref, start_ref, dest_ref):
    t = idx_ref.shape[1]
    row = lax.broadcasted_iota(i32, (N_EXPERTS, t), 0)
    for k in range(TOP_K):
        hot = row == idx_ref[k:k + 1, :]
        off = jnp.sum(jnp.where(hot, start_ref[...], 0.0), axis=0, keepdims=True)
        dest_ref[k:k + 1, :] = rank_ref[k:k + 1, :] + off.astype(i32)


def dest_rows(idx, rank, pad_start):
    n = idx.shape[1]
    t = min(DEST_TILE, n)
    tok = pl.BlockSpec((TOP_K, t), lambda i: (0, i))
    return pl.pallas_call(
        _dest_kernel, grid=(n // t,),
        in_specs=[tok, tok, pl.BlockSpec((N_EXPERTS, 1), lambda i: (0, 0))],
        out_specs=tok, out_shape=jax.ShapeDtypeStruct((TOP_K, n), i32),
        compiler_params=_cparams("parallel"), name="dest_rows",
    )(idx, rank, pad_start)


def _sc_mesh():
    return plsc.VectorSubcoreMesh(core_axis_name="c", subcore_axis_name="s")


def sc_scatter_rows(x_lo, x_hi, dest, n_rows):
    n, d = x_lo.shape
    kk = dest.shape[0]
    out = jax.ShapeDtypeStruct((n_rows, d), x_lo.dtype)

    @functools.partial(pl.kernel, out_type=(out, out), mesh=_sc_mesh(), scratch_types=[])
    def k(lo_hbm, hi_hbm, i_hbm, olo_hbm, ohi_hbm):
        for x_hbm, o_hbm in ((lo_hbm, olo_hbm), (hi_hbm, ohi_hbm)):
            def body(x_vmem, i_vmem, o_hbm=o_hbm):
                for j in range(kk):
                    pltpu.sync_copy(x_vmem, o_hbm.at[i_vmem.at[j]])

            pltpu.emit_pipeline(
                body, grid=(n // SC_WINDOW,),
                in_specs=[pl.BlockSpec((SC_WINDOW, d), lambda i: (i, 0)),
                          pl.BlockSpec((kk, SC_WINDOW), lambda i: (0, i))],
                out_specs=[], core_axis_name=("c", "s"), dimension_semantics=(pltpu.PARALLEL,),
            )(x_hbm, i_hbm)

    return k(x_lo, x_hi, dest)


def sc_gather_rows(t_lo, t_hi, idx):
    m = idx.shape[1]
    d = t_lo.shape[1]
    out = jax.ShapeDtypeStruct((m, d), t_lo.dtype)

    @functools.partial(pl.kernel, out_type=(out, out), mesh=_sc_mesh(), scratch_types=[])
    def k(lo_hbm, hi_hbm, i_hbm, olo_hbm, ohi_hbm):
        for t_hbm, o_hbm in ((lo_hbm, olo_hbm), (hi_hbm, ohi_hbm)):
            def body(i_vmem, o_vmem, t_hbm=t_hbm):
                pltpu.sync_copy(t_hbm.at[i_vmem.at[0]], o_vmem)

            pltpu.emit_pipeline(
                body, grid=(m // SC_WINDOW,),
                in_specs=[pl.BlockSpec((1, SC_WINDOW), lambda i: (0, i))],
                out_specs=[pl.BlockSpec((SC_WINDOW, d), lambda i: (i, 0))],
                core_axis_name=("c", "s"), dimension_semantics=(pltpu.PARALLEL,),
            )(i_hbm, o_hbm)

    return k(t_lo, t_hi, idx)


def _expert_kernel(be_ref, nu_ref, xlo_ref, xhi_ref, w1_ref, b1_ref, w2_ref, b2_ref, ylo_ref, yhi_ref, w1b, w2b):
    i = pl.program_id(0)
    new_expert = jnp.logical_or(i == 0, be_ref[i] != be_ref[jnp.maximum(i - 1, 0)])

    @pl.when(jnp.logical_and(i < nu_ref[0], new_expert))
    def _():
        w1b[...] = w1_ref[0, 0].astype(bf16)
        w2b[...] = w2_ref[0, 0].astype(bf16)

    @pl.when(i < nu_ref[0])
    def _():
        for r0 in range(0, MOE_BLK, SUB_ROWS):
            rows = slice(r0, r0 + SUB_ROWS)
            lo = xlo_ref[rows, :]
            hi = xhi_ref[rows, :]
            x = jnp.concatenate([_unpack_lo(lo).astype(bf16), _unpack_lo(hi).astype(bf16),
                                 _unpack_hi(lo).astype(bf16), _unpack_hi(hi).astype(bf16)], axis=1)
            u = jnp.dot(x, w1b[...], preferred_element_type=f32) + b1_ref[0, 0]
            glu = jnp.minimum(u[:, :D_EXPERT], SWIGLU_LIMIT)
            lin = jnp.clip(u[:, D_EXPERT:], -SWIGLU_LIMIT, SWIGLU_LIMIT)
            act = (glu * _sigmoid(SWIGLU_ALPHA * glu) * (lin + 1.0)).astype(bf16)
            y = jnp.dot(act, w2b[...], preferred_element_type=f32) + b2_ref[0, 0]
            ylo, yhi = _pack_row(y)
            ylo_ref[rows, :] = ylo
            yhi_ref[rows, :] = yhi


def expert_ffn(layer, block_expert, n_used, xs_lo, xs_hi, w1, b1, w2, b2):
    n_rows = xs_lo.shape[0]
    nb = n_rows // MOE_BLK
    rows = pl.BlockSpec((MOE_BLK, QUART), lambda i, be, nu: (jnp.minimum(i, nu[0] - 1), 0))
    wsel = lambda shape: pl.BlockSpec((1, 1) + shape, lambda i, be, nu: (layer, be[i], 0, 0))
    out = jax.ShapeDtypeStruct((n_rows, QUART), i32)
    return pl.pallas_call(
        _expert_kernel,
        grid_spec=pltpu.PrefetchScalarGridSpec(
            num_scalar_prefetch=2, grid=(nb,),
            in_specs=[rows, rows, wsel((D_MODEL, 2 * D_EXPERT)), wsel((1, 2 * D_EXPERT)),
                      wsel((D_EXPERT, D_MODEL)), wsel((1, D_MODEL))],
            out_specs=[rows, rows],
            scratch_shapes=[pltpu.VMEM((D_MODEL, 2 * D_EXPERT), bf16), pltpu.VMEM((D_EXPERT, D_MODEL), bf16)]),
        out_shape=[out, out],
        compiler_params=_cparams("arbitrary"), name="expert_ffn",
    )(block_expert, n_used, xs_lo, xs_hi, w1, b1, w2, b2)


def _combine_kernel(ylo_ref, yhi_ref, gt_ref, h1_ref, g_ref, b_ref, of_ref):
    z = DN_ALPHA * h1_ref[...] + _moe_sum(ylo_ref, yhi_ref, gt_ref, slice(None))
    of_ref[...] = _layer_norm(z, g_ref[...], b_ref[...])


def combine(yk_lo, yk_hi, gates_t, h1f, g, b):
    n, d = h1f.shape
    t = ROW_TILE
    ysp = pl.BlockSpec((TOP_K, t, QUART), lambda i: (0, i, 0))
    rowb = lambda c: pl.BlockSpec((t, c), lambda i: (i, 0))
    vec = pl.BlockSpec((1, d), lambda i: (0, 0))
    return pl.pallas_call(
        _combine_kernel, grid=(n // t,),
        in_specs=[ysp, ysp, rowb(TOP_K), rowb(d), vec, vec],
        out_specs=rowb(d), out_shape=jax.ShapeDtypeStruct((n, d), f32),
        compiler_params=_cparams("parallel"), name="combine",
    )(yk_lo, yk_hi, gates_t, h1f, g.reshape(1, d), b.reshape(1, d))


def _rope_tables(positions):
    inv = ROPE_THETA ** (-jnp.arange(0, C_ROPE, 2, dtype=f32) / C_ROPE)
    ang = positions.reshape(-1).astype(f32)[:, None] * inv
    cos, sin = jnp.cos(ang), jnp.sin(ang)
    pad = jnp.zeros((ang.shape[0], LANES - C_ROPE), f32)
    return jnp.concatenate([cos, cos, pad], axis=1), jnp.concatenate([-sin, sin, pad], axis=1)


def _attn_a_bias(rel_tables):
    depth = rel_tables.shape[0]
    period = A_BAND + A_QBLK - 1
    u = jnp.arange(period)
    diag = jnp.clip(u - (A_QBLK - 1) - A_LEFT, -A_MAX_REL, A_MAX_REL) + A_MAX_REL
    e = rel_tables.astype(f32)[:, :, diag]
    pitch = period + 1
    tiled = jnp.tile(e, (1, 1, A_QBLK + 1))[:, :, :A_QBLK * pitch]
    rows = tiled.reshape(depth, A_HEADS, A_QBLK, pitch)[:, :, :, :A_BAND]
    bias = rows[:, :, ::-1, :]
    r = jnp.arange(A_QBLK)[:, None]
    c = jnp.arange(A_BAND)[None, :]
    own = c - CHUNK * (r // CHUNK)
    valid = (own >= 0) & (own < A_LEFT + CHUNK)
    bias = jnp.where(valid[None, None], bias * LOG2E, NEG_INF)
    return bias.reshape(depth, A_HEADS // 2, 2 * A_QBLK, A_BAND)


def _layout_w_uq(w_uq):
    w = w_uq.reshape(C_Q_RANK, C_HEADS, C_QK)
    return jnp.concatenate([w[:, :, :C_NOPE].reshape(C_Q_RANK, -1), w[:, :, C_NOPE:].reshape(C_Q_RANK, -1)], axis=1)


def _layout_w_ukv(w_ukv):
    w = w_ukv.reshape(C_KV_RANK, C_HEADS, C_NOPE + C_V)
    return jnp.concatenate([w[:, :, :C_NOPE].reshape(C_KV_RANK, -1), w[:, :, C_NOPE:].reshape(C_KV_RANK, -1)], axis=1)


def _moe(layer, hf, lo, hi, w_rt, b_r, w1, b1, w2, b2):
    n = hf.shape[0]
    n_rows = n * TOP_K + N_EXPERTS * MOE_BLK
    nb = n_rows // MOE_BLK
    idx, gates, rank, cnt = router(hf, w_rt, b_r)
    counts = cnt[:, 0].astype(i32)
    padded = (counts + MOE_BLK - 1) // MOE_BLK * MOE_BLK
    pad_end = jnp.cumsum(padded)
    pad_start = (pad_end - padded).astype(f32).reshape(N_EXPERTS, 1)
    block_start = jnp.arange(nb, dtype=i32) * MOE_BLK
    block_expert = jnp.minimum(jnp.sum((pad_end[None, :] <= block_start[:, None]).astype(i32), axis=1),
                               N_EXPERTS - 1)
    n_used = (pad_end[-1:] // MOE_BLK).astype(i32)
    dest = dest_rows(idx, rank, pad_start)
    xs_lo, xs_hi = sc_scatter_rows(lo, hi, dest, n_rows)
    ys_lo, ys_hi = expert_ffn(layer, block_expert, n_used, xs_lo, xs_hi, w1, b1, w2, b2)
    yk_lo, yk_hi = sc_gather_rows(ys_lo, ys_hi, dest.reshape(1, TOP_K * n))
    return yk_lo.reshape(TOP_K, n, QUART), yk_hi.reshape(TOP_K, n, QUART), gates.T


def kernel(x, positions, ln_in_g, ln_in_b, w_in, w_gate, b_gate, rel_bias, conv_w, conv_b, conv_ln_g, conv_ln_b, w_pw2, q_norm_g, kv_norm_g, w_uq, w_ukv, w_oa, w_oc, w_out, ln1_g, ln1_b, w_router, b_router, w1, b1, w2, b2, ln2_g, ln2_b):
    batch, seq, d = x.shape
    n = batch * seq
    cosm, sinm = _rope_tables(positions)
    a_bias = _attn_a_bias(rel_bias)
    ab = 3 * A_WIDTH + 2 * CONV_CH
    row1 = lambda v: v.reshape(1, -1)
    b1r = b1.reshape(DEPTH, N_EXPERTS, 1, -1)
    b2r = b2.reshape(DEPTH, N_EXPERTS, 1, -1)
    moe_out = None
    for l in range(DEPTH):
        w_ab = w_in[l, :, :ab].astype(bf16)
        w_down = jnp.pad(w_in[l, :, ab:], ((0, 0), (0, LANES - C_ROPE))).astype(bf16)
        if l == 0:
            hf, hb, qkvg = entry_first(x.reshape(n, d), ln_in_g, ln_in_b, w_ab)
        else:
            hf, hb, qkvg = entry_moe(*moe_out, h1f, ln2_g[l - 1], ln2_b[l - 1], w_ab)
        qc, kc, vc = mla_prep(hb, w_down, row1(q_norm_g[l]), row1(kv_norm_g[l]),
                              _layout_w_uq(w_uq[l]).astype(bf16), _layout_w_ukv(w_ukv[l]).astype(bf16), cosm, sinm)
        cb = conv_module(qkvg, conv_w[l], conv_b[l], conv_ln_g[l], conv_ln_b[l], batch, seq)
        ya = attn_a(qkvg, a_bias[l], batch, seq)
        yc = mla_attn_pairs(qc, kc, vc, batch, seq)
        h1f, lo, hi = mix_layer(ya, cb, yc, hb, hf, w_oa[l].astype(bf16), w_pw2[l].astype(bf16),
                                w_oc[l].astype(bf16), w_gate[l].astype(bf16), row1(b_gate[l]),
                                w_out[l].astype(bf16), row1(ln1_g[l]), row1(ln1_b[l]))
        moe_out = _moe(l, h1f, lo, hi, w_router[l].T, b_router[l].reshape(N_EXPERTS, 1), w1, b1r, w2, b2r)
    return combine(*moe_out, h1f, ln2_g[DEPTH - 1], ln2_b[DEPTH - 1]).reshape(batch, seq, d)
```
